```python
import math
import jax, jax.numpy as jnp
from jax import lax
import numpy as np

D_MODEL = 1024
BATCH = 8
SEQ = 2048
DEPTH = 1

MEM_LEN = 256
HG_HEADS = 4
HG_DIM = 128
HG_WIDTH = HG_HEADS * HG_DIM
HG_CHUNK = 64
DA_HEADS = 4
DA_QK_DIM = 64
DA_V_DIM = 2 * DA_QK_DIM
DA_QK_WIDTH = DA_HEADS * 2 * DA_QK_DIM
DA_WIDTH = DA_HEADS * DA_V_DIM
ROPE_DIM = DA_QK_DIM // 4
ROPE_THETA = 500000.0
Q_BLOCK = 128
MIX_WIDTH = HG_WIDTH + DA_WIDTH
XA_HEADS = 4
XA_DIM = D_MODEL // XA_HEADS
N_EXPERTS = 16
EC_FACTOR = 2
D_FF_EXPERT = 2 * D_MODEL
LN_EPS = 1e-5
RMS_EPS = 1e-6
DEEPNORM_ALPHA = (2.0 * DEPTH) ** 0.25
DEEPNORM_BETA = (8.0 * DEPTH) ** -0.25
IN_SIZES = (HG_WIDTH, HG_WIDTH, HG_WIDTH, HG_WIDTH, HG_WIDTH, DA_QK_WIDTH, DA_QK_WIDTH, DA_WIDTH)
IN_COLS = sum(IN_SIZES)
IN_OFFSETS = tuple(sum(IN_SIZES[:i + 1]) for i in range(len(IN_SIZES) - 1))

kernel_name = 'hybrid_hgrn2_diffattn_ecmoe_encoder'


def layer_norm(x, g, b):
    xf = x.astype(jnp.float32)
    mu = jnp.mean(xf, axis=-1, keepdims=True)
    var = jnp.mean(jnp.square(xf - mu), axis=-1, keepdims=True)
    y = (xf - mu) * lax.rsqrt(var + LN_EPS) * g.astype(jnp.float32) + b.astype(jnp.float32)
    return y.astype(x.dtype)


def rms_norm(x, g):
    xf = x.astype(jnp.float32)
    y = xf * lax.rsqrt(jnp.mean(jnp.square(xf), axis=-1, keepdims=True) + RMS_EPS) * g.astype(jnp.float32)
    return y.astype(x.dtype)


def rope_tables(seq_len):
    inv = 1.0 / (ROPE_THETA ** (jnp.arange(0, ROPE_DIM, 2, dtype=jnp.float32) / ROPE_DIM))
    ang = jnp.arange(seq_len, dtype=jnp.float32)[:, None] * inv[None, :]
    return jnp.cos(ang), jnp.sin(ang)


def apply_partial_rope(t, cos, sin):
    rot, rest = t[..., :ROPE_DIM], t[..., ROPE_DIM:]
    r1, r2 = rot[..., :ROPE_DIM // 2], rot[..., ROPE_DIM // 2:]
    c = cos.astype(t.dtype)
    s = sin.astype(t.dtype)
    return jnp.concatenate([r1 * c - r2 * s, r2 * c + r1 * s, rest], axis=-1)


def gla_chunkwise(q, k, v, log_f):
    out_dtype = v.dtype
    B, H, L, dk = q.shape
    dv = v.shape[-1]
    n = L // HG_CHUNK
    rs = lambda t: t.astype(jnp.float32).reshape(B, H, n, HG_CHUNK, t.shape[-1])
    q, k, v, log_f = rs(q), rs(k), rs(v), rs(log_f)
    bcum = jnp.cumsum(log_f, axis=3)
    b_last = bcum[:, :, :, -1:, :]
    q_in = q * jnp.exp(bcum)
    k_in = k * jnp.exp(-bcum)
    k_out = k * jnp.exp(b_last - bcum)
    mask = jnp.tril(jnp.ones((HG_CHUNK, HG_CHUNK), dtype=bool))
    att = jnp.where(mask, jnp.einsum('bhnck,bhnsk->bhncs', q_in, k_in), 0.0)
    o_intra = jnp.einsum('bhncs,bhnsv->bhncv', att, v)
    delta = jnp.einsum('bhnck,bhncv->nbhkv', k_out, v)
    decay = jnp.moveaxis(jnp.exp(b_last[:, :, :, 0, :]), 2, 0)

    def step(S, inp):
        d, dl = inp
        return d[..., None] * S + dl, S

    S0 = jnp.zeros((B, H, dk, dv), jnp.float32)
    _, S_prev = lax.scan(step, S0, (decay, delta))
    o_inter = jnp.einsum('bhnck,nbhkv->bhncv', q_in, S_prev)
    return (o_intra + o_inter).reshape(B, H, L, dv).astype(out_dtype)


def hgrn2_group(u_q, u_i, u_g, u_ff, u_fb, lb_fwd, lb_bwd, norm_g):
    B, L, _ = u_q.shape
    heads = lambda t: t.reshape(B, L, HG_HEADS, HG_DIM).transpose(0, 2, 1, 3)
    q = heads(jax.nn.silu(u_q))
    v = heads(u_i)

    def gates(u_f, lb):
        f = lb + (1.0 - lb) * jax.nn.sigmoid(u_f.astype(jnp.float32))
        return heads((1.0 - f).astype(u_f.dtype)), heads(jnp.log(f))

    k_f, lf_f = gates(u_ff, lb_fwd)
    k_b, lf_b = gates(u_fb, lb_bwd)
    flip = lambda t: jnp.flip(t, axis=2)
    o_f = gla_chunkwise(q, k_f, v, lf_f)
    o_b = flip(gla_chunkwise(flip(q), flip(k_b), flip(v), flip(lf_b)))
    o = (o_f + o_b).transpose(0, 2, 1, 3)
    o = rms_norm(o, norm_g.reshape(HG_HEADS, HG_DIM))
    return o.reshape(B, L, HG_WIDTH) * jax.nn.silu(u_g)


def diff_attention_group(u_q, u_k, u_v, lam, lambda_init, subln_g, cos, sin):
    B, L, _ = u_q.shape
    q = u_q.reshape(B, L, 2 * DA_HEADS, DA_QK_DIM).transpose(0, 2, 1, 3)
    k = u_k.reshape(B, L, 2 * DA_HEADS, DA_QK_DIM).transpose(0, 2, 1, 3)
    v = u_v.reshape(B, L, DA_HEADS, DA_V_DIM).transpose(0, 2, 1, 3)
    q = apply_partial_rope(q, cos, sin) * (DA_QK_DIM ** -0.5)
    k = apply_partial_rope(k, cos, sin)
    nb = L // Q_BLOCK
    qb = jnp.moveaxis(q.reshape(B, 2 * DA_HEADS, nb, Q_BLOCK, DA_QK_DIM), 2, 0)

    def block(q_blk):
        s = jnp.einsum('bgqd,bgkd->bgqk', q_blk, k).astype(jnp.float32)
        p = jax.nn.softmax(s, axis=-1).reshape(B, DA_HEADS, 2, Q_BLOCK, L)
        a = (p[:, :, 0] - lam * p[:, :, 1]).astype(v.dtype)
        return jnp.einsum('bhqk,bhkv->bhqv', a, v)

    o = lax.map(block, qb)
    o = jnp.moveaxis(o, 0, 2).reshape(B, DA_HEADS, L, DA_V_DIM)
    o = rms_norm(o, subln_g) * (1.0 - lambda_init)
    return o.transpose(0, 2, 1, 3).reshape(B, L, DA_WIDTH)


def memory_cross_attention(h, mem, wq, wk, wv, wo):
    B, L, _ = h.shape
    M = mem.shape[1]
    q = jnp.einsum('bld,de->ble', h, wq).reshape(B, L, XA_HEADS, XA_DIM)
    k = jnp.einsum('bmd,de->bme', mem, wk).reshape(B, M, XA_HEADS, XA_DIM)
    v = jnp.einsum('bmd,de->bme', mem, wv).reshape(B, M, XA_HEADS, XA_DIM)
    s = jnp.einsum('blhd,bmhd->bhlm', q, k).astype(jnp.float32) * (XA_DIM ** -0.5)
    p = jax.nn.softmax(s, axis=-1).astype(v.dtype)
    o = jnp.einsum('bhlm,bmhd->blhd', p, v).reshape(B, L, D_MODEL)
    return jnp.einsum('bld,de->ble', o, wo)


def expert_choice_moe(h, w_router, w_gate, w_up, w_down):
    B, L, D = h.shape
    cap = EC_FACTOR * L // N_EXPERTS
    aff = jax.nn.softmax(jnp.einsum('bld,de->ble', h, w_router).astype(jnp.float32), axis=-1)
    gates, idx = lax.top_k(jnp.swapaxes(aff, 1, 2), cap)
    xs = jax.vmap(lambda hb, ib: hb[ib])(h, idx)
    a = jax.nn.silu(jnp.einsum('becd,edf->becf', xs, w_gate)) * jnp.einsum('becd,edf->becf', xs, w_up)
    y = jnp.einsum('becf,efd->becd', a, w_down) * gates[..., None].astype(h.dtype)
    return jax.vmap(lambda ib, yb: jnp.zeros((L, D), yb.dtype).at[ib.reshape(-1)].add(yb.reshape(-1, D)))(idx, y)


def setup_inputs(seed: int = 0) -> dict:
    key = jax.random.key(seed)
    ks = jax.random.split(key, 32)
    f32 = jnp.float32
    nrm = lambda k, shape, scale: jax.random.normal(k, shape, f32) * scale
    gain = lambda k, shape: 1.0 + 0.02 * jax.random.normal(k, shape, f32)
    return {
        'x': nrm(ks[0], (BATCH, SEQ, D_MODEL), 1.0),
        'mem': nrm(ks[1], (BATCH, MEM_LEN, D_MODEL), 1.0),
        'emb_ln_g': gain(ks[2], (D_MODEL,)),
        'emb_ln_b': nrm(ks[3], (D_MODEL,), 0.02),
        'w_in': nrm(ks[4], (DEPTH, D_MODEL, IN_COLS), D_MODEL ** -0.5),
        'hg_lb_logits': nrm(ks[5], (2, DEPTH + 1, HG_WIDTH), 0.1),
        'hg_norm_g': gain(ks[6], (DEPTH, HG_WIDTH)),
        'da_lambda_q1': nrm(ks[7], (DEPTH, DA_QK_DIM), 0.1),
        'da_lambda_k1': nrm(ks[8], (DEPTH, DA_QK_DIM), 0.1),
        'da_lambda_q2': nrm(ks[9], (DEPTH, DA_QK_DIM), 0.1),
        'da_lambda_k2': nrm(ks[10], (DEPTH, DA_QK_DIM), 0.1),
        'da_subln_g': gain(ks[11], (DEPTH, DA_V_DIM)),
        'w_mix_out': nrm(ks[12], (DEPTH, MIX_WIDTH, D_MODEL), MIX_WIDTH ** -0.5 * DEEPNORM_BETA),
        'ln1_g': gain(ks[13], (DEPTH, D_MODEL)),
        'ln1_b': nrm(ks[14], (DEPTH, D_MODEL), 0.02),
        'xa_wq': nrm(ks[15], (DEPTH, D_MODEL, D_MODEL), D_MODEL ** -0.5),
        'xa_wk': nrm(ks[16], (DEPTH, D_MODEL, D_MODEL), D_MODEL ** -0.5),
        'xa_wv': nrm(ks[17], (DEPTH, D_MODEL, D_MODEL), D_MODEL ** -0.5),
        'xa_wo': nrm(ks[18], (DEPTH, D_MODEL, D_MODEL), D_MODEL ** -0.5 * DEEPNORM_BETA),
        'ln2_g': gain(ks[19], (DEPTH, D_MODEL)),
        'ln2_b': nrm(ks[20], (DEPTH, D_MODEL), 0.02),
        'w_router': nrm(ks[21], (DEPTH, D_MODEL, N_EXPERTS), D_MODEL ** -0.5),
        'w_gate': nrm(ks[22], (DEPTH, N_EXPERTS, D_MODEL, D_FF_EXPERT), D_MODEL ** -0.5),
        'w_up': nrm(ks[23], (DEPTH, N_EXPERTS, D_MODEL, D_FF_EXPERT), D_MODEL ** -0.5),
        'w_down': nrm(ks[24], (DEPTH, N_EXPERTS, D_FF_EXPERT, D_MODEL), D_FF_EXPERT ** -0.5 * DEEPNORM_BETA),
        'ln3_g': gain(ks[25], (DEPTH, D_MODEL)),
        'ln3_b': nrm(ks[26], (DEPTH, D_MODEL), 0.02),
    }


def reference(x, mem, emb_ln_g, emb_ln_b, w_in, hg_lb_logits, hg_norm_g, da_lambda_q1, da_lambda_k1,
              da_lambda_q2, da_lambda_k2, da_subln_g, w_mix_out, ln1_g, ln1_b, xa_wq, xa_wk, xa_wv, xa_wo,
              ln2_g, ln2_b, w_router, w_gate, w_up, w_down, ln3_g, ln3_b):
    L = x.shape[1]
    cos, sin = rope_tables(L)
    lb_all = jnp.cumsum(jax.nn.softmax(hg_lb_logits.astype(jnp.float32), axis=1), axis=1)
    h = layer_norm(x, emb_ln_g, emb_ln_b)
    for l in range(DEPTH):
        lambda_init = 0.8 - 0.6 * math.exp(-0.3 * l)
        u = jnp.einsum('bld,dc->blc', h, w_in[l])
        hq, hi, hgt, hff, hfb, dq, dk, dv = jnp.split(u, IN_OFFSETS, axis=-1)
        hg_out = hgrn2_group(hq, hi, hgt, hff, hfb, lb_all[0, l], lb_all[1, l], hg_norm_g[l])
        lam = (jnp.exp(jnp.sum(da_lambda_q1[l].astype(jnp.float32) * da_lambda_k1[l].astype(jnp.float32)))
               - jnp.exp(jnp.sum(da_lambda_q2[l].astype(jnp.float32) * da_lambda_k2[l].astype(jnp.float32)))
               + lambda_init)
        da_out = diff_attention_group(dq, dk, dv, lam, lambda_init, da_subln_g[l], cos, sin)
        mix = jnp.einsum('blc,cd->bld', jnp.concatenate([hg_out, da_out], axis=-1), w_mix_out[l])
        h = layer_norm(DEEPNORM_ALPHA * h + mix, ln1_g[l], ln1_b[l])
        xa = memory_cross_attention(h, mem, xa_wq[l], xa_wk[l], xa_wv[l], xa_wo[l])
        h = layer_norm(DEEPNORM_ALPHA * h + xa, ln2_g[l], ln2_b[l])
        moe = expert_choice_moe(h, w_router[l], w_gate[l], w_up[l], w_down[l])
        h = layer_norm(DEEPNORM_ALPHA * h + moe, ln3_g[l], ln3_b[l])
    return h
```

```python
import functools
import math

import jax
import jax.numpy as jnp
from jax import lax
from jax.experimental import pallas as pl
from jax.experimental.pallas import tpu as pltpu

F32 = jnp.float32
BF16 = jnp.bfloat16

HG_HEADS = 4
HG_DIM = 128
HG_CHUNK = 64
DA_HEADS = 4
DA_QK_DIM = 64
DA_V_DIM = 128
ROPE_DIM = DA_QK_DIM // 4
ROPE_THETA = 500000.0
XA_HEADS = 4
N_EXPERTS = 16
EC_FACTOR = 2
LN_EPS = 1e-5
RMS_EPS = 1e-6
DEPTH = 1
DEEPNORM_ALPHA = (2.0 * DEPTH) ** 0.25
LAMBDA_INIT = 0.8 - 0.6 * math.exp(-0.3 * 0)
LOG2_E = math.log2(math.e)

LANES_V7X = 128
VMEM_LIMIT_V7X = 56 * 1024 * 1024


def _cparams(sem):
    return pltpu.CompilerParams(dimension_semantics=sem, vmem_limit_bytes=VMEM_LIMIT_V7X)


def _layer_norm(x, g, b):
    mu = jnp.mean(x, axis=-1, keepdims=True)
    xc = x - mu
    var = jnp.mean(xc * xc, axis=-1, keepdims=True)
    return xc * lax.rsqrt(var + LN_EPS) * g + b


def _silu(x):
    return x * jax.nn.sigmoid(x)


def _dot(a, b):
    return jnp.dot(a, b, preferred_element_type=F32)


def _dot_nt(a, b):
    return lax.dot_general(a, b, (((1,), (1,)), ((), ())), preferred_element_type=F32)


def _dot_tn(a, b):
    return lax.dot_general(a, b, (((0,), (0,)), ((), ())), preferred_element_type=F32)


def _inproj_kernel(x_ref, g_ref, b_ref, w_ref, c_ref, sa_ref, sb_ref,
                   h0_ref, q_ref, v_ref, gt_ref, ff_ref, fb_ref, dq_ref, dk_ref, dv_ref):
    h = _layer_norm(x_ref[...], g_ref[...], b_ref[...])
    h0_ref[...] = h
    hb = h.astype(BF16)
    width = q_ref.shape[1]

    def proj(c):
        return _dot(hb, w_ref[:, c * width:(c + 1) * width])

    def rope(t):
        return (t * c_ref[...] + pltpu.roll(t, width - ROPE_DIM // 2, 1) * sa_ref[...]
                + pltpu.roll(t, ROPE_DIM // 2, 1) * sb_ref[...])

    q_ref[...] = _silu(proj(0)).astype(BF16)
    v_ref[...] = proj(1).astype(BF16)
    gt_ref[...] = _silu(proj(2)).astype(BF16)
    ff_ref[...] = proj(3)
    fb_ref[...] = proj(4)
    dq_ref[...] = (rope(proj(5)) * (DA_QK_DIM ** -0.5 * LOG2_E)).astype(BF16)
    dk_ref[...] = rope(proj(6)).astype(BF16)
    dv_ref[...] = proj(7).T.astype(BF16)


def _rope_lane_tables(seq_len, width):
    half = ROPE_DIM // 2
    inv = 1.0 / (ROPE_THETA ** (jnp.arange(0, ROPE_DIM, 2, dtype=F32) / ROPE_DIM))
    ang = jnp.arange(seq_len, dtype=F32)[:, None] * inv[None, :]
    cos, sin = jnp.cos(ang), jnp.sin(ang)
    pad1 = jnp.ones((seq_len, DA_QK_DIM - ROPE_DIM), F32)
    pad0 = jnp.zeros((seq_len, DA_QK_DIM - ROPE_DIM), F32)
    z = jnp.zeros((seq_len, half), F32)
    reps = width // DA_QK_DIM
    c = jnp.tile(jnp.concatenate([cos, cos, pad1], -1), (1, reps))
    sa = jnp.tile(jnp.concatenate([-sin, z, pad0], -1), (1, reps))
    sb = jnp.tile(jnp.concatenate([z, sin, pad0], -1), (1, reps))
    return c, sa, sb


def _inproj(x2, g, b, w_bf, seq_len, tm):
    n, d = x2.shape
    width = w_bf.shape[1] // 8
    c, sa, sb = _rope_lane_tables(seq_len, width)
    tpb = seq_len // tm
    row = lambda i: (i, 0)
    const = lambda i: (0, 0)
    tab = lambda i: (i % tpb, 0)
    wide = lambda dt: jax.ShapeDtypeStruct((n, width), dt)
    return pl.pallas_call(
        _inproj_kernel,
        grid=(n // tm,),
        in_specs=[pl.BlockSpec((tm, d), row), pl.BlockSpec((1, d), const), pl.BlockSpec((1, d), const),
                  pl.BlockSpec(w_bf.shape, const),
                  pl.BlockSpec((tm, width), tab), pl.BlockSpec((tm, width), tab), pl.BlockSpec((tm, width), tab)],
        out_specs=([pl.BlockSpec((tm, d), row)] + [pl.BlockSpec((tm, width), row)] * 7
                   + [pl.BlockSpec((width, tm), lambda i: (0, i))]),
        out_shape=[jax.ShapeDtypeStruct((n, d), F32), wide(BF16), wide(BF16), wide(BF16), wide(F32), wide(F32),
                   wide(BF16), wide(BF16), jax.ShapeDtypeStruct((width, n), BF16)],
        compiler_params=_cparams(("parallel",)),
        name="inproj",
    )(x2, g, b, w_bf, c, sa, sb)


def _hgrn_kernel(q_ref, v_ref, gt_ref, ff_ref, fb_ref, lbl_ref, ng_ref, o_ref, st_ref, oacc_ref):
    seq_len, width = q_ref.shape
    heads = width // HG_DIM
    ch = HG_CHUNK
    nch = seq_len // ch
    lbl = lbl_ref[...]

    def lower_bound(d):
        l0, l1 = lbl[2 * d:2 * d + 1], lbl[2 * d + 1:2 * d + 2]
        m = jnp.maximum(l0, l1)
        e0, e1 = jnp.exp(l0 - m), jnp.exp(l1 - m)
        return e0 / (e0 + e1)

    lbs = (lower_bound(0), lower_bound(1))
    st_ref[...] = jnp.zeros_like(st_ref)
    oacc_ref[...] = jnp.zeros_like(oacc_ref)
    row = lax.broadcasted_iota(jnp.int32, (ch, width), 0)
    ri = lax.broadcasted_iota(jnp.int32, (ch, ch), 0)
    ci = lax.broadcasted_iota(jnp.int32, (ch, ch), 1)
    masks = (ri >= ci, ri <= ci)

    def cumsum(x, backward):
        s = 1
        while s < ch:
            if backward:
                x = x + jnp.where(row < ch - s, pltpu.roll(x, ch - s, 0), 0.0)
            else:
                x = x + jnp.where(row >= s, pltpu.roll(x, s, 0), 0.0)
            s *= 2
        return x

    def body(n, carry):
        for d in (0, 1):
            c = n if d == 0 else nch - 1 - n
            rows = pl.ds(pl.multiple_of(c * ch, ch), ch)
            uf = (ff_ref if d == 0 else fb_ref)[rows, :]
            f = lbs[d] + (1.0 - lbs[d]) * jax.nn.sigmoid(uf)
            k = 1.0 - f
            bc = cumsum(jnp.log(f), d == 1)
            tot = bc[ch - 1:ch] if d == 0 else bc[0:1]
            q_in = (q_ref[rows, :].astype(F32) * jnp.exp(bc)).astype(BF16)
            k_in = (k * jnp.exp(-bc)).astype(BF16)
            k_out = (k * jnp.exp(tot - bc)).astype(BF16)
            decay = jnp.exp(tot)
            vv = v_ref[rows, :]
            for hh in range(heads):
                sl = slice(hh * HG_DIM, (hh + 1) * HG_DIM)
                att = jnp.where(masks[d], _dot_nt(q_in[:, sl], k_in[:, sl]), 0.0).astype(BF16)
                st = st_ref[d * heads + hh]
                o = _dot(att, vv[:, sl]) + _dot_nt(q_in[:, sl], st.astype(BF16))
                oacc_ref[rows, sl] += o
                st_ref[d * heads + hh] = st * decay[:, sl] + _dot_tn(vv[:, sl], k_out[:, sl])
        return carry

    lax.fori_loop(0, nch, body, 0, unroll=2)

    rb = min(256, seq_len)

    def finish(i, carry):
        rows = pl.ds(pl.multiple_of(i * rb, rb), rb)
        o = oacc_ref[rows, :]
        gate = gt_ref[rows, :].astype(F32)
        ng = ng_ref[...]
        for hh in range(heads):
            sl = slice(hh * HG_DIM, (hh + 1) * HG_DIM)
            oh = o[:, sl]
            ms = jnp.mean(oh * oh, axis=-1, keepdims=True)
            o_ref[rows, sl] = (oh * lax.rsqrt(ms + RMS_EPS) * ng[:, sl] * gate[:, sl]).astype(BF16)
        return carry

    lax.fori_loop(0, seq_len // rb, finish, 0)


def _hgrn(q, v, gt, ff, fb, lb_logits, norm_g, batch, seq_len):
    width = q.shape[1]
    hw = width
    sp = lambda a: a.reshape(batch, seq_len, width)
    blk = pl.BlockSpec((None, seq_len, hw), lambda b, h: (b, 0, h))
    return pl.pallas_call(
        _hgrn_kernel,
        grid=(batch, width // hw),
        in_specs=[blk, blk, blk, blk, blk,
                  pl.BlockSpec((4, hw), lambda b, h: (0, h)), pl.BlockSpec((1, hw), lambda b, h: (0, h))],
        out_specs=blk,
        out_shape=jax.ShapeDtypeStruct((batch, seq_len, width), BF16),
        scratch_shapes=[pltpu.VMEM((2 * (hw // HG_DIM), HG_DIM, HG_DIM), F32), pltpu.VMEM((seq_len, hw), F32)],
        compiler_params=_cparams(("parallel", "parallel")),
        name="hgrn",
    )(sp(q), sp(v), sp(gt), sp(ff), sp(fb), lb_logits.reshape(4, width), norm_g.reshape(1, width))


def _dattn_kernel(q_ref, k_ref, vt_ref, lam_ref, sg_ref, o_ref, s_buf, e_buf, acc_buf, *, kc):
    lv = lam_ref[...]
    lam = (jnp.exp(jnp.sum(lv[0:1] * lv[1:2], axis=-1, keepdims=True))
           - jnp.exp(jnp.sum(lv[2:3] * lv[3:4], axis=-1, keepdims=True)) + LAMBDA_INIT)
    q = q_ref[...]
    lane = lax.broadcasted_iota(jnp.int32, q.shape, 1)
    zero = jnp.zeros_like(q)
    qs = (jnp.where(lane < DA_QK_DIM, q, zero), jnp.where(lane >= DA_QK_DIM, q, zero))
    items = [(j, c) for j in range(k_ref.shape[0] // kc) for c in (0, 1)]
    nbuf = s_buf.shape[0]
    m, l, alpha = [None, None], [None, None], {}

    def scores(i):
        j, c = items[i]
        s_buf[i % nbuf] = _dot_nt(k_ref[j * kc:(j + 1) * kc, :], qs[c])

    def exponentials(i):
        j, c = items[i]
        s = s_buf[i % nbuf]
        mj = jnp.max(s, axis=0, keepdims=True)
        if j == 0:
            m[c], alpha[i] = mj, None
            e = jnp.exp2(s - mj)
            l[c] = jnp.sum(e, axis=0, keepdims=True)
        else:
            mn = jnp.maximum(m[c], mj)
            alpha[i] = jnp.exp2(m[c] - mn)
            e = jnp.exp2(s - mn)
            l[c] = alpha[i] * l[c] + jnp.sum(e, axis=0, keepdims=True)
            m[c] = mn
        e_buf[i % nbuf] = e.astype(BF16)

    def values(i):
        j, c = items[i]
        pv = _dot(vt_ref[:, j * kc:(j + 1) * kc], e_buf[i % nbuf])
        acc_buf[c] = pv if alpha[i] is None else alpha[i] * acc_buf[c] + pv

    scores(0)
    scores(1)
    exponentials(0)
    for i in range(len(items)):
        if i + 2 < len(items):
            scores(i + 2)
        if i + 1 < len(items):
            exponentials(i + 1)
        values(i)
    ot = acc_buf[0] * (1.0 / l[0]) - acc_buf[1] * (lam / l[1])
    ms = jnp.mean(ot * ot, axis=0, keepdims=True)
    y = ot * lax.rsqrt(ms + RMS_EPS) * (sg_ref[...] * (1.0 - LAMBDA_INIT))
    o_ref[...] = y.T.astype(BF16)


def _dattn(dq, dk, dvt, lam_vecs, subln_g, batch, seq_len, tq):
    width = dq.shape[1]
    hw = DA_V_DIM
    sp = lambda a: a.reshape(batch, seq_len, width)
    kk = pl.BlockSpec((None, seq_len, hw), lambda b, h, i: (b, 0, h))
    qo = pl.BlockSpec((None, tq, hw), lambda b, h, i: (b, i, h))
    kc = min(512, seq_len)
    nbuf = 4
    return pl.pallas_call(
        functools.partial(_dattn_kernel, kc=kc),
        grid=(batch, width // hw, seq_len // tq),
        in_specs=[qo, kk, pl.BlockSpec((hw, seq_len), lambda b, h, i: (h, b)),
                  pl.BlockSpec(lam_vecs.shape, lambda b, h, i: (0, 0)),
                  pl.BlockSpec((hw, 1), lambda b, h, i: (0, 0))],
        out_specs=qo,
        out_shape=jax.ShapeDtypeStruct((batch, seq_len, width), BF16),
        scratch_shapes=[pltpu.VMEM((nbuf, kc, tq), F32), pltpu.VMEM((nbuf, kc, tq), BF16),
                        pltpu.VMEM((2, hw, tq), F32)],
        compiler_params=_cparams(("parallel", "parallel", "parallel")),
        name="dattn",
    )(sp(dq), sp(dk), dvt, lam_vecs, subln_g.reshape(hw, 1))


def _mixproj_kernel(hg_ref, da_ref, h0_ref, w_ref, g_ref, b_ref, o_ref):
    half = hg_ref.shape[1]
    mix = _dot(hg_ref[...], w_ref[:half, :]) + _dot(da_ref[...], w_ref[half:, :])
    o_ref[...] = _layer_norm(DEEPNORM_ALPHA * h0_ref[...] + mix, g_ref[...], b_ref[...])


def _mixproj(hg, da, h0, w_bf, g, b, tm):
    n, d = h0.shape
    half = hg.shape[1]
    row = lambda i: (i, 0)
    const = lambda i: (0, 0)
    return pl.pallas_call(
        _mixproj_kernel,
        grid=(n // tm,),
        in_specs=[pl.BlockSpec((tm, half), row), pl.BlockSpec((tm, half), row), pl.BlockSpec((tm, d), row),
                  pl.BlockSpec(w_bf.shape, const), pl.BlockSpec((1, d), const), pl.BlockSpec((1, d), const)],
        out_specs=pl.BlockSpec((tm, d), row),
        out_shape=jax.ShapeDtypeStruct((n, d), F32),
        compiler_params=_cparams(("parallel",)),
        name="mixproj",
    )(hg, da, h0, w_bf, g, b)


def _kvproj_kernel(m_ref, wk_ref, wv_ref, k_ref, v_ref):
    mb = m_ref[...].astype(BF16)
    k_ref[...] = _dot(mb, wk_ref[...]).astype(BF16)
    v_ref[...] = _dot(mb, wv_ref[...]).astype(BF16)


def _kvproj(mem, wk_bf, wv_bf):
    batch, m, d = mem.shape
    blk = pl.BlockSpec((None, m, d), lambda b: (b, 0, 0))
    const = pl.BlockSpec((d, d), lambda b: (0, 0))
    return pl.pallas_call(
        _kvproj_kernel,
        grid=(batch,),
        in_specs=[blk, const, const],
        out_specs=[blk, blk],
        out_shape=[jax.ShapeDtypeStruct((batch, m, d), BF16)] * 2,
        compiler_params=_cparams(("parallel",)),
        name="kvproj",
    )(mem, wk_bf, wv_bf)


def _xattn_kernel(h_ref, k_ref, v_ref, wq_ref, wo_ref, g_ref, b_ref, wr_ref, h2_ref, aff_ref):
    h1 = h_ref[...]
    d = h1.shape[1]
    hd = d // XA_HEADS
    q = (_dot(h1.astype(BF16), wq_ref[...]) * (hd ** -0.5)).astype(BF16)
    outs = []
    for i in range(XA_HEADS):
        sl = slice(i * hd, (i + 1) * hd)
        s = _dot_nt(q[:, sl], k_ref[:, sl])
        e = jnp.exp(s - jnp.max(s, axis=-1, keepdims=True))
        p = (e / jnp.sum(e, axis=-1, keepdims=True)).astype(BF16)
        outs.append(_dot(p, v_ref[:, sl]).astype(BF16))
    xa = _dot(jnp.concatenate(outs, axis=-1), wo_ref[...])
    h2 = _layer_norm(DEEPNORM_ALPHA * h1 + xa, g_ref[...], b_ref[...])
    h2_ref[...] = h2
    logits = _dot_nt(wr_ref[...], h2.astype(BF16))
    e = jnp.exp(logits - jnp.max(logits, axis=0, keepdims=True))
    aff_ref[...] = e / jnp.sum(e, axis=0, keepdims=True)


def _xattn(h1, kx, vx, wq_bf, wo_bf, g, b, wr_t_bf, batch, seq_len, tm):
    n, d = h1.shape
    m = kx.shape[1]
    ne = wr_t_bf.shape[0]
    tiles = seq_len // tm
    row = pl.BlockSpec((tm, d), lambda bi, i: (bi * tiles + i, 0))
    kv = pl.BlockSpec((None, m, d), lambda bi, i: (bi, 0, 0))
    const = lambda shape: pl.BlockSpec(shape, lambda bi, i: (0, 0))
    return pl.pallas_call(
        _xattn_kernel,
        grid=(batch, tiles),
        in_specs=[row, kv, kv, const((d, d)), const((d, d)), const((1, d)), const((1, d)), const((ne, d))],
        out_specs=[row, pl.BlockSpec((None, ne, tm), lambda bi, i: (bi, 0, i))],
        out_shape=[jax.ShapeDtypeStruct((n, d), F32), jax.ShapeDtypeStruct((batch, ne, seq_len), F32)],
        compiler_params=_cparams(("parallel", "parallel")),
        name="xattn",
    )(h1, kx, vx, wq_bf, wo_bf, g, b, wr_t_bf)


def _route_kernel(aff_ref, slot_ref, *, cap):
    aff = aff_ref[...]
    ne, seq_len = aff.shape
    nblk = seq_len // LANES_V7X

    def count(mask):
        return jnp.sum(jnp.where(mask, 1.0, 0.0), axis=-1, keepdims=True)

    bits = jnp.zeros((ne, 1), jnp.int32)
    for bit in range(30, -1, -1):
        cand = bits | (1 << bit)
        bits = jnp.where(count(aff >= pltpu.bitcast(cand, F32)) >= cap, cand, bits)
    thr = pltpu.bitcast(bits, F32)

    ri = lax.broadcasted_iota(jnp.int32, (LANES_V7X, LANES_V7X), 0)
    ci = lax.broadcasted_iota(jnp.int32, (LANES_V7X, LANES_V7X), 1)
    tri = jnp.where(ri <= ci, 1.0, 0.0).astype(BF16)

    def prefix_count(mask):
        m = jnp.where(mask, 1.0, 0.0)
        stack = jnp.concatenate([m[:, j * LANES_V7X:(j + 1) * LANES_V7X] for j in range(nblk)], axis=0)
        within = _dot(stack.astype(BF16), tri)
        pieces, carry = [], jnp.zeros((ne, 1), F32)
        for j in range(nblk):
            blk = within[j * ne:(j + 1) * ne]
            pieces.append(blk + carry)
            carry = carry + blk[:, LANES_V7X - 1:LANES_V7X]
        return jnp.concatenate(pieces, axis=1), m

    gt = aff > thr
    eq = aff == thr
    need = cap - count(gt)
    eq_incl, eq_f = prefix_count(eq)
    sel = gt | (eq & ((eq_incl - eq_f) < need))
    sel_incl, sel_f = prefix_count(sel)
    slot_ref[...] = jnp.where(sel, sel_incl - sel_f, -1.0).astype(jnp.int32)


def _route(aff_t, cap):
    batch, ne, seq_len = aff_t.shape
    blk = pl.BlockSpec((None, ne, seq_len), lambda b: (b, 0, 0))
    return pl.pallas_call(
        functools.partial(_route_kernel, cap=cap),
        grid=(batch,),
        in_specs=[blk],
        out_specs=blk,
        out_shape=jax.ShapeDtypeStruct((batch, ne, seq_len), jnp.int32),
        compiler_params=_cparams(("parallel",)),
        name="route",
    )(aff_t)


def _gather_kernel(slot_ref, aff_ref, h_ref, xs_ref, gate_ref, *, cap):
    hb = h_ref[...].astype(BF16)
    ne, seq_len = slot_ref.shape
    rid = lax.broadcasted_iota(jnp.int32, (cap, seq_len), 0)
    for e in range(ne):
        pick = slot_ref[e:e + 1, :] == rid
        xs_ref[e] = _dot(jnp.where(pick, 1.0, 0.0).astype(BF16), hb).astype(BF16)
        gate_ref[e] = jnp.sum(jnp.where(pick, aff_ref[e:e + 1, :], 0.0), axis=-1, keepdims=True)


def _gather(slots, aff_t, h2, cap):
    batch, ne, seq_len = slots.shape
    d = h2.shape[1]
    blk = pl.BlockSpec((None, ne, seq_len), lambda b: (b, 0, 0))
    return pl.pallas_call(
        functools.partial(_gather_kernel, cap=cap),
        grid=(batch,),
        in_specs=[blk, blk, pl.BlockSpec((seq_len, d), lambda b: (b, 0))],
        out_specs=[pl.BlockSpec((ne, None, cap, d), lambda b: (0, b, 0, 0)),
                   pl.BlockSpec((ne, None, cap, 1), lambda b: (0, b, 0, 0))],
        out_shape=[jax.ShapeDtypeStruct((ne, batch, cap, d), BF16), jax.ShapeDtypeStruct((ne, batch, cap, 1), F32)],
        compiler_params=_cparams(("parallel",)),
        name="gather",
    )(slots, aff_t, h2)


def _ffn_kernel(xs_ref, gate_ref, wg_ref, wu_ref, wd_ref, y_ref, acc_ref):
    c = pl.program_id(1)

    @pl.when(c == 0)
    def _():
        acc_ref[...] = jnp.zeros_like(acc_ref)

    xs = xs_ref[...]
    a = (_silu(_dot(xs, wg_ref[...].astype(BF16))) * _dot(xs, wu_ref[...].astype(BF16))).astype(BF16)
    acc_ref[...] += _dot(a, wd_ref[...].astype(BF16))

    @pl.when(c == pl.num_programs(1) - 1)
    def _():
        y_ref[...] = (acc_ref[...] * gate_ref[...]).astype(BF16)


def _ffn(xs, gates, w_gate, w_up, w_down, fc):
    ne, rows, d = xs.shape
    dff = w_gate.shape[2]
    return pl.pallas_call(
        _ffn_kernel,
        grid=(ne, dff // fc),
        in_specs=[pl.BlockSpec((None, rows, d), lambda e, c: (e, 0, 0)),
                  pl.BlockSpec((None, rows, 1), lambda e, c: (e, 0, 0)),
                  pl.BlockSpec((None, d, fc), lambda e, c: (e, 0, c)),
                  pl.BlockSpec((None, d, fc), lambda e, c: (e, 0, c)),
                  pl.BlockSpec((None, fc, d), lambda e, c: (e, c, 0))],
        out_specs=pl.BlockSpec((None, rows, d), lambda e, c: (e, 0, 0)),
        out_shape=jax.ShapeDtypeStruct((ne, rows, d), BF16),
        scratch_shapes=[pltpu.VMEM((rows, d), F32)],
        compiler_params=_cparams(("parallel", "arbitrary")),
        name="ffn",
    )(xs, gates, w_gate, w_up, w_down)


def _combine_kernel(slot_ref, y_ref, h_ref, g_ref, b_ref, o_ref, *, cap):
    ne, tl = slot_ref.shape
    slot_t = slot_ref[...].astype(F32).T
    lane = lax.broadcasted_iota(jnp.int32, (tl, cap), 1).astype(F32)
    place = jnp.concatenate(
        [jnp.where(slot_t[:, e:e + 1] == lane, 1.0, 0.0).astype(BF16) for e in range(ne)], axis=1)
    moe = _dot(place, y_ref[...].reshape(ne * cap, y_ref.shape[-1]))
    o_ref[...] = _layer_norm(DEEPNORM_ALPHA * h_ref[...] + moe, g_ref[...], b_ref[...])


def _combine(slots, y, h2, g, b, cap, tl):
    batch, ne, seq_len = slots.shape
    d = h2.shape[1]
    tiles = seq_len // tl
    row = pl.BlockSpec((tl, d), lambda bi, i: (bi * tiles + i, 0))
    const = pl.BlockSpec((1, d), lambda bi, i: (0, 0))
    return pl.pallas_call(
        functools.partial(_combine_kernel, cap=cap),
        grid=(batch, tiles),
        in_specs=[pl.BlockSpec((None, ne, tl), lambda bi, i: (bi, 0, i)),
                  pl.BlockSpec((ne, None, cap, d), lambda bi, i: (0, bi, 0, 0)),
                  row, const, const],
        out_specs=row,
        out_shape=jax.ShapeDtypeStruct((batch * seq_len, d), F32),
        compiler_params=_cparams(("parallel", "parallel")),
        name="combine",
    )(slots, y, h2, g, b)


def kernel(x, mem, emb_ln_g, emb_ln_b, w_in, hg_lb_logits, hg_norm_g, da_lambda_q1, da_lambda_k1, da_lambda_q2,
           da_lambda_k2, da_subln_g, w_mix_out, ln1_g, ln1_b, xa_wq, xa_wk, xa_wv, xa_wo, ln2_g, ln2_b, w_router,
           w_gate, w_up, w_down, ln3_g, ln3_b):
    batch, seq_len, d = x.shape
    assert w_in.shape[0] == DEPTH and seq_len % HG_CHUNK == 0 and seq_len % LANES_V7X == 0
    n = batch * seq_len
    cap = EC_FACTOR * seq_len // N_EXPERTS
    vec = lambda a: a.reshape(1, -1)
    bf = lambda a: a.astype(BF16)

    h0, q, v, gt, ff, fb, dq, dk, dv = _inproj(
        x.reshape(n, d), vec(emb_ln_g), vec(emb_ln_b), bf(w_in[0]), seq_len, min(256, seq_len))
    hg = _hgrn(q, v, gt, ff, fb, hg_lb_logits, hg_norm_g[0], batch, seq_len)
    lam_vecs = jnp.stack([da_lambda_q1[0], da_lambda_k1[0], da_lambda_q2[0], da_lambda_k2[0]])
    da = _dattn(dq, dk, dv, lam_vecs, da_subln_g[0], batch, seq_len, min(512, seq_len))
    h1 = _mixproj(hg.reshape(n, -1), da.reshape(n, -1), h0, bf(w_mix_out[0]), ln1_g, ln1_b, min(512, seq_len))

    kx, vx = _kvproj(mem, bf(xa_wk[0]), bf(xa_wv[0]))
    h2, aff_t = _xattn(h1, kx, vx, bf(xa_wq[0]), bf(xa_wo[0]), ln2_g, ln2_b, bf(w_router[0].T),
                       batch, seq_len, min(512, seq_len))

    slots = _route(aff_t, cap)
    xs, gates = _gather(slots, aff_t, h2, cap)
    y = _ffn(xs.reshape(N_EXPERTS, batch * cap, d), gates.reshape(N_EXPERTS, batch * cap, 1),
             w_gate[0], w_up[0], w_down[0], 512)
    out = _combine(slots, y.reshape(N_EXPERTS, batch, cap, d), h2, ln3_g, ln3_b, cap, min(512, seq_len))
    return out.reshape(batch, seq_len, d)
```

```python
import functools
import math
from typing import NamedTuple

import jax
import jax.numpy as jnp
from jax import lax
from jax.experimental import pallas as pl
from jax.experimental.pallas import tpu as pltpu

F32 = jnp.float32
BF16 = jnp.bfloat16

HG_HEADS = 4
HG_DIM = 128
HG_CHUNK = 64
DA_HEADS = 4
DA_QK_DIM = 64
DA_V_DIM = 128
ROPE_DIM = DA_QK_DIM // 4
ROPE_THETA = 500000.0
XA_HEADS = 4
N_EXPERTS = 16
EC_FACTOR = 2
LN_EPS = 1e-5
RMS_EPS = 1e-6
DEPTH = 1
DEEPNORM_ALPHA = (2.0 * DEPTH) ** 0.25
LAMBDA_INIT = 0.8 - 0.6 * math.exp(-0.3 * 0)
LOG2_E = math.log2(math.e)

LANES_V7X = 128
VMEM_LIMIT_V7X = 56 * 1024 * 1024


class _Tiles(NamedTuple):
    inproj_rows: int
    dattn_queries: int
    dattn_keys: int
    mix_rows: int
    xattn_rows: int
    ffn_cols: int
    combine_rows: int


def _tiles(seq_len):
    fit = lambda rows: min(rows, seq_len)
    return _Tiles(inproj_rows=fit(512), dattn_queries=fit(512), dattn_keys=fit(512), mix_rows=fit(1024),
                  xattn_rows=fit(1024), ffn_cols=512, combine_rows=fit(1024))


def _cparams(sem):
    return pltpu.CompilerParams(dimension_semantics=sem, vmem_limit_bytes=VMEM_LIMIT_V7X)


def _layer_norm(x, g, b):
    mu = jnp.mean(x, axis=-1, keepdims=True)
    xc = x - mu
    var = jnp.mean(xc * xc, axis=-1, keepdims=True)
    return xc * lax.rsqrt(var + LN_EPS) * g + b


def _silu(x):
    return x * jax.nn.sigmoid(x)


def _dot(a, b):
    return jnp.dot(a, b, preferred_element_type=F32)


def _dot_nt(a, b):
    return lax.dot_general(a, b, (((1,), (1,)), ((), ())), preferred_element_type=F32)


def _dot_tn(a, b):
    return lax.dot_general(a, b, (((0,), (0,)), ((), ())), preferred_element_type=F32)


def _inproj_kernel(x_ref, g_ref, b_ref, w_ref, c_ref, sa_ref, sb_ref,
                   h0_ref, q_ref, v_ref, gt_ref, ff_ref, fb_ref, dq_ref, dk_ref, dv_ref):
    h = _layer_norm(x_ref[...], g_ref[...], b_ref[...])
    h0_ref[...] = h
    hb = h.astype(BF16)
    width = q_ref.shape[1]

    def proj(c):
        return _dot(hb, w_ref[:, c * width:(c + 1) * width])

    def rope(t):
        return (t * c_ref[...] + pltpu.roll(t, width - ROPE_DIM // 2, 1) * sa_ref[...]
                + pltpu.roll(t, ROPE_DIM // 2, 1) * sb_ref[...])

    q_ref[...] = _silu(proj(0)).astype(BF16)
    v_ref[...] = proj(1).astype(BF16)
    gt_ref[...] = _silu(proj(2)).astype(BF16)
    ff_ref[...] = proj(3)
    fb_ref[...] = proj(4)
    dq_ref[...] = (rope(proj(5)) * (DA_QK_DIM ** -0.5 * LOG2_E)).astype(BF16)
    dk_ref[...] = rope(proj(6)).astype(BF16)
    dv_ref[...] = proj(7).T.astype(BF16)


def _rope_lane_tables(seq_len, width):
    half = ROPE_DIM // 2
    inv = 1.0 / (ROPE_THETA ** (jnp.arange(0, ROPE_DIM, 2, dtype=F32) / ROPE_DIM))
    ang = jnp.arange(seq_len, dtype=F32)[:, None] * inv[None, :]
    cos, sin = jnp.cos(ang), jnp.sin(ang)
    pad1 = jnp.ones((seq_len, DA_QK_DIM - ROPE_DIM), F32)
    pad0 = jnp.zeros((seq_len, DA_QK_DIM - ROPE_DIM), F32)
    z = jnp.zeros((seq_len, half), F32)
    reps = width // DA_QK_DIM
    c = jnp.tile(jnp.concatenate([cos, cos, pad1], -1), (1, reps))
    sa = jnp.tile(jnp.concatenate([-sin, z, pad0], -1), (1, reps))
    sb = jnp.tile(jnp.concatenate([z, sin, pad0], -1), (1, reps))
    return c, sa, sb


def _inproj(x2, g, b, w_bf, seq_len, tm):
    n, d = x2.shape
    width = w_bf.shape[1] // 8
    c, sa, sb = _rope_lane_tables(seq_len, width)
    tpb = seq_len // tm
    row = lambda p, bi: (bi * tpb + p, 0)
    const = lambda p, bi: (0, 0)
    tab = lambda p, bi: (p, 0)
    wide = lambda dt: jax.ShapeDtypeStruct((n, width), dt)
    return pl.pallas_call(
        _inproj_kernel,
        grid=(tpb, n // seq_len),
        in_specs=[pl.BlockSpec((tm, d), row), pl.BlockSpec((1, d), const), pl.BlockSpec((1, d), const),
                  pl.BlockSpec(w_bf.shape, const),
                  pl.BlockSpec((tm, width), tab), pl.BlockSpec((tm, width), tab), pl.BlockSpec((tm, width), tab)],
        out_specs=([pl.BlockSpec((tm, d), row)] + [pl.BlockSpec((tm, width), row)] * 7
                   + [pl.BlockSpec((width, tm), lambda p, bi: (0, bi * tpb + p))]),
        out_shape=[jax.ShapeDtypeStruct((n, d), F32), wide(BF16), wide(BF16), wide(BF16), wide(F32), wide(F32),
                   wide(BF16), wide(BF16), jax.ShapeDtypeStruct((width, n), BF16)],
        compiler_params=_cparams(("parallel", "parallel")),
        name="inproj",
    )(x2, g, b, w_bf, c, sa, sb)


def _hgrn_kernel(q_ref, v_ref, gt_ref, ff_ref, fb_ref, lbl_ref, ng_ref, o_ref, st_ref, oacc_ref):
    seq_len, width = q_ref.shape
    heads = width // HG_DIM
    ch = HG_CHUNK
    nch = seq_len // ch
    lbl = lbl_ref[...]

    def lower_bound(d):
        l0, l1 = lbl[2 * d:2 * d + 1], lbl[2 * d + 1:2 * d + 2]
        m = jnp.maximum(l0, l1)
        e0, e1 = jnp.exp(l0 - m), jnp.exp(l1 - m)
        return e0 / (e0 + e1)

    lbs = (lower_bound(0), lower_bound(1))
    st_ref[...] = jnp.zeros_like(st_ref)
    oacc_ref[...] = jnp.zeros_like(oacc_ref)
    row = lax.broadcasted_iota(jnp.int32, (ch, width), 0)
    ri = lax.broadcasted_iota(jnp.int32, (ch, ch), 0)
    ci = lax.broadcasted_iota(jnp.int32, (ch, ch), 1)
    masks = (ri >= ci, ri <= ci)

    def cumsum(x, backward):
        s = 1
        while s < ch:
            if backward:
                x = x + jnp.where(row < ch - s, pltpu.roll(x, ch - s, 0), 0.0)
            else:
                x = x + jnp.where(row >= s, pltpu.roll(x, s, 0), 0.0)
            s *= 2
        return x

    def body(n, carry):
        for d in (0, 1):
            c = n if d == 0 else nch - 1 - n
            rows = pl.ds(pl.multiple_of(c * ch, ch), ch)
            uf = (ff_ref if d == 0 else fb_ref)[rows, :]
            f = lbs[d] + (1.0 - lbs[d]) * jax.nn.sigmoid(uf)
            k = 1.0 - f
            bc = cumsum(jnp.log(f), d == 1)
            tot = bc[ch - 1:ch] if d == 0 else bc[0:1]
            q_in = (q_ref[rows, :].astype(F32) * jnp.exp(bc)).astype(BF16)
            k_in = (k * jnp.exp(-bc)).astype(BF16)
            k_out = (k * jnp.exp(tot - bc)).astype(BF16)
            decay = jnp.exp(tot)
            vv = v_ref[rows, :]
            for hh in range(heads):
                sl = slice(hh * HG_DIM, (hh + 1) * HG_DIM)
                att = jnp.where(masks[d], _dot_nt(q_in[:, sl], k_in[:, sl]), 0.0).astype(BF16)
                st = st_ref[d * heads + hh]
                o = _dot(att, vv[:, sl]) + _dot_nt(q_in[:, sl], st.astype(BF16))
                oacc_ref[rows, sl] += o
                st_ref[d * heads + hh] = st * decay[:, sl] + _dot_tn(vv[:, sl], k_out[:, sl])
        return carry

    lax.fori_loop(0, nch, body, 0, unroll=2)

    rb = min(256, seq_len)

    def finish(i, carry):
        rows = pl.ds(pl.multiple_of(i * rb, rb), rb)
        o = oacc_ref[rows, :]
        gate = gt_ref[rows, :].astype(F32)
        ng = ng_ref[...]
        for hh in range(heads):
            sl = slice(hh * HG_DIM, (hh + 1) * HG_DIM)
            oh = o[:, sl]
            ms = jnp.mean(oh * oh, axis=-1, keepdims=True)
            o_ref[rows, sl] = (oh * lax.rsqrt(ms + RMS_EPS) * ng[:, sl] * gate[:, sl]).astype(BF16)
        return carry

    lax.fori_loop(0, seq_len // rb, finish, 0)


def _hgrn(q, v, gt, ff, fb, lb_logits, norm_g, batch, seq_len):
    width = q.shape[1]
    hw = width
    sp = lambda a: a.reshape(batch, seq_len, width)
    blk = pl.BlockSpec((None, seq_len, hw), lambda b, h: (b, 0, h))
    return pl.pallas_call(
        _hgrn_kernel,
        grid=(batch, width // hw),
        in_specs=[blk, blk, blk, blk, blk,
                  pl.BlockSpec((4, hw), lambda b, h: (0, h)), pl.BlockSpec((1, hw), lambda b, h: (0, h))],
        out_specs=blk,
        out_shape=jax.ShapeDtypeStruct((batch, seq_len, width), BF16),
        scratch_shapes=[pltpu.VMEM((2 * (hw // HG_DIM), HG_DIM, HG_DIM), F32), pltpu.VMEM((seq_len, hw), F32)],
        compiler_params=_cparams(("parallel", "parallel")),
        name="hgrn",
    )(sp(q), sp(v), sp(gt), sp(ff), sp(fb), lb_logits.reshape(4, width), norm_g.reshape(1, width))


def _dattn_kernel(q_ref, k_ref, vt_ref, lam_ref, sg_ref, o_ref, s_buf, e_buf, acc_buf, *, kc):
    lv = lam_ref[...]
    lam = (jnp.exp(jnp.sum(lv[0:1] * lv[1:2], axis=-1, keepdims=True))
           - jnp.exp(jnp.sum(lv[2:3] * lv[3:4], axis=-1, keepdims=True)) + LAMBDA_INIT)
    q = q_ref[...]
    lane = lax.broadcasted_iota(jnp.int32, q.shape, 1)
    zero = jnp.zeros_like(q)
    qs = (jnp.where(lane < DA_QK_DIM, q, zero), jnp.where(lane >= DA_QK_DIM, q, zero))
    items = [(j, c) for j in range(k_ref.shape[0] // kc) for c in (0, 1)]
    nbuf = s_buf.shape[0]
    m, l, alpha = [None, None], [None, None], {}

    def scores(i):
        j, c = items[i]
        s_buf[i % nbuf] = _dot_nt(k_ref[j * kc:(j + 1) * kc, :], qs[c])

    def exponentials(i):
        j, c = items[i]
        s = s_buf[i % nbuf]
        mj = jnp.max(s, axis=0, keepdims=True)
        if j == 0:
            m[c], alpha[i] = mj, None
            e = jnp.exp2(s - mj)
            l[c] = jnp.sum(e, axis=0, keepdims=True)
        else:
            mn = jnp.maximum(m[c], mj)
            alpha[i] = jnp.exp2(m[c] - mn)
            e = jnp.exp2(s - mn)
            l[c] = alpha[i] * l[c] + jnp.sum(e, axis=0, keepdims=True)
            m[c] = mn
        e_buf[i % nbuf] = e.astype(BF16)

    def values(i):
        j, c = items[i]
        pv = _dot(vt_ref[:, j * kc:(j + 1) * kc], e_buf[i % nbuf])
        acc_buf[c] = pv if alpha[i] is None else alpha[i] * acc_buf[c] + pv

    scores(0)
    scores(1)
    exponentials(0)
    for i in range(len(items)):
        if i + 2 < len(items):
            scores(i + 2)
        if i + 1 < len(items):
            exponentials(i + 1)
        values(i)
    ot = acc_buf[0] * (1.0 / l[0]) - acc_buf[1] * (lam / l[1])
    ms = jnp.mean(ot * ot, axis=0, keepdims=True)
    y = ot * lax.rsqrt(ms + RMS_EPS) * (sg_ref[...] * (1.0 - LAMBDA_INIT))
    o_ref[...] = y.T.astype(BF16)


def _dattn(dq, dk, dvt, lam_vecs, subln_g, batch, seq_len, tq, kc):
    width = dq.shape[1]
    hw = DA_V_DIM
    sp = lambda a: a.reshape(batch, seq_len, width)
    kk = pl.BlockSpec((None, seq_len, hw), lambda b, h, i: (b, 0, h))
    qo = pl.BlockSpec((None, tq, hw), lambda b, h, i: (b, i, h))
    nbuf = 4
    return pl.pallas_call(
        functools.partial(_dattn_kernel, kc=kc),
        grid=(batch, width // hw, seq_len // tq),
        in_specs=[qo, kk, pl.BlockSpec((hw, seq_len), lambda b, h, i: (h, b)),
                  pl.BlockSpec(lam_vecs.shape, lambda b, h, i: (0, 0)),
                  pl.BlockSpec((hw, 1), lambda b, h, i: (0, 0))],
        out_specs=qo,
        out_shape=jax.ShapeDtypeStruct((batch, seq_len, width), BF16),
        scratch_shapes=[pltpu.VMEM((nbuf, kc, tq), F32), pltpu.VMEM((nbuf, kc, tq), BF16),
                        pltpu.VMEM((2, hw, tq), F32)],
        compiler_params=_cparams(("parallel", "parallel", "parallel")),
        name="dattn",
    )(sp(dq), sp(dk), dvt, lam_vecs, subln_g.reshape(hw, 1))


def _mixproj_kernel(hg_ref, da_ref, h0_ref, w_ref, g_ref, b_ref, o_ref):
    half = hg_ref.shape[1]
    mix = _dot(hg_ref[...], w_ref[:half, :]) + _dot(da_ref[...], w_ref[half:, :])
    o_ref[...] = _layer_norm(DEEPNORM_ALPHA * h0_ref[...] + mix, g_ref[...], b_ref[...])


def _mixproj(hg, da, h0, w_bf, g, b, tm):
    n, d = h0.shape
    half = hg.shape[1]
    row = lambda i: (i, 0)
    const = lambda i: (0, 0)
    return pl.pallas_call(
        _mixproj_kernel,
        grid=(n // tm,),
        in_specs=[pl.BlockSpec((tm, half), row), pl.BlockSpec((tm, half), row), pl.BlockSpec((tm, d), row),
                  pl.BlockSpec(w_bf.shape, const), pl.BlockSpec((1, d), const), pl.BlockSpec((1, d), const)],
        out_specs=pl.BlockSpec((tm, d), row),
        out_shape=jax.ShapeDtypeStruct((n, d), F32),
        compiler_params=_cparams(("parallel",)),
        name="mixproj",
    )(hg, da, h0, w_bf, g, b)


def _kvproj_kernel(m_ref, wk_ref, wv_ref, k_ref, v_ref):
    mb = m_ref[...].astype(BF16)
    k_ref[...] = _dot(mb, wk_ref[...]).astype(BF16)
    v_ref[...] = _dot(mb, wv_ref[...]).astype(BF16)


def _kvproj(mem, wk_bf, wv_bf):
    batch, m, d = mem.shape
    blk = pl.BlockSpec((None, m, d), lambda b: (b, 0, 0))
    const = pl.BlockSpec((d, d), lambda b: (0, 0))
    return pl.pallas_call(
        _kvproj_kernel,
        grid=(batch,),
        in_specs=[blk, const, const],
        out_specs=[blk, blk],
        out_shape=[jax.ShapeDtypeStruct((batch, m, d), BF16)] * 2,
        compiler_params=_cparams(("parallel",)),
        name="kvproj",
    )(mem, wk_bf, wv_bf)


def _xattn_kernel(h_ref, k_ref, v_ref, wq_ref, wo_ref, g_ref, b_ref, wr_ref, h2_ref, aff_ref):
    h1 = h_ref[...]
    d = h1.shape[1]
    hd = d // XA_HEADS
    q = (_dot(h1.astype(BF16), wq_ref[...]) * (hd ** -0.5)).astype(BF16)
    outs = []
    for i in range(XA_HEADS):
        sl = slice(i * hd, (i + 1) * hd)
        s = _dot_nt(q[:, sl], k_ref[:, sl])
        e = jnp.exp(s - jnp.max(s, axis=-1, keepdims=True))
        p = (e / jnp.sum(e, axis=-1, keepdims=True)).astype(BF16)
        outs.append(_dot(p, v_ref[:, sl]).astype(BF16))
    xa = _dot(jnp.concatenate(outs, axis=-1), wo_ref[...])
    h2 = _layer_norm(DEEPNORM_ALPHA * h1 + xa, g_ref[...], b_ref[...])
    h2_ref[...] = h2
    logits = _dot_nt(wr_ref[...], h2.astype(BF16))
    e = jnp.exp(logits - jnp.max(logits, axis=0, keepdims=True))
    aff_ref[...] = e / jnp.sum(e, axis=0, keepdims=True)


def _xattn(h1, kx, vx, wq_bf, wo_bf, g, b, wr_t_bf, batch, seq_len, tm):
    n, d = h1.shape
    m = kx.shape[1]
    ne = wr_t_bf.shape[0]
    tiles = seq_len // tm
    row = pl.BlockSpec((tm, d), lambda bi, i: (bi * tiles + i, 0))
    kv = pl.BlockSpec((None, m, d), lambda bi, i: (bi, 0, 0))
    const = lambda shape: pl.BlockSpec(shape, lambda bi, i: (0, 0))
    return pl.pallas_call(
        _xattn_kernel,
        grid=(batch, tiles),
        in_specs=[row, kv, kv, const((d, d)), const((d, d)), const((1, d)), const((1, d)), const((ne, d))],
        out_specs=[row, pl.BlockSpec((None, ne, tm), lambda bi, i: (bi, 0, i))],
        out_shape=[jax.ShapeDtypeStruct((n, d), F32), jax.ShapeDtypeStruct((batch, ne, seq_len), F32)],
        compiler_params=_cparams(("parallel", "parallel")),
        name="xattn",
    )(h1, kx, vx, wq_bf, wo_bf, g, b, wr_t_bf)


def _route_kernel(aff_ref, slot_ref, *, cap):
    aff = aff_ref[...]
    ne, seq_len = aff.shape
    nblk = seq_len // LANES_V7X

    def count(mask):
        return jnp.sum(jnp.where(mask, 1.0, 0.0), axis=-1, keepdims=True)

    bits = jnp.zeros((ne, 1), jnp.int32)
    for bit in range(30, -1, -1):
        cand = bits | (1 << bit)
        bits = jnp.where(count(aff >= pltpu.bitcast(cand, F32)) >= cap, cand, bits)
    thr = pltpu.bitcast(bits, F32)

    ri = lax.broadcasted_iota(jnp.int32, (LANES_V7X, LANES_V7X), 0)
    ci = lax.broadcasted_iota(jnp.int32, (LANES_V7X, LANES_V7X), 1)
    tri = jnp.where(ri <= ci, 1.0, 0.0).astype(BF16)

    def prefix_count(mask):
        m = jnp.where(mask, 1.0, 0.0)
        stack = jnp.concatenate([m[:, j * LANES_V7X:(j + 1) * LANES_V7X] for j in range(nblk)], axis=0)
        within = _dot(stack.astype(BF16), tri)
        pieces, carry = [], jnp.zeros((ne, 1), F32)
        for j in range(nblk):
            blk = within[j * ne:(j + 1) * ne]
            pieces.append(blk + carry)
            carry = carry + blk[:, LANES_V7X - 1:LANES_V7X]
        return jnp.concatenate(pieces, axis=1), m

    gt = aff > thr
    eq = aff == thr
    need = cap - count(gt)
    eq_incl, eq_f = prefix_count(eq)
    sel = gt | (eq & ((eq_incl - eq_f) < need))
    sel_incl, sel_f = prefix_count(sel)
    slot_ref[...] = jnp.where(sel, sel_incl - sel_f, -1.0).astype(jnp.int32)


def _route(aff_t, cap):
    batch, ne, seq_len = aff_t.shape
    blk = pl.BlockSpec((None, ne, seq_len), lambda b: (b, 0, 0))
    return pl.pallas_call(
        functools.partial(_route_kernel, cap=cap),
        grid=(batch,),
        in_specs=[blk],
        out_specs=blk,
        out_shape=jax.ShapeDtypeStruct((batch, ne, seq_len), jnp.int32),
        compiler_params=_cparams(("parallel",)),
        name="route",
    )(aff_t)


def _gather_kernel(slot_ref, aff_ref, h_ref, xs_ref, gate_ref, *, cap):
    hb = h_ref[...].astype(BF16)
    ne, seq_len = slot_ref.shape
    rid = lax.broadcasted_iota(jnp.int32, (cap, seq_len), 0)
    group = 4
    for e0 in range(0, ne, group):
        picks = [slot_ref[e:e + 1, :] == rid for e in range(e0, e0 + group)]
        onehot = jnp.concatenate([jnp.where(p, 1.0, 0.0).astype(BF16) for p in picks], axis=0)
        xs = _dot(onehot, hb).astype(BF16)
        for i, pick in enumerate(picks):
            xs_ref[e0 + i] = xs[i * cap:(i + 1) * cap]
            gate_ref[e0 + i] = jnp.sum(jnp.where(pick, aff_ref[e0 + i:e0 + i + 1, :], 0.0), axis=-1, keepdims=True)


def _gather(slots, aff_t, h2, cap):
    batch, ne, seq_len = slots.shape
    d = h2.shape[1]
    blk = pl.BlockSpec((None, ne, seq_len), lambda b: (b, 0, 0))
    return pl.pallas_call(
        functools.partial(_gather_kernel, cap=cap),
        grid=(batch,),
        in_specs=[blk, blk, pl.BlockSpec((seq_len, d), lambda b: (b, 0))],
        out_specs=[pl.BlockSpec((ne, None, cap, d), lambda b: (0, b, 0, 0)),
                   pl.BlockSpec((ne, None, cap, 1), lambda b: (0, b, 0, 0))],
        out_shape=[jax.ShapeDtypeStruct((ne, batch, cap, d), BF16), jax.ShapeDtypeStruct((ne, batch, cap, 1), F32)],
        compiler_params=_cparams(("parallel",)),
        name="gather",
    )(slots, aff_t, h2)


def _ffn_kernel(xs_ref, gate_ref, wg_ref, wu_ref, wd_ref, y_ref, acc_ref):
    c = pl.program_id(1)

    @pl.when(c == 0)
    def _():
        acc_ref[...] = jnp.zeros_like(acc_ref)

    xs = xs_ref[...]
    a = (_silu(_dot(xs, wg_ref[...].astype(BF16))) * _dot(xs, wu_ref[...].astype(BF16))).astype(BF16)
    acc_ref[...] += _dot(a, wd_ref[...].astype(BF16))

    @pl.when(c == pl.num_programs(1) - 1)
    def _():
        y_ref[...] = (acc_ref[...] * gate_ref[...]).astype(BF16)


def _ffn(xs, gates, w_gate, w_up, w_down, fc):
    ne, rows, d = xs.shape
    dff = w_gate.shape[2]
    return pl.pallas_call(
        _ffn_kernel,
        grid=(ne, dff // fc),
        in_specs=[pl.BlockSpec((None, rows, d), lambda e, c: (e, 0, 0)),
                  pl.BlockSpec((None, rows, 1), lambda e, c: (e, 0, 0)),
                  pl.BlockSpec((None, d, fc), lambda e, c: (e, 0, c)),
                  pl.BlockSpec((None, d, fc), lambda e, c: (e, 0, c)),
                  pl.BlockSpec((None, fc, d), lambda e, c: (e, c, 0))],
        out_specs=pl.BlockSpec((None, rows, d), lambda e, c: (e, 0, 0)),
        out_shape=jax.ShapeDtypeStruct((ne, rows, d), BF16),
        scratch_shapes=[pltpu.VMEM((rows, d), F32)],
        compiler_params=_cparams(("parallel", "arbitrary")),
        name="ffn",
    )(xs, gates, w_gate, w_up, w_down)


def _combine_kernel(slot_ref, y_ref, h_ref, g_ref, b_ref, o_ref, *, cap):
    ne, tl = slot_ref.shape
    slot_t = slot_ref[...].astype(F32).T
    lane = lax.broadcasted_iota(jnp.int32, (tl, cap), 1).astype(F32)
    place = jnp.concatenate(
        [jnp.where(slot_t[:, e:e + 1] == lane, 1.0, 0.0).astype(BF16) for e in range(ne)], axis=1)
    moe = _dot(place, y_ref[...].reshape(ne * cap, y_ref.shape[-1]))
    o_ref[...] = _layer_norm(DEEPNORM_ALPHA * h_ref[...] + moe, g_ref[...], b_ref[...])


def _combine(slots, y, h2, g, b, cap, tl):
    batch, ne, seq_len = slots.shape
    d = h2.shape[1]
    tiles = seq_len // tl
    row = pl.BlockSpec((tl, d), lambda bi, i: (bi * tiles + i, 0))
    const = pl.BlockSpec((1, d), lambda bi, i: (0, 0))
    return pl.pallas_call(
        functools.partial(_combine_kernel, cap=cap),
        grid=(batch, tiles),
        in_specs=[pl.BlockSpec((None, ne, tl), lambda bi, i: (bi, 0, i)),
                  pl.BlockSpec((ne, None, cap, d), lambda bi, i: (0, bi, 0, 0)),
                  row, const, const],
        out_specs=row,
        out_shape=jax.ShapeDtypeStruct((batch * seq_len, d), F32),
        compiler_params=_cparams(("parallel", "parallel")),
        name="combine",
    )(slots, y, h2, g, b)


def kernel(x, mem, emb_ln_g, emb_ln_b, w_in, hg_lb_logits, hg_norm_g, da_lambda_q1, da_lambda_k1, da_lambda_q2,
           da_lambda_k2, da_subln_g, w_mix_out, ln1_g, ln1_b, xa_wq, xa_wk, xa_wv, xa_wo, ln2_g, ln2_b, w_router,
           w_gate, w_up, w_down, ln3_g, ln3_b):
    batch, seq_len, d = x.shape
    assert w_in.shape[0] == DEPTH and seq_len % HG_CHUNK == 0 and seq_len % LANES_V7X == 0
    n = batch * seq_len
    cap = EC_FACTOR * seq_len // N_EXPERTS
    vec = lambda a: a.reshape(1, -1)
    bf = lambda a: a.astype(BF16)

    t = _tiles(seq_len)

    h0, q, v, gt, ff, fb, dq, dk, dv = _inproj(
        x.reshape(n, d), vec(emb_ln_g), vec(emb_ln_b), bf(w_in[0]), seq_len, t.inproj_rows)
    hg = _hgrn(q, v, gt, ff, fb, hg_lb_logits, hg_norm_g[0], batch, seq_len)
    lam_vecs = jnp.stack([da_lambda_q1[0], da_lambda_k1[0], da_lambda_q2[0], da_lambda_k2[0]])
    da = _dattn(dq, dk, dv, lam_vecs, da_subln_g[0], batch, seq_len, t.dattn_queries, t.dattn_keys)
    h1 = _mixproj(hg.reshape(n, -1), da.reshape(n, -1), h0, bf(w_mix_out[0]), ln1_g, ln1_b, t.mix_rows)

    kx, vx = _kvproj(mem, bf(xa_wk[0]), bf(xa_wv[0]))
    h2, aff_t = _xattn(h1, kx, vx, bf(xa_wq[0]), bf(xa_wo[0]), ln2_g, ln2_b, bf(w_router[0].T),
                       batch, seq_len, t.xattn_rows)

    slots = _route(aff_t, cap)
    xs, gates = _gather(slots, aff_t, h2, cap)
    y = _ffn(xs.reshape(N_EXPERTS, batch * cap, d), gates.reshape(N_EXPERTS, batch * cap, 1),
             w_gate[0], w_up[0], w_down[0], t.ffn_cols)
    out = _combine(slots, y.reshape(N_EXPERTS, batch, cap, d), h2, ln3_g, ln3_b, cap, t.combine_rows)
    return out.reshape(batch, seq_len, d)
```

```python
import functools
import math
from typing import NamedTuple

import jax
import jax.numpy as jnp
from jax import lax
from jax.experimental import pallas as pl
from jax.experimental.pallas import tpu as pltpu

F32 = jnp.float32
BF16 = jnp.bfloat16

HG_HEADS = 4
HG_DIM = 128
HG_CHUNK = 64
DA_HEADS = 4
DA_QK_DIM = 64
DA_V_DIM = 128
ROPE_DIM = DA_QK_DIM // 4
ROPE_THETA = 500000.0
XA_HEADS = 4
N_EXPERTS = 16
EC_FACTOR = 2
LN_EPS = 1e-5
RMS_EPS = 1e-6
DEPTH = 1
DEEPNORM_ALPHA = (2.0 * DEPTH) ** 0.25
LAMBDA_INIT = 0.8 - 0.6 * math.exp(-0.3 * 0)
LOG2_E = math.log2(math.e)

LANES_V7X = 128
VMEM_LIMIT_V7X = 56 * 1024 * 1024


class _Tiles(NamedTuple):
    inproj_rows: int
    dattn_queries: int
    dattn_keys: int
    mix_rows: int
    xattn_rows: int
    ffn_cols: int
    combine_rows: int


def _tiles(seq_len):
    fit = lambda rows: min(rows, seq_len)
    return _Tiles(inproj_rows=fit(512), dattn_queries=fit(2048), dattn_keys=fit(512), mix_rows=fit(1024),
                  xattn_rows=fit(1024), ffn_cols=512, combine_rows=fit(1024))


def _cparams(sem):
    return pltpu.CompilerParams(dimension_semantics=sem, vmem_limit_bytes=VMEM_LIMIT_V7X)


def _layer_norm(x, g, b):
    mu = jnp.mean(x, axis=-1, keepdims=True)
    xc = x - mu
    var = jnp.mean(xc * xc, axis=-1, keepdims=True)
    return xc * lax.rsqrt(var + LN_EPS) * g + b


def _silu(x):
    return x * jax.nn.sigmoid(x)


def _dot(a, b):
    return jnp.dot(a, b, preferred_element_type=F32)


def _dot_nt(a, b):
    return lax.dot_general(a, b, (((1,), (1,)), ((), ())), preferred_element_type=F32)


def _dot_tn(a, b):
    return lax.dot_general(a, b, (((0,), (0,)), ((), ())), preferred_element_type=F32)


def _inproj_kernel(x_ref, g_ref, b_ref, w_ref, c_ref, sa_ref, sb_ref,
                   h0_ref, q_ref, v_ref, gt_ref, ff_ref, fb_ref, dq_ref, dk_ref, dv_ref):
    h = _layer_norm(x_ref[...], g_ref[...], b_ref[...])
    h0_ref[...] = h
    hb = h.astype(BF16)
    width = q_ref.shape[1]

    def proj(c):
        return _dot(hb, w_ref[:, c * width:(c + 1) * width])

    def rope(t):
        return (t * c_ref[...] + pltpu.roll(t, width - ROPE_DIM // 2, 1) * sa_ref[...]
                + pltpu.roll(t, ROPE_DIM // 2, 1) * sb_ref[...])

    q_ref[...] = _silu(proj(0)).astype(BF16)
    v_ref[...] = proj(1).astype(BF16)
    gt_ref[...] = _silu(proj(2)).astype(BF16)
    ff_ref[...] = proj(3)
    fb_ref[...] = proj(4)
    dq_ref[...] = (rope(proj(5)) * (DA_QK_DIM ** -0.5 * LOG2_E)).astype(BF16)
    dk_ref[...] = rope(proj(6)).astype(BF16)
    dv_ref[...] = proj(7).T.astype(BF16)


def _rope_lane_tables(seq_len, width):
    half = ROPE_DIM // 2
    inv = 1.0 / (ROPE_THETA ** (jnp.arange(0, ROPE_DIM, 2, dtype=F32) / ROPE_DIM))
    ang = jnp.arange(seq_len, dtype=F32)[:, None] * inv[None, :]
    cos, sin = jnp.cos(ang), jnp.sin(ang)
    pad1 = jnp.ones((seq_len, DA_QK_DIM - ROPE_DIM), F32)
    pad0 = jnp.zeros((seq_len, DA_QK_DIM - ROPE_DIM), F32)
    z = jnp.zeros((seq_len, half), F32)
    reps = width // DA_QK_DIM
    c = jnp.tile(jnp.concatenate([cos, cos, pad1], -1), (1, reps))
    sa = jnp.tile(jnp.concatenate([-sin, z, pad0], -1), (1, reps))
    sb = jnp.tile(jnp.concatenate([z, sin, pad0], -1), (1, reps))
    return c, sa, sb


def _inproj(x2, g, b, w_bf, seq_len, tm):
    n, d = x2.shape
    width = w_bf.shape[1] // 8
    c, sa, sb = _rope_lane_tables(seq_len, width)
    tpb = seq_len // tm
    row = lambda p, bi: (bi * tpb + p, 0)
    const = lambda p, bi: (0, 0)
    tab = lambda p, bi: (p, 0)
    wide = lambda dt: jax.ShapeDtypeStruct((n, width), dt)
    return pl.pallas_call(
        _inproj_kernel,
        grid=(tpb, n // seq_len),
        in_specs=[pl.BlockSpec((tm, d), row), pl.BlockSpec((1, d), const), pl.BlockSpec((1, d), const),
                  pl.BlockSpec(w_bf.shape, const),
                  pl.BlockSpec((tm, width), tab), pl.BlockSpec((tm, width), tab), pl.BlockSpec((tm, width), tab)],
        out_specs=([pl.BlockSpec((tm, d), row)] + [pl.BlockSpec((tm, width), row)] * 7
                   + [pl.BlockSpec((width, tm), lambda p, bi: (0, bi * tpb + p))]),
        out_shape=[jax.ShapeDtypeStruct((n, d), F32), wide(BF16), wide(BF16), wide(BF16), wide(F32), wide(F32),
                   wide(BF16), wide(BF16), jax.ShapeDtypeStruct((width, n), BF16)],
        compiler_params=_cparams(("parallel", "parallel")),
        name="inproj",
    )(x2, g, b, w_bf, c, sa, sb)


def _hgrn_kernel(q_ref, v_ref, gt_ref, ff_ref, fb_ref, lbl_ref, ng_ref, o_ref, st_ref, oacc_ref):
    seq_len, width = q_ref.shape
    heads = width // HG_DIM
    ch = HG_CHUNK
    blk = min(4 * ch, seq_len)
    cpb = blk // ch
    nblk = seq_len // blk
    sub_rows = 8
    tiles_per_chunk = ch // sub_rows
    lbl = lbl_ref[...]

    def lower_bound(d):
        l0, l1 = lbl[2 * d:2 * d + 1], lbl[2 * d + 1:2 * d + 2]
        m = jnp.maximum(l0, l1)
        e0, e1 = jnp.exp(l0 - m), jnp.exp(l1 - m)
        return e0 / (e0 + e1)

    lbs = (lower_bound(0), lower_bound(1))
    st_ref[...] = jnp.zeros_like(st_ref)
    oacc_ref[...] = jnp.zeros_like(oacc_ref)
    sub = lax.broadcasted_iota(jnp.int32, (blk // sub_rows, sub_rows, width), 1)
    ri = lax.broadcasted_iota(jnp.int32, (blk, blk), 0)
    ci = lax.broadcasted_iota(jnp.int32, (blk, blk), 1)
    same_chunk = (ri // ch) == (ci // ch)
    masks = (same_chunk & (ri >= ci), same_chunk & (ri <= ci))
    row_chunk = (lax.broadcasted_iota(jnp.int32, (blk, HG_DIM), 0) // ch).astype(BF16)
    in_chunk = [row_chunk == j for j in range(cpb)]
    sl = lambda hh: slice(hh * HG_DIM, (hh + 1) * HG_DIM)

    def cumprod(x, backward):
        x3 = x.reshape(blk // sub_rows, sub_rows, width)
        s = 1
        while s < sub_rows:
            if backward:
                x3 = x3 * jnp.where(sub < sub_rows - s, pltpu.roll(x3, sub_rows - s, 1), 1.0)
            else:
                x3 = x3 * jnp.where(sub >= s, pltpu.roll(x3, s, 1), 1.0)
            s *= 2
        edge = 0 if backward else sub_rows - 1
        out = [None] * (blk // sub_rows)
        for c in range(cpb):
            carry = None
            order = range(tiles_per_chunk - 1, -1, -1) if backward else range(tiles_per_chunk)
            for t in order:
                idx = c * tiles_per_chunk + t
                out[idx] = x3[idx] if carry is None else x3[idx] * carry
                carry = out[idx][edge:edge + 1]
        return jnp.concatenate(out, axis=0)

    class Item:
        pass

    def prep(d, i):
        it = Item()
        it.d = d
        bi = i if d == 0 else nblk - 1 - i
        it.rows = pl.ds(pl.multiple_of(bi * blk, blk), blk)
        uf = (ff_ref if d == 0 else fb_ref)[it.rows, :]
        f = lbs[d] + (1.0 - lbs[d]) * jax.nn.sigmoid(uf)
        g = cumprod(f, d == 1)
        edge = ch - 1 if d == 0 else 0
        it.decays = [g[j * ch + edge:j * ch + edge + 1] for j in range(cpb)]
        kin = (1.0 - f) / g
        it.k_out = (kin * jnp.concatenate([jnp.broadcast_to(dj, (ch, width)) for dj in it.decays], axis=0)
                    ).astype(BF16)
        it.q_in = (q_ref[it.rows, :].astype(F32) * g).astype(BF16)
        it.k_in = kin.astype(BF16)
        it.vv = v_ref[it.rows, :]
        return it

    def scores(it):
        it.scores = [_dot_nt(it.q_in[:, sl(hh)], it.k_in[:, sl(hh)]) for hh in range(heads)]
        it.incs = []
        for hh in range(heads):
            k_diag = jnp.concatenate([jnp.where(m, it.k_out[:, sl(hh)], jnp.zeros((), BF16)) for m in in_chunk],
                                     axis=1)
            it.incs.append(_dot_tn(it.vv[:, sl(hh)], k_diag))

    def intra(it):
        it.intra = [_dot(jnp.where(masks[it.d], it.scores[hh], 0.0).astype(BF16), it.vv[:, sl(hh)])
                    for hh in range(heads)]

    def scan(it):
        d = it.d
        st = [st_ref[d * heads + hh] for hh in range(heads)]
        it.inter = [[None] * cpb for _ in range(heads)]
        for step in range(cpb):
            j = step if d == 0 else cpb - 1 - step
            for hh in range(heads):
                it.inter[hh][j] = _dot_nt(it.q_in[j * ch:(j + 1) * ch, sl(hh)], st[hh].astype(BF16))
            for hh in range(heads):
                st[hh] = st[hh] * it.decays[j][:, sl(hh)] + it.incs[hh][:, j * HG_DIM:(j + 1) * HG_DIM]
        for hh in range(heads):
            st_ref[d * heads + hh] = st[hh]

    def write(it):
        for hh in range(heads):
            oacc_ref[it.rows, sl(hh)] += it.intra[hh] + jnp.concatenate(it.inter[hh], axis=0)

    def body(i, carry):
        fwd = prep(0, i)
        scores(fwd)
        bwd = prep(1, i)
        intra(fwd)
        scores(bwd)
        scan(fwd)
        intra(bwd)
        write(fwd)
        scan(bwd)
        write(bwd)
        return carry

    lax.fori_loop(0, nblk, body, 0)

    rb = min(256, seq_len)

    def finish(i, carry):
        rows = pl.ds(pl.multiple_of(i * rb, rb), rb)
        o = oacc_ref[rows, :]
        gate = gt_ref[rows, :].astype(F32)
        ng = ng_ref[...]
        for hh in range(heads):
            sl = slice(hh * HG_DIM, (hh + 1) * HG_DIM)
            oh = o[:, sl]
            ms = jnp.mean(oh * oh, axis=-1, keepdims=True)
            o_ref[rows, sl] = (oh * lax.rsqrt(ms + RMS_EPS) * ng[:, sl] * gate[:, sl]).astype(BF16)
        return carry

    lax.fori_loop(0, seq_len // rb, finish, 0)


def _hgrn(q, v, gt, ff, fb, lb_logits, norm_g, batch, seq_len):
    width = q.shape[1]
    hw = width
    sp = lambda a: a.reshape(batch, seq_len, width)
    blk = pl.BlockSpec((None, seq_len, hw), lambda b, h: (b, 0, h))
    return pl.pallas_call(
        _hgrn_kernel,
        grid=(batch, width // hw),
        in_specs=[blk, blk, blk, blk, blk,
                  pl.BlockSpec((4, hw), lambda b, h: (0, h)), pl.BlockSpec((1, hw), lambda b, h: (0, h))],
        out_specs=blk,
        out_shape=jax.ShapeDtypeStruct((batch, seq_len, width), BF16),
        scratch_shapes=[pltpu.VMEM((2 * (hw // HG_DIM), HG_DIM, HG_DIM), F32), pltpu.VMEM((seq_len, hw), F32)],
        compiler_params=_cparams(("parallel", "parallel")),
        name="hgrn",
    )(sp(q), sp(v), sp(gt), sp(ff), sp(fb), lb_logits.reshape(4, width), norm_g.reshape(1, width))


def _dattn_kernel(q_ref, k_ref, vt_ref, lam_ref, sg_ref, o_ref, s_buf, e_buf, acc_buf, *, kc):
    lv = lam_ref[...]
    lam = (jnp.exp(jnp.sum(lv[0:1] * lv[1:2], axis=-1, keepdims=True))
           - jnp.exp(jnp.sum(lv[2:3] * lv[3:4], axis=-1, keepdims=True)) + LAMBDA_INIT)
    q = q_ref[...]
    lane = lax.broadcasted_iota(jnp.int32, q.shape, 1)
    zero = jnp.zeros_like(q)
    qs = (jnp.where(lane < DA_QK_DIM, q, zero), jnp.where(lane >= DA_QK_DIM, q, zero))
    items = [(j, c) for j in range(k_ref.shape[0] // kc) for c in (0, 1)]
    nbuf = s_buf.shape[0]
    m, l, alpha = [None, None], [None, None], {}

    def scores(i):
        j, c = items[i]
        s_buf[i % nbuf] = _dot_nt(k_ref[j * kc:(j + 1) * kc, :], qs[c])

    def exponentials(i):
        j, c = items[i]
        s = s_buf[i % nbuf]
        mj = jnp.max(s, axis=0, keepdims=True)
        if j == 0:
            m[c], alpha[i] = mj, None
            e = jnp.exp2(s - mj)
            l[c] = jnp.sum(e, axis=0, keepdims=True)
        else:
            mn = jnp.maximum(m[c], mj)
            alpha[i] = jnp.exp2(m[c] - mn)
            e = jnp.exp2(s - mn)
            l[c] = alpha[i] * l[c] + jnp.sum(e, axis=0, keepdims=True)
            m[c] = mn
        e_buf[i % nbuf] = e.astype(BF16)

    def values(i):
        j, c = items[i]
        pv = _dot(vt_ref[:, j * kc:(j + 1) * kc], e_buf[i % nbuf])
        acc_buf[c] = pv if alpha[i] is None else alpha[i] * acc_buf[c] + pv

    scores(0)
    scores(1)
    exponentials(0)
    for i in range(len(items)):
        if i + 2 < len(items):
            scores(i + 2)
        if i + 1 < len(items):
            exponentials(i + 1)
        values(i)
    ot = acc_buf[0] * (1.0 / l[0]) - acc_buf[1] * (lam / l[1])
    ms = jnp.mean(ot * ot, axis=0, keepdims=True)
    y = ot * lax.rsqrt(ms + RMS_EPS) * (sg_ref[...] * (1.0 - LAMBDA_INIT))
    o_ref[...] = y.T.astype(BF16)


def _dattn(dq, dk, dvt, lam_vecs, subln_g, batch, seq_len, tq, kc):
    width = dq.shape[1]
    hw = DA_V_DIM
    sp = lambda a: a.reshape(batch, seq_len, width)
    kk = pl.BlockSpec((None, seq_len, hw), lambda b, h, i: (b, 0, h))
    qo = pl.BlockSpec((None, tq, hw), lambda b, h, i: (b, i, h))
    nbuf = 4
    return pl.pallas_call(
        functools.partial(_dattn_kernel, kc=kc),
        grid=(batch, width // hw, seq_len // tq),
        in_specs=[qo, kk, pl.BlockSpec((hw, seq_len), lambda b, h, i: (h, b)),
                  pl.BlockSpec(lam_vecs.shape, lambda b, h, i: (0, 0)),
                  pl.BlockSpec((hw, 1), lambda b, h, i: (0, 0))],
        out_specs=qo,
        out_shape=jax.ShapeDtypeStruct((batch, seq_len, width), BF16),
        scratch_shapes=[pltpu.VMEM((nbuf, kc, tq), F32), pltpu.VMEM((nbuf, kc, tq), BF16),
                        pltpu.VMEM((2, hw, tq), F32)],
        compiler_params=_cparams(("parallel", "parallel", "parallel")),
        name="dattn",
    )(sp(dq), sp(dk), dvt, lam_vecs, subln_g.reshape(hw, 1))


def _mixproj_kernel(hg_ref, da_ref, h0_ref, w_ref, g_ref, b_ref, o_ref):
    half = hg_ref.shape[1]
    mix = _dot(hg_ref[...], w_ref[:half, :]) + _dot(da_ref[...], w_ref[half:, :])
    o_ref[...] = _layer_norm(DEEPNORM_ALPHA * h0_ref[...] + mix, g_ref[...], b_ref[...])


def _mixproj(hg, da, h0, w_bf, g, b, tm):
    n, d = h0.shape
    half = hg.shape[1]
    row = lambda i: (i, 0)
    const = lambda i: (0, 0)
    return pl.pallas_call(
        _mixproj_kernel,
        grid=(n // tm,),
        in_specs=[pl.BlockSpec((tm, half), row), pl.BlockSpec((tm, half), row), pl.BlockSpec((tm, d), row),
                  pl.BlockSpec(w_bf.shape, const), pl.BlockSpec((1, d), const), pl.BlockSpec((1, d), const)],
        out_specs=pl.BlockSpec((tm, d), row),
        out_shape=jax.ShapeDtypeStruct((n, d), F32),
        compiler_params=_cparams(("parallel",)),
        name="mixproj",
    )(hg, da, h0, w_bf, g, b)


def _kvproj_kernel(m_ref, wk_ref, wv_ref, k_ref, v_ref):
    mb = m_ref[...].astype(BF16)
    k_ref[...] = _dot(mb, wk_ref[...]).astype(BF16)
    v_ref[...] = _dot(mb, wv_ref[...]).astype(BF16)


def _kvproj(mem, wk_bf, wv_bf):
    batch, m, d = mem.shape
    blk = pl.BlockSpec((None, m, d), lambda b: (b, 0, 0))
    const = pl.BlockSpec((d, d), lambda b: (0, 0))
    return pl.pallas_call(
        _kvproj_kernel,
        grid=(batch,),
        in_specs=[blk, const, const],
        out_specs=[blk, blk],
        out_shape=[jax.ShapeDtypeStruct((batch, m, d), BF16)] * 2,
        compiler_params=_cparams(("parallel",)),
        name="kvproj",
    )(mem, wk_bf, wv_bf)


def _xattn_kernel(h_ref, k_ref, v_ref, wq_ref, wo_ref, g_ref, b_ref, wr_ref, h2_ref, aff_ref):
    h1 = h_ref[...]
    d = h1.shape[1]
    hd = d // XA_HEADS
    q = (_dot(h1.astype(BF16), wq_ref[...]) * (hd ** -0.5)).astype(BF16)
    outs = []
    for i in range(XA_HEADS):
        sl = slice(i * hd, (i + 1) * hd)
        s = _dot_nt(q[:, sl], k_ref[:, sl])
        e = jnp.exp(s - jnp.max(s, axis=-1, keepdims=True))
        p = (e / jnp.sum(e, axis=-1, keepdims=True)).astype(BF16)
        outs.append(_dot(p, v_ref[:, sl]).astype(BF16))
    xa = _dot(jnp.concatenate(outs, axis=-1), wo_ref[...])
    h2 = _layer_norm(DEEPNORM_ALPHA * h1 + xa, g_ref[...], b_ref[...])
    h2_ref[...] = h2
    logits = _dot_nt(wr_ref[...], h2.astype(BF16))
    e = jnp.exp(logits - jnp.max(logits, axis=0, keepdims=True))
    aff_ref[...] = e / jnp.sum(e, axis=0, keepdims=True)


def _xattn(h1, kx, vx, wq_bf, wo_bf, g, b, wr_t_bf, batch, seq_len, tm):
    n, d = h1.shape
    m = kx.shape[1]
    ne = wr_t_bf.shape[0]
    tiles = seq_len // tm
    row = pl.BlockSpec((tm, d), lambda bi, i: (bi * tiles + i, 0))
    kv = pl.BlockSpec((None, m, d), lambda bi, i: (bi, 0, 0))
    const = lambda shape: pl.BlockSpec(shape, lambda bi, i: (0, 0))
    return pl.pallas_call(
        _xattn_kernel,
        grid=(batch, tiles),
        in_specs=[row, kv, kv, const((d, d)), const((d, d)), const((1, d)), const((1, d)), const((ne, d))],
        out_specs=[row, pl.BlockSpec((None, ne, tm), lambda bi, i: (bi, 0, i))],
        out_shape=[jax.ShapeDtypeStruct((n, d), F32), jax.ShapeDtypeStruct((batch, ne, seq_len), F32)],
        compiler_params=_cparams(("parallel", "parallel")),
        name="xattn",
    )(h1, kx, vx, wq_bf, wo_bf, g, b, wr_t_bf)


def _route_kernel(aff_ref, slot_ref, *, cap):
    aff = aff_ref[...]
    ne, seq_len = aff.shape
    nblk = seq_len // LANES_V7X

    def count(mask):
        return jnp.sum(jnp.where(mask, 1.0, 0.0), axis=-1, keepdims=True)

    bits = jnp.zeros((ne, 1), jnp.int32)
    for bit in range(30, -1, -1):
        cand = bits | (1 << bit)
        bits = jnp.where(count(aff >= pltpu.bitcast(cand, F32)) >= cap, cand, bits)
    thr = pltpu.bitcast(bits, F32)

    ri = lax.broadcasted_iota(jnp.int32, (LANES_V7X, LANES_V7X), 0)
    ci = lax.broadcasted_iota(jnp.int32, (LANES_V7X, LANES_V7X), 1)
    tri = jnp.where(ri <= ci, 1.0, 0.0).astype(BF16)

    def prefix_count(mask):
        m = jnp.where(mask, 1.0, 0.0)
        stack = jnp.concatenate([m[:, j * LANES_V7X:(j + 1) * LANES_V7X] for j in range(nblk)], axis=0)
        within = _dot(stack.astype(BF16), tri)
        pieces, carry = [], jnp.zeros((ne, 1), F32)
        for j in range(nblk):
            blk = within[j * ne:(j + 1) * ne]
            pieces.append(blk + carry)
            carry = carry + blk[:, LANES_V7X - 1:LANES_V7X]
        return jnp.concatenate(pieces, axis=1), m

    gt = aff > thr
    eq = aff == thr
    need = cap - count(gt)
    eq_incl, eq_f = prefix_count(eq)
    sel = gt | (eq & ((eq_incl - eq_f) < need))
    sel_incl, sel_f = prefix_count(sel)
    slot_ref[...] = jnp.where(sel, sel_incl - sel_f, -1.0).astype(jnp.int32)


def _route(aff_t, cap):
    batch, ne, seq_len = aff_t.shape
    blk = pl.BlockSpec((None, ne, seq_len), lambda b: (b, 0, 0))
    return pl.pallas_call(
        functools.partial(_route_kernel, cap=cap),
        grid=(batch,),
        in_specs=[blk],
        out_specs=blk,
        out_shape=jax.ShapeDtypeStruct((batch, ne, seq_len), jnp.int32),
        compiler_params=_cparams(("parallel",)),
        name="route",
    )(aff_t)


def _gather_kernel(slot_ref, aff_ref, h_ref, xs_ref, gate_ref, *, cap):
    hb = h_ref[...].astype(BF16)
    ne, seq_len = slot_ref.shape
    rid = lax.broadcasted_iota(jnp.int32, (cap, seq_len), 0)
    for e in range(ne):
        pick = slot_ref[e:e + 1, :] == rid
        xs_ref[e] = _dot(jnp.where(pick, 1.0, 0.0).astype(BF16), hb).astype(BF16)
        gate_ref[e] = jnp.sum(jnp.where(pick, aff_ref[e:e + 1, :], 0.0), axis=-1, keepdims=True)


def _gather(slots, aff_t, h2, cap):
    batch, ne, seq_len = slots.shape
    d = h2.shape[1]
    blk = pl.BlockSpec((None, ne, seq_len), lambda b: (b, 0, 0))
    return pl.pallas_call(
        functools.partial(_gather_kernel, cap=cap),
        grid=(batch,),
        in_specs=[blk, blk, pl.BlockSpec((seq_len, d), lambda b: (b, 0))],
        out_specs=[pl.BlockSpec((ne, None, cap, d), lambda b: (0, b, 0, 0)),
                   pl.BlockSpec((ne, None, cap, 1), lambda b: (0, b, 0, 0))],
        out_shape=[jax.ShapeDtypeStruct((ne, batch, cap, d), BF16), jax.ShapeDtypeStruct((ne, batch, cap, 1), F32)],
        compiler_params=_cparams(("parallel",)),
        name="gather",
    )(slots, aff_t, h2)


def _ffn_kernel(xs_ref, gate_ref, wg_ref, wu_ref, wd_ref, y_ref, acc_ref):
    c = pl.program_id(1)

    @pl.when(c == 0)
    def _():
        acc_ref[...] = jnp.zeros_like(acc_ref)

    xs = xs_ref[...]
    a = (_silu(_dot(xs, wg_ref[...].astype(BF16))) * _dot(xs, wu_ref[...].astype(BF16))).astype(BF16)
    acc_ref[...] += _dot(a, wd_ref[...].astype(BF16))

    @pl.when(c == pl.num_programs(1) - 1)
    def _():
        y_ref[...] = (acc_ref[...] * gate_ref[...]).astype(BF16)


def _ffn(xs, gates, w_gate, w_up, w_down, fc):
    ne, rows, d = xs.shape
    dff = w_gate.shape[2]
    return pl.pallas_call(
        _ffn_kernel,
        grid=(ne, dff // fc),
        in_specs=[pl.BlockSpec((None, rows, d), lambda e, c: (e, 0, 0)),
                  pl.BlockSpec((None, rows, 1), lambda e, c: (e, 0, 0)),
                  pl.BlockSpec((None, d, fc), lambda e, c: (e, 0, c)),
                  pl.BlockSpec((None, d, fc), lambda e, c: (e, 0, c)),
                  pl.BlockSpec((None, fc, d), lambda e, c: (e, c, 0))],
        out_specs=pl.BlockSpec((None, rows, d), lambda e, c: (e, 0, 0)),
        out_shape=jax.ShapeDtypeStruct((ne, rows, d), BF16),
        scratch_shapes=[pltpu.VMEM((rows, d), F32)],
        compiler_params=_cparams(("parallel", "arbitrary")),
        name="ffn",
    )(xs, gates, w_gate, w_up, w_down)


def _combine_kernel(slot_ref, y_ref, h_ref, g_ref, b_ref, o_ref, *, cap):
    ne, tl = slot_ref.shape
    slot_t = slot_ref[...].astype(F32).T
    lane = lax.broadcasted_iota(jnp.int32, (tl, cap), 1).astype(F32)
    place = jnp.concatenate(
        [jnp.where(slot_t[:, e:e + 1] == lane, 1.0, 0.0).astype(BF16) for e in range(ne)], axis=1)
    moe = _dot(place, y_ref[...].reshape(ne * cap, y_ref.shape[-1]))
    o_ref[...] = _layer_norm(DEEPNORM_ALPHA * h_ref[...] + moe, g_ref[...], b_ref[...])


def _combine(slots, y, h2, g, b, cap, tl):
    batch, ne, seq_len = slots.shape
    d = h2.shape[1]
    tiles = seq_len // tl
    row = pl.BlockSpec((tl, d), lambda bi, i: (bi * tiles + i, 0))
    const = pl.BlockSpec((1, d), lambda bi, i: (0, 0))
    return pl.pallas_call(
        functools.partial(_combine_kernel, cap=cap),
        grid=(batch, tiles),
        in_specs=[pl.BlockSpec((None, ne, tl), lambda bi, i: (bi, 0, i)),
                  pl.BlockSpec((ne, None, cap, d), lambda bi, i: (0, bi, 0, 0)),
                  row, const, const],
        out_specs=row,
        out_shape=jax.ShapeDtypeStruct((batch * seq_len, d), F32),
        compiler_params=_cparams(("parallel", "parallel")),
        name="combine",
    )(slots, y, h2, g, b)


def kernel(x, mem, emb_ln_g, emb_ln_b, w_in, hg_lb_logits, hg_norm_g, da_lambda_q1, da_lambda_k1, da_lambda_q2,
           da_lambda_k2, da_subln_g, w_mix_out, ln1_g, ln1_b, xa_wq, xa_wk, xa_wv, xa_wo, ln2_g, ln2_b, w_router,
           w_gate, w_up, w_down, ln3_g, ln3_b):
    batch, seq_len, d = x.shape
    assert w_in.shape[0] == DEPTH and seq_len % HG_CHUNK == 0 and seq_len % LANES_V7X == 0
    n = batch * seq_len
    cap = EC_FACTOR * seq_len // N_EXPERTS
    vec = lambda a: a.reshape(1, -1)
    bf = lambda a: a.astype(BF16)

    t = _tiles(seq_len)

    h0, q, v, gt, ff, fb, dq, dk, dv = _inproj(
        x.reshape(n, d), vec(emb_ln_g), vec(emb_ln_b), bf(w_in[0]), seq_len, t.inproj_rows)
    hg = _hgrn(q, v, gt, ff, fb, hg_lb_logits, hg_norm_g[0], batch, seq_len)
    lam_vecs = jnp.stack([da_lambda_q1[0], da_lambda_k1[0], da_lambda_q2[0], da_lambda_k2[0]])
    da = _dattn(dq, dk, dv, lam_vecs, da_subln_g[0], batch, seq_len, t.dattn_queries, t.dattn_keys)
    h1 = _mixproj(hg.reshape(n, -1), da.reshape(n, -1), h0, bf(w_mix_out[0]), ln1_g, ln1_b, t.mix_rows)

    kx, vx = _kvproj(mem, bf(xa_wk[0]), bf(xa_wv[0]))
    h2, aff_t = _xattn(h1, kx, vx, bf(xa_wq[0]), bf(xa_wo[0]), ln2_g, ln2_b, bf(w_router[0].T),
                       batch, seq_len, t.xattn_rows)

    slots = _route(aff_t, cap)
    xs, gates = _gather(slots, aff_t, h2, cap)
    y = _ffn(xs.reshape(N_EXPERTS, batch * cap, d), gates.reshape(N_EXPERTS, batch * cap, 1),
             w_gate[0], w_up[0], w_down[0], t.ffn_cols)
    out = _combine(slots, y.reshape(N_EXPERTS, batch, cap, d), h2, ln3_g, ln3_b, cap, t.combine_rows)
    return out.reshape(batch, seq_len, d)
```

```python
import functools
import math
from typing import NamedTuple

import jax
import jax.numpy as jnp
from jax import lax
from jax.experimental import pallas as pl
from jax.experimental.pallas import tpu as pltpu

F32 = jnp.float32
BF16 = jnp.bfloat16

HG_HEADS = 4
HG_DIM = 128
HG_CHUNK = 64
DA_HEADS = 4
DA_QK_DIM = 64
DA_V_DIM = 128
ROPE_DIM = DA_QK_DIM // 4
ROPE_THETA = 500000.0
XA_HEADS = 4
N_EXPERTS = 16
EC_FACTOR = 2
LN_EPS = 1e-5
RMS_EPS = 1e-6
DEPTH = 1
DEEPNORM_ALPHA = (2.0 * DEPTH) ** 0.25
LAMBDA_INIT = 0.8 - 0.6 * math.exp(-0.3 * 0)
LOG2_E = math.log2(math.e)

LANES_V7X = 128
VMEM_LIMIT_V7X = 56 * 1024 * 1024


class _Tiles(NamedTuple):
    inproj_rows: int
    dattn_queries: int
    dattn_keys: int
    mix_rows: int
    xattn_rows: int
    ffn_cols: int
    combine_rows: int


def _tiles(seq_len):
    fit = lambda rows: min(rows, seq_len)
    return _Tiles(inproj_rows=fit(512), dattn_queries=fit(2048), dattn_keys=fit(512), mix_rows=fit(1024),
                  xattn_rows=fit(1024), ffn_cols=512, combine_rows=fit(1024))


def _cparams(sem):
    return pltpu.CompilerParams(dimension_semantics=sem, vmem_limit_bytes=VMEM_LIMIT_V7X)


def _layer_norm(x, g, b):
    mu = jnp.mean(x, axis=-1, keepdims=True)
    xc = x - mu
    var = jnp.mean(xc * xc, axis=-1, keepdims=True)
    return xc * lax.rsqrt(var + LN_EPS) * g + b


def _silu(x):
    return x * jax.nn.sigmoid(x)


def _dot(a, b):
    return jnp.dot(a, b, preferred_element_type=F32)


def _dot_nt(a, b):
    return lax.dot_general(a, b, (((1,), (1,)), ((), ())), preferred_element_type=F32)


def _dot_tn(a, b):
    return lax.dot_general(a, b, (((0,), (0,)), ((), ())), preferred_element_type=F32)


def _inproj_kernel(x_ref, g_ref, b_ref, w_ref, c_ref, sa_ref, sb_ref,
                   h0_ref, q_ref, v_ref, gt_ref, ff_ref, fb_ref, dq_ref, dk_ref, dv_ref):
    h = _layer_norm(x_ref[...], g_ref[...], b_ref[...])
    h0_ref[...] = h
    hb = h.astype(BF16)
    width = q_ref.shape[1]

    def proj(c):
        return _dot(hb, w_ref[:, c * width:(c + 1) * width])

    widen = lambda ref: jnp.concatenate([ref[...]] * (width // ref.shape[1]), axis=1)
    rope_c, rope_sa, rope_sb = widen(c_ref), widen(sa_ref), widen(sb_ref)

    def rope(t):
        return (t * rope_c + pltpu.roll(t, width - ROPE_DIM // 2, 1) * rope_sa
                + pltpu.roll(t, ROPE_DIM // 2, 1) * rope_sb)

    q_ref[...] = _silu(proj(0)).astype(BF16)
    v_ref[...] = proj(1).astype(BF16)
    gt_ref[...] = _silu(proj(2)).astype(BF16)
    ff_ref[...] = proj(3)
    fb_ref[...] = proj(4)
    dq_ref[...] = (rope(proj(5)) * (DA_QK_DIM ** -0.5 * LOG2_E)).astype(BF16)
    dk_ref[...] = rope(proj(6)).astype(BF16)
    dv_ref[...] = proj(7).T.astype(BF16)


def _rope_lane_tables(seq_len, width):
    half = ROPE_DIM // 2
    inv = 1.0 / (ROPE_THETA ** (jnp.arange(0, ROPE_DIM, 2, dtype=F32) / ROPE_DIM))
    ang = jnp.arange(seq_len, dtype=F32)[:, None] * inv[None, :]
    cos, sin = jnp.cos(ang), jnp.sin(ang)
    pad1 = jnp.ones((seq_len, DA_QK_DIM - ROPE_DIM), F32)
    pad0 = jnp.zeros((seq_len, DA_QK_DIM - ROPE_DIM), F32)
    z = jnp.zeros((seq_len, half), F32)
    reps = width // DA_QK_DIM
    c = jnp.tile(jnp.concatenate([cos, cos, pad1], -1), (1, reps))
    sa = jnp.tile(jnp.concatenate([-sin, z, pad0], -1), (1, reps))
    sb = jnp.tile(jnp.concatenate([z, sin, pad0], -1), (1, reps))
    return c, sa, sb


def _inproj(x2, g, b, w_bf, seq_len, tm):
    n, d = x2.shape
    width = w_bf.shape[1] // 8
    c, sa, sb = _rope_lane_tables(seq_len, LANES_V7X)
    tpb = seq_len // tm
    row = lambda p, bi: (bi * tpb + p, 0)
    const = lambda p, bi: (0, 0)
    tab = lambda p, bi: (p, 0)
    wide = lambda dt: jax.ShapeDtypeStruct((n, width), dt)
    return pl.pallas_call(
        _inproj_kernel,
        grid=(tpb, n // seq_len),
        in_specs=[pl.BlockSpec((tm, d), row), pl.BlockSpec((1, d), const), pl.BlockSpec((1, d), const),
                  pl.BlockSpec(w_bf.shape, const),
                  pl.BlockSpec((tm, LANES_V7X), tab), pl.BlockSpec((tm, LANES_V7X), tab),
                  pl.BlockSpec((tm, LANES_V7X), tab)],
        out_specs=([pl.BlockSpec((tm, d), row)] + [pl.BlockSpec((tm, width), row)] * 7
                   + [pl.BlockSpec((width, tm), lambda p, bi: (0, bi * tpb + p))]),
        out_shape=[jax.ShapeDtypeStruct((n, d), F32), wide(BF16), wide(BF16), wide(BF16), wide(F32), wide(F32),
                   wide(BF16), wide(BF16), jax.ShapeDtypeStruct((width, n), BF16)],
        compiler_params=_cparams(("parallel", "parallel")),
        name="inproj",
    )(x2, g, b, w_bf, c, sa, sb)


def _hgrn_kernel(q_ref, v_ref, gt_ref, ff_ref, fb_ref, lbl_ref, ng_ref, o_ref, st_ref, oacc_ref):
    seq_len, width = q_ref.shape
    heads = width // HG_DIM
    ch = HG_CHUNK
    blk = min(4 * ch, seq_len)
    cpb = blk // ch
    nblk = seq_len // blk
    sub_rows = 8
    tiles_per_chunk = ch // sub_rows
    lbl = lbl_ref[...]

    def lower_bound(d):
        l0, l1 = lbl[2 * d:2 * d + 1], lbl[2 * d + 1:2 * d + 2]
        m = jnp.maximum(l0, l1)
        e0, e1 = jnp.exp(l0 - m), jnp.exp(l1 - m)
        return e0 / (e0 + e1)

    lbs = (lower_bound(0), lower_bound(1))
    st_ref[...] = jnp.zeros_like(st_ref)
    oacc_ref[...] = jnp.zeros_like(oacc_ref)
    sub = lax.broadcasted_iota(jnp.int32, (blk // sub_rows, sub_rows, width), 1)
    ri = lax.broadcasted_iota(jnp.int32, (blk, blk), 0)
    ci = lax.broadcasted_iota(jnp.int32, (blk, blk), 1)
    same_chunk = (ri // ch) == (ci // ch)
    masks = (same_chunk & (ri >= ci), same_chunk & (ri <= ci))
    row_chunk = (lax.broadcasted_iota(jnp.int32, (blk, HG_DIM), 0) // ch).astype(BF16)
    in_chunk = [row_chunk == j for j in range(cpb)]
    sl = lambda hh: slice(hh * HG_DIM, (hh + 1) * HG_DIM)

    def cumprod(x, backward):
        x3 = x.reshape(blk // sub_rows, sub_rows, width)
        s = 1
        while s < sub_rows:
            if backward:
                x3 = x3 * jnp.where(sub < sub_rows - s, pltpu.roll(x3, sub_rows - s, 1), 1.0)
            else:
                x3 = x3 * jnp.where(sub >= s, pltpu.roll(x3, s, 1), 1.0)
            s *= 2
        edge = 0 if backward else sub_rows - 1
        out = [None] * (blk // sub_rows)
        for c in range(cpb):
            carry = None
            order = range(tiles_per_chunk - 1, -1, -1) if backward else range(tiles_per_chunk)
            for t in order:
                idx = c * tiles_per_chunk + t
                out[idx] = x3[idx] if carry is None else x3[idx] * carry
                carry = out[idx][edge:edge + 1]
        return jnp.concatenate(out, axis=0)

    class Item:
        pass

    def prep(d, i):
        it = Item()
        it.d = d
        bi = i if d == 0 else nblk - 1 - i
        it.rows = pl.ds(pl.multiple_of(bi * blk, blk), blk)
        uf = (ff_ref if d == 0 else fb_ref)[it.rows, :]
        f = lbs[d] + (1.0 - lbs[d]) * jax.nn.sigmoid(uf)
        g = cumprod(f, d == 1)
        edge = ch - 1 if d == 0 else 0
        it.decays = [g[j * ch + edge:j * ch + edge + 1] for j in range(cpb)]
        kin = (1.0 - f) / g
        it.k_out = (kin * jnp.concatenate([jnp.broadcast_to(dj, (ch, width)) for dj in it.decays], axis=0)
                    ).astype(BF16)
        it.q_in = (q_ref[it.rows, :].astype(F32) * g).astype(BF16)
        it.k_in = kin.astype(BF16)
        it.vv = v_ref[it.rows, :]
        return it

    def scores(it):
        it.scores = [_dot_nt(it.q_in[:, sl(hh)], it.k_in[:, sl(hh)]) for hh in range(heads)]
        it.incs = []
        for hh in range(heads):
            k_diag = jnp.concatenate([jnp.where(m, it.k_out[:, sl(hh)], jnp.zeros((), BF16)) for m in in_chunk],
                                     axis=1)
            it.incs.append(_dot_tn(it.vv[:, sl(hh)], k_diag))

    def intra(it):
        it.intra = [_dot(jnp.where(masks[it.d], it.scores[hh], 0.0).astype(BF16), it.vv[:, sl(hh)])
                    for hh in range(heads)]

    def scan(it):
        d = it.d
        st = [st_ref[d * heads + hh] for hh in range(heads)]
        it.inter = [[None] * cpb for _ in range(heads)]
        for step in range(cpb):
            j = step if d == 0 else cpb - 1 - step
            for hh in range(heads):
                it.inter[hh][j] = _dot_nt(it.q_in[j * ch:(j + 1) * ch, sl(hh)], st[hh].astype(BF16))
            for hh in range(heads):
                st[hh] = st[hh] * it.decays[j][:, sl(hh)] + it.incs[hh][:, j * HG_DIM:(j + 1) * HG_DIM]
        for hh in range(heads):
            st_ref[d * heads + hh] = st[hh]

    def write(it):
        for hh in range(heads):
            oacc_ref[it.rows, sl(hh)] += it.intra[hh] + jnp.concatenate(it.inter[hh], axis=0)

    def body(i, carry):
        fwd = prep(0, i)
        scores(fwd)
        bwd = prep(1, i)
        intra(fwd)
        scores(bwd)
        scan(fwd)
        intra(bwd)
        write(fwd)
        scan(bwd)
        write(bwd)
        return carry

    lax.fori_loop(0, nblk, body, 0)

    rb = min(256, seq_len)

    def finish(i, carry):
        rows = pl.ds(pl.multiple_of(i * rb, rb), rb)
        o = oacc_ref[rows, :]
        gate = gt_ref[rows, :].astype(F32)
        ng = ng_ref[...]
        for hh in range(heads):
            sl = slice(hh * HG_DIM, (hh + 1) * HG_DIM)
            oh = o[:, sl]
            ms = jnp.mean(oh * oh, axis=-1, keepdims=True)
            o_ref[rows, sl] = (oh * lax.rsqrt(ms + RMS_EPS) * ng[:, sl] * gate[:, sl]).astype(BF16)
        return carry

    lax.fori_loop(0, seq_len // rb, finish, 0)


def _hgrn(q, v, gt, ff, fb, lb_logits, norm_g, batch, seq_len):
    width = q.shape[1]
    hw = width
    sp = lambda a: a.reshape(batch, seq_len, width)
    blk = pl.BlockSpec((None, seq_len, hw), lambda b, h: (b, 0, h))
    return pl.pallas_call(
        _hgrn_kernel,
        grid=(batch, width // hw),
        in_specs=[blk, blk, blk, blk, blk,
                  pl.BlockSpec((4, hw), lambda b, h: (0, h)), pl.BlockSpec((1, hw), lambda b, h: (0, h))],
        out_specs=blk,
        out_shape=jax.ShapeDtypeStruct((batch, seq_len, width), BF16),
        scratch_shapes=[pltpu.VMEM((2 * (hw // HG_DIM), HG_DIM, HG_DIM), F32), pltpu.VMEM((seq_len, hw), F32)],
        compiler_params=_cparams(("parallel", "parallel")),
        name="hgrn",
    )(sp(q), sp(v), sp(gt), sp(ff), sp(fb), lb_logits.reshape(4, width), norm_g.reshape(1, width))


def _dattn_kernel(q_ref, k_ref, vt_ref, lam_ref, sg_ref, o_ref, s_buf, e_buf, acc_buf, *, kc):
    lv = lam_ref[...]
    lam = (jnp.exp(jnp.sum(lv[0:1] * lv[1:2], axis=-1, keepdims=True))
           - jnp.exp(jnp.sum(lv[2:3] * lv[3:4], axis=-1, keepdims=True)) + LAMBDA_INIT)
    q = q_ref[...]
    lane = lax.broadcasted_iota(jnp.int32, q.shape, 1)
    zero = jnp.zeros_like(q)
    qs = (jnp.where(lane < DA_QK_DIM, q, zero), jnp.where(lane >= DA_QK_DIM, q, zero))
    items = [(j, c) for j in range(k_ref.shape[0] // kc) for c in (0, 1)]
    nbuf = s_buf.shape[0]
    m, l, alpha = [None, None], [None, None], {}

    def scores(i):
        j, c = items[i]
        s_buf[i % nbuf] = _dot_nt(k_ref[j * kc:(j + 1) * kc, :], qs[c])

    def exponentials(i):
        j, c = items[i]
        s = s_buf[i % nbuf]
        mj = jnp.max(s, axis=0, keepdims=True)
        if j == 0:
            m[c], alpha[i] = mj, None
            e = jnp.exp2(s - mj)
            l[c] = jnp.sum(e, axis=0, keepdims=True)
        else:
            mn = jnp.maximum(m[c], mj)
            alpha[i] = jnp.exp2(m[c] - mn)
            e = jnp.exp2(s - mn)
            l[c] = alpha[i] * l[c] + jnp.sum(e, axis=0, keepdims=True)
            m[c] = mn
        e_buf[i % nbuf] = e.astype(BF16)

    def values(i):
        j, c = items[i]
        pv = _dot(vt_ref[:, j * kc:(j + 1) * kc], e_buf[i % nbuf])
        acc_buf[c] = pv if alpha[i] is None else alpha[i] * acc_buf[c] + pv

    scores(0)
    scores(1)
    exponentials(0)
    for i in range(len(items)):
        if i + 2 < len(items):
            scores(i + 2)
        if i + 1 < len(items):
            exponentials(i + 1)
        values(i)
    ot = acc_buf[0] * (1.0 / l[0]) - acc_buf[1] * (lam / l[1])
    ms = jnp.mean(ot * ot, axis=0, keepdims=True)
    y = ot * lax.rsqrt(ms + RMS_EPS) * (sg_ref[...] * (1.0 - LAMBDA_INIT))
    o_ref[...] = y.T.astype(BF16)


def _dattn(dq, dk, dvt, lam_vecs, subln_g, batch, seq_len, tq, kc):
    width = dq.shape[1]
    hw = DA_V_DIM
    sp = lambda a: a.reshape(batch, seq_len, width)
    kk = pl.BlockSpec((None, seq_len, hw), lambda b, h, i: (b, 0, h))
    qo = pl.BlockSpec((None, tq, hw), lambda b, h, i: (b, i, h))
    nbuf = 4
    return pl.pallas_call(
        functools.partial(_dattn_kernel, kc=kc),
        grid=(batch, width // hw, seq_len // tq),
        in_specs=[qo, kk, pl.BlockSpec((hw, seq_len), lambda b, h, i: (h, b)),
                  pl.BlockSpec(lam_vecs.shape, lambda b, h, i: (0, 0)),
                  pl.BlockSpec((hw, 1), lambda b, h, i: (0, 0))],
        out_specs=qo,
        out_shape=jax.ShapeDtypeStruct((batch, seq_len, width), BF16),
        scratch_shapes=[pltpu.VMEM((nbuf, kc, tq), F32), pltpu.VMEM((nbuf, kc, tq), BF16),
                        pltpu.VMEM((2, hw, tq), F32)],
        compiler_params=_cparams(("parallel", "parallel", "parallel")),
        name="dattn",
    )(sp(dq), sp(dk), dvt, lam_vecs, subln_g.reshape(hw, 1))


def _mixproj_kernel(hg_ref, da_ref, h0_ref, w_ref, g_ref, b_ref, o_ref, mix_buf, *, parts):
    half = hg_ref.shape[1]
    pr = hg_ref.shape[0] // parts
    rows = lambda r: slice(r * pr, (r + 1) * pr)

    def project(r):
        mix_buf[r % 2] = _dot(hg_ref[rows(r), :], w_ref[:half, :]) + _dot(da_ref[rows(r), :], w_ref[half:, :])

    def normalize(r):
        o_ref[rows(r), :] = _layer_norm(DEEPNORM_ALPHA * h0_ref[rows(r), :] + mix_buf[r % 2], g_ref[...], b_ref[...])

    project(0)
    for r in range(parts):
        if r + 1 < parts:
            project(r + 1)
        normalize(r)


def _mixproj(hg, da, h0, w_bf, g, b, tm):
    n, d = h0.shape
    half = hg.shape[1]
    parts = 4 if tm % 64 == 0 else 1
    row = lambda i: (i, 0)
    const = lambda i: (0, 0)
    return pl.pallas_call(
        functools.partial(_mixproj_kernel, parts=parts),
        grid=(n // tm,),
        in_specs=[pl.BlockSpec((tm, half), row), pl.BlockSpec((tm, half), row), pl.BlockSpec((tm, d), row),
                  pl.BlockSpec(w_bf.shape, const), pl.BlockSpec((1, d), const), pl.BlockSpec((1, d), const)],
        out_specs=pl.BlockSpec((tm, d), row),
        out_shape=jax.ShapeDtypeStruct((n, d), F32),
        scratch_shapes=[pltpu.VMEM((2, tm // parts, d), F32)],
        compiler_params=_cparams(("parallel",)),
        name="mixproj",
    )(hg, da, h0, w_bf, g, b)


def _kvproj_kernel(m_ref, wk_ref, wv_ref, k_ref, v_ref):
    mb = m_ref[...].astype(BF16)
    k_ref[...] = _dot(mb, wk_ref[...]).astype(BF16)
    v_ref[...] = _dot(mb, wv_ref[...]).astype(BF16)


def _kvproj(mem, wk_bf, wv_bf):
    batch, m, d = mem.shape
    blk = pl.BlockSpec((None, m, d), lambda b: (b, 0, 0))
    const = pl.BlockSpec((d, d), lambda b: (0, 0))
    return pl.pallas_call(
        _kvproj_kernel,
        grid=(batch,),
        in_specs=[blk, const, const],
        out_specs=[blk, blk],
        out_shape=[jax.ShapeDtypeStruct((batch, m, d), BF16)] * 2,
        compiler_params=_cparams(("parallel",)),
        name="kvproj",
    )(mem, wk_bf, wv_bf)


def _xattn_kernel(h_ref, k_ref, v_ref, wq_ref, wo_ref, g_ref, b_ref, wr_ref, h2_ref, aff_ref):
    h1 = h_ref[...]
    d = h1.shape[1]
    hd = d // XA_HEADS
    q = (_dot(h1.astype(BF16), wq_ref[...]) * (hd ** -0.5)).astype(BF16)
    outs = []
    for i in range(XA_HEADS):
        sl = slice(i * hd, (i + 1) * hd)
        s = _dot_nt(q[:, sl], k_ref[:, sl])
        e = jnp.exp(s - jnp.max(s, axis=-1, keepdims=True))
        p = (e / jnp.sum(e, axis=-1, keepdims=True)).astype(BF16)
        outs.append(_dot(p, v_ref[:, sl]).astype(BF16))
    xa = _dot(jnp.concatenate(outs, axis=-1), wo_ref[...])
    h2 = _layer_norm(DEEPNORM_ALPHA * h1 + xa, g_ref[...], b_ref[...])
    h2_ref[...] = h2
    logits = _dot_nt(wr_ref[...], h2.astype(BF16))
    e = jnp.exp(logits - jnp.max(logits, axis=0, keepdims=True))
    aff_ref[...] = e / jnp.sum(e, axis=0, keepdims=True)


def _xattn(h1, kx, vx, wq_bf, wo_bf, g, b, wr_t_bf, batch, seq_len, tm):
    n, d = h1.shape
    m = kx.shape[1]
    ne = wr_t_bf.shape[0]
    tiles = seq_len // tm
    row = pl.BlockSpec((tm, d), lambda bi, i: (bi * tiles + i, 0))
    kv = pl.BlockSpec((None, m, d), lambda bi, i: (bi, 0, 0))
    const = lambda shape: pl.BlockSpec(shape, lambda bi, i: (0, 0))
    return pl.pallas_call(
        _xattn_kernel,
        grid=(batch, tiles),
        in_specs=[row, kv, kv, const((d, d)), const((d, d)), const((1, d)), const((1, d)), const((ne, d))],
        out_specs=[row, pl.BlockSpec((None, ne, tm), lambda bi, i: (bi, 0, i))],
        out_shape=[jax.ShapeDtypeStruct((n, d), F32), jax.ShapeDtypeStruct((batch, ne, seq_len), F32)],
        compiler_params=_cparams(("parallel", "parallel")),
        name="xattn",
    )(h1, kx, vx, wq_bf, wo_bf, g, b, wr_t_bf)


def _route_kernel(aff_ref, slot_ref, *, cap):
    aff = aff_ref[...]
    ne, seq_len = aff.shape
    nblk = seq_len // LANES_V7X

    def count(mask):
        return jnp.sum(jnp.where(mask, 1.0, 0.0), axis=-1, keepdims=True)

    bits = jnp.zeros((ne, 1), jnp.int32)
    for bit in range(30, -1, -1):
        cand = bits | (1 << bit)
        bits = jnp.where(count(aff >= pltpu.bitcast(cand, F32)) >= cap, cand, bits)
    thr = pltpu.bitcast(bits, F32)

    ri = lax.broadcasted_iota(jnp.int32, (LANES_V7X, LANES_V7X), 0)
    ci = lax.broadcasted_iota(jnp.int32, (LANES_V7X, LANES_V7X), 1)
    tri = jnp.where(ri <= ci, 1.0, 0.0).astype(BF16)

    def prefix_count(mask):
        m = jnp.where(mask, 1.0, 0.0)
        stack = jnp.concatenate([m[:, j * LANES_V7X:(j + 1) * LANES_V7X] for j in range(nblk)], axis=0)
        within = _dot(stack.astype(BF16), tri)
        pieces, carry = [], jnp.zeros((ne, 1), F32)
        for j in range(nblk):
            blk = within[j * ne:(j + 1) * ne]
            pieces.append(blk + carry)
            carry = carry + blk[:, LANES_V7X - 1:LANES_V7X]
        return jnp.concatenate(pieces, axis=1), m

    gt = aff > thr
    eq = aff == thr
    need = cap - count(gt)
    eq_incl, eq_f = prefix_count(eq)
    sel = gt | (eq & ((eq_incl - eq_f) < need))
    sel_incl, sel_f = prefix_count(sel)
    slot_ref[...] = jnp.where(sel, sel_incl - sel_f, -1.0).astype(jnp.int32)


def _route(aff_t, cap):
    batch, ne, seq_len = aff_t.shape
    blk = pl.BlockSpec((None, ne, seq_len), lambda b: (b, 0, 0))
    return pl.pallas_call(
        functools.partial(_route_kernel, cap=cap),
        grid=(batch,),
        in_specs=[blk],
        out_specs=blk,
        out_shape=jax.ShapeDtypeStruct((batch, ne, seq_len), jnp.int32),
        compiler_params=_cparams(("parallel",)),
        name="route",
    )(aff_t)


def _gather_kernel(slot_ref, aff_ref, h_ref, xs_ref, gate_ref, *, cap):
    hb = h_ref[...].astype(BF16)
    ne, seq_len = slot_ref.shape
    rid = lax.broadcasted_iota(jnp.int32, (cap, seq_len), 0)
    for e in range(ne):
        pick = slot_ref[e:e + 1, :] == rid
        xs_ref[e] = _dot(jnp.where(pick, 1.0, 0.0).astype(BF16), hb).astype(BF16)
        gate_ref[e] = jnp.sum(jnp.where(pick, aff_ref[e:e + 1, :], 0.0), axis=-1, keepdims=True)


def _gather(slots, aff_t, h2, cap):
    batch, ne, seq_len = slots.shape
    d = h2.shape[1]
    blk = pl.BlockSpec((None, ne, seq_len), lambda b: (b, 0, 0))
    return pl.pallas_call(
        functools.partial(_gather_kernel, cap=cap),
        grid=(batch,),
        in_specs=[blk, blk, pl.BlockSpec((seq_len, d), lambda b: (b, 0))],
        out_specs=[pl.BlockSpec((ne, None, cap, d), lambda b: (0, b, 0, 0)),
                   pl.BlockSpec((ne, None, cap, 1), lambda b: (0, b, 0, 0))],
        out_shape=[jax.ShapeDtypeStruct((ne, batch, cap, d), BF16), jax.ShapeDtypeStruct((ne, batch, cap, 1), F32)],
        compiler_params=_cparams(("parallel",)),
        name="gather",
    )(slots, aff_t, h2)


def _ffn_kernel(xs_ref, gate_ref, wg_ref, wu_ref, wd_ref, y_ref, acc_ref, g_buf, u_buf, a_buf):
    c = pl.program_id(1)

    @pl.when(c == 0)
    def _():
        acc_ref[...] = jnp.zeros_like(acc_ref)

    wg, wu, wd = wg_ref[...].astype(BF16), wu_ref[...].astype(BF16), wd_ref[...].astype(BF16)
    parts = g_buf.shape[0]
    pr = xs_ref.shape[0] // parts
    rows = lambda r: slice(r * pr, (r + 1) * pr)

    def gate_up(r):
        xs = xs_ref[rows(r), :]
        g_buf[r] = _dot(xs, wg)
        u_buf[r] = _dot(xs, wu)

    def down(r):
        a_buf[r] = (_silu(g_buf[r]) * u_buf[r]).astype(BF16)
        acc_ref[rows(r), :] += _dot(a_buf[r], wd)

    gate_up(0)
    for r in range(parts):
        if r + 1 < parts:
            gate_up(r + 1)
        down(r)

    @pl.when(c == pl.num_programs(1) - 1)
    def _():
        y_ref[...] = (acc_ref[...] * gate_ref[...]).astype(BF16)


def _ffn(xs, gates, w_gate, w_up, w_down, fc):
    ne, rows, d = xs.shape
    dff = w_gate.shape[2]
    parts = 2 if rows % 32 == 0 else 1
    return pl.pallas_call(
        _ffn_kernel,
        grid=(ne, dff // fc),
        in_specs=[pl.BlockSpec((None, rows, d), lambda e, c: (e, 0, 0)),
                  pl.BlockSpec((None, rows, 1), lambda e, c: (e, 0, 0)),
                  pl.BlockSpec((None, d, fc), lambda e, c: (e, 0, c)),
                  pl.BlockSpec((None, d, fc), lambda e, c: (e, 0, c)),
                  pl.BlockSpec((None, fc, d), lambda e, c: (e, c, 0))],
        out_specs=pl.BlockSpec((None, rows, d), lambda e, c: (e, 0, 0)),
        out_shape=jax.ShapeDtypeStruct((ne, rows, d), BF16),
        scratch_shapes=[pltpu.VMEM((rows, d), F32), pltpu.VMEM((parts, rows // parts, fc), F32),
                        pltpu.VMEM((parts, rows // parts, fc), F32), pltpu.VMEM((parts, rows // parts, fc), BF16)],
        compiler_params=_cparams(("parallel", "arbitrary")),
        name="ffn",
    )(xs, gates, w_gate, w_up, w_down)


def _combine_kernel(slot_ref, y_ref, h_ref, g_ref, b_ref, o_ref, *, cap):
    ne, tl = slot_ref.shape
    slot_t = slot_ref[...].astype(F32).T
    lane = lax.broadcasted_iota(jnp.int32, (tl, cap), 1).astype(F32)
    place = jnp.concatenate(
        [jnp.where(slot_t[:, e:e + 1] == lane, 1.0, 0.0).astype(BF16) for e in range(ne)], axis=1)
    moe = _dot(place, y_ref[...].reshape(ne * cap, y_ref.shape[-1]))
    o_ref[...] = _layer_norm(DEEPNORM_ALPHA * h_ref[...] + moe, g_ref[...], b_ref[...])


def _combine(slots, y, h2, g, b, cap, tl):
    batch, ne, seq_len = slots.shape
    d = h2.shape[1]
    tiles = seq_len // tl
    row = pl.BlockSpec((tl, d), lambda bi, i: (bi * tiles + i, 0))
    const = pl.BlockSpec((1, d), lambda bi, i: (0, 0))
    return pl.pallas_call(
        functools.partial(_combine_kernel, cap=cap),
        grid=(batch, tiles),
        in_specs=[pl.BlockSpec((None, ne, tl), lambda bi, i: (bi, 0, i)),
                  pl.BlockSpec((ne, None, cap, d), lambda bi, i: (0, bi, 0, 0)),
                  row, const, const],
        out_specs=row,
        out_shape=jax.ShapeDtypeStruct((batch * seq_len, d), F32),
        compiler_params=_cparams(("parallel", "parallel")),
        name="combine",
    )(slots, y, h2, g, b)


def kernel(x, mem, emb_ln_g, emb_ln_b, w_in, hg_lb_logits, hg_norm_g, da_lambda_q1, da_lambda_k1, da_lambda_q2,
           da_lambda_k2, da_subln_g, w_mix_out, ln1_g, ln1_b, xa_wq, xa_wk, xa_wv, xa_wo, ln2_g, ln2_b, w_router,
           w_gate, w_up, w_down, ln3_g, ln3_b):
    batch, seq_len, d = x.shape
    assert w_in.shape[0] == DEPTH and seq_len % HG_CHUNK == 0 and seq_len % LANES_V7X == 0
    n = batch * seq_len
    cap = EC_FACTOR * seq_len // N_EXPERTS
    vec = lambda a: a.reshape(1, -1)
    bf = lambda a: a.astype(BF16)

    t = _tiles(seq_len)

    h0, q, v, gt, ff, fb, dq, dk, dv = _inproj(
        x.reshape(n, d), vec(emb_ln_g), vec(emb_ln_b), bf(w_in[0]), seq_len, t.inproj_rows)
    hg = _hgrn(q, v, gt, ff, fb, hg_lb_logits, hg_norm_g[0], batch, seq_len)
    lam_vecs = jnp.stack([da_lambda_q1[0], da_lambda_k1[0], da_lambda_q2[0], da_lambda_k2[0]])
    da = _dattn(dq, dk, dv, lam_vecs, da_subln_g[0], batch, seq_len, t.dattn_queries, t.dattn_keys)
    h1 = _mixproj(hg.reshape(n, -1), da.reshape(n, -1), h0, bf(w_mix_out[0]), ln1_g, ln1_b, t.mix_rows)

    kx, vx = _kvproj(mem, bf(xa_wk[0]), bf(xa_wv[0]))
    h2, aff_t = _xattn(h1, kx, vx, bf(xa_wq[0]), bf(xa_wo[0]), ln2_g, ln2_b, bf(w_router[0].T),
                       batch, seq_len, t.xattn_rows)

    slots = _route(aff_t, cap)
    xs, gates = _gather(slots, aff_t, h2, cap)
    y = _ffn(xs.reshape(N_EXPERTS, batch * cap, d), gates.reshape(N_EXPERTS, batch * cap, 1),
             w_gate[0], w_up[0], w_down[0], t.ffn_cols)
    out = _combine(slots, y.reshape(N_EXPERTS, batch, cap, d), h2, ln3_g, ln3_b, cap, t.combine_rows)
    return out.reshape(batch, seq_len, d)
```

```python
import functools
import math
from typing import NamedTuple

import jax
import jax.numpy as jnp
from jax import lax
from jax.experimental import pallas as pl
from jax.experimental.pallas import tpu as pltpu

F32 = jnp.float32
BF16 = jnp.bfloat16

HG_HEADS = 4
HG_DIM = 128
HG_CHUNK = 64
DA_HEADS = 4
DA_QK_DIM = 64
DA_V_DIM = 128
ROPE_DIM = DA_QK_DIM // 4
ROPE_THETA = 500000.0
XA_HEADS = 4
N_EXPERTS = 16
EC_FACTOR = 2
LN_EPS = 1e-5
RMS_EPS = 1e-6
DEPTH = 1
DEEPNORM_ALPHA = (2.0 * DEPTH) ** 0.25
LAMBDA_INIT = 0.8 - 0.6 * math.exp(-0.3 * 0)
LOG2_E = math.log2(math.e)

LANES_V7X = 128
BF16_SUBLANES_V7X = 16
VMEM_LIMIT_V7X = 56 * 1024 * 1024


class _Tiles(NamedTuple):
    inproj_rows: int
    dattn_queries: int
    dattn_keys: int
    mix_rows: int
    xattn_rows: int
    ffn_cols: int
    combine_rows: int


def _tiles(seq_len):
    fit = lambda rows: min(rows, seq_len)
    return _Tiles(inproj_rows=fit(1024), dattn_queries=fit(2048), dattn_keys=fit(512), mix_rows=fit(1024),
                  xattn_rows=fit(1024), ffn_cols=512, combine_rows=fit(1024))


def _cparams(sem):
    return pltpu.CompilerParams(dimension_semantics=sem, vmem_limit_bytes=VMEM_LIMIT_V7X)


def _layer_norm(x, g, b):
    mu = jnp.mean(x, axis=-1, keepdims=True)
    xc = x - mu
    var = jnp.mean(xc * xc, axis=-1, keepdims=True)
    return xc * lax.rsqrt(var + LN_EPS) * g + b


def _silu(x):
    return x * jax.nn.sigmoid(x)


def _dot(a, b):
    return jnp.dot(a, b, preferred_element_type=F32)


def _dot_nt(a, b):
    return lax.dot_general(a, b, (((1,), (1,)), ((), ())), preferred_element_type=F32)


def _dot_tn(a, b):
    return lax.dot_general(a, b, (((0,), (0,)), ((), ())), preferred_element_type=F32)


def _inproj_kernel(x_ref, g_ref, b_ref, w_ref, c_ref, sa_ref, sb_ref,
                   h0_ref, q_ref, v_ref, gt_ref, ff_ref, fb_ref, dq_ref, dk_ref, dv_ref, hb_buf):
    width = q_ref.shape[1]
    parts = hb_buf.shape[0]
    pr = x_ref.shape[0] // parts
    rows = lambda r: slice(r * pr, (r + 1) * pr)
    widen = lambda ref, r: jnp.concatenate([ref[rows(r), :]] * (width // ref.shape[1]), axis=1)

    def normalize(r):
        h = _layer_norm(x_ref[rows(r), :], g_ref[...], b_ref[...])
        h0_ref[rows(r), :] = h
        hb_buf[r] = h.astype(BF16)

    def project(r, columns):
        proj = lambda c: _dot(hb_buf[r], w_ref[:, c * width:(c + 1) * width])
        rope_c, rope_sa, rope_sb = widen(c_ref, r), widen(sa_ref, r), widen(sb_ref, r)

        def rope(t):
            return (t * rope_c + pltpu.roll(t, width - ROPE_DIM // 2, 1) * rope_sa
                    + pltpu.roll(t, ROPE_DIM // 2, 1) * rope_sb)

        for c in columns:
            u = proj(c)
            if c == 0:
                q_ref[rows(r), :] = _silu(u).astype(BF16)
            elif c == 1:
                v_ref[rows(r), :] = u.astype(BF16)
            elif c == 2:
                gt_ref[rows(r), :] = _silu(u).astype(BF16)
            elif c == 3:
                ff_ref[rows(r), :] = u
            elif c == 4:
                fb_ref[rows(r), :] = u
            elif c == 5:
                dq_ref[rows(r), :] = (rope(u) * (DA_QK_DIM ** -0.5 * LOG2_E)).astype(BF16)
            elif c == 6:
                dk_ref[rows(r), :] = rope(u).astype(BF16)
            else:
                dv_ref[:, rows(r)] = u.T.astype(BF16)

    normalize(0)
    for r in range(parts):
        project(r, range(0, 2))
        if r + 1 < parts:
            normalize(r + 1)
        project(r, range(2, 8))


def _rope_lane_tables(seq_len, width):
    half = ROPE_DIM // 2
    inv = 1.0 / (ROPE_THETA ** (jnp.arange(0, ROPE_DIM, 2, dtype=F32) / ROPE_DIM))
    ang = jnp.arange(seq_len, dtype=F32)[:, None] * inv[None, :]
    cos, sin = jnp.cos(ang), jnp.sin(ang)
    pad1 = jnp.ones((seq_len, DA_QK_DIM - ROPE_DIM), F32)
    pad0 = jnp.zeros((seq_len, DA_QK_DIM - ROPE_DIM), F32)
    z = jnp.zeros((seq_len, half), F32)
    reps = width // DA_QK_DIM
    c = jnp.tile(jnp.concatenate([cos, cos, pad1], -1), (1, reps))
    sa = jnp.tile(jnp.concatenate([-sin, z, pad0], -1), (1, reps))
    sb = jnp.tile(jnp.concatenate([z, sin, pad0], -1), (1, reps))
    return c, sa, sb


def _inproj(x2, g, b, w_bf, seq_len, tm):
    n, d = x2.shape
    width = w_bf.shape[1] // 8
    c, sa, sb = _rope_lane_tables(seq_len, LANES_V7X)
    tpb = seq_len // tm
    parts = 4 if tm % 512 == 0 else 1
    row = lambda p, bi: (bi * tpb + p, 0)
    const = lambda p, bi: (0, 0)
    tab = lambda p, bi: (p, 0)
    wide = lambda dt: jax.ShapeDtypeStruct((n, width), dt)
    return pl.pallas_call(
        _inproj_kernel,
        grid=(tpb, n // seq_len),
        in_specs=[pl.BlockSpec((tm, d), row), pl.BlockSpec((1, d), const), pl.BlockSpec((1, d), const),
                  pl.BlockSpec(w_bf.shape, const),
                  pl.BlockSpec((tm, LANES_V7X), tab), pl.BlockSpec((tm, LANES_V7X), tab),
                  pl.BlockSpec((tm, LANES_V7X), tab)],
        out_specs=([pl.BlockSpec((tm, d), row)] + [pl.BlockSpec((tm, width), row)] * 7
                   + [pl.BlockSpec((width, tm), lambda p, bi: (0, bi * tpb + p))]),
        out_shape=[jax.ShapeDtypeStruct((n, d), F32), wide(BF16), wide(BF16), wide(BF16), wide(F32), wide(F32),
                   wide(BF16), wide(BF16), jax.ShapeDtypeStruct((width, n), BF16)],
        scratch_shapes=[pltpu.VMEM((parts, tm // parts, d), BF16)],
        compiler_params=_cparams(("parallel", "parallel")),
        name="inproj",
    )(x2, g, b, w_bf, c, sa, sb)


def _hgrn_kernel(q_ref, v_ref, gt_ref, ff_ref, fb_ref, lbl_ref, ng_ref, o_ref, st_ref, oacc_ref):
    seq_len, width = q_ref.shape
    heads = width // HG_DIM
    ch = HG_CHUNK
    blk = min(4 * ch, seq_len)
    cpb = blk // ch
    nblk = seq_len // blk
    sub_rows = 8
    tiles_per_chunk = ch // sub_rows
    lbl = lbl_ref[...]

    def lower_bound(d):
        l0, l1 = lbl[2 * d:2 * d + 1], lbl[2 * d + 1:2 * d + 2]
        m = jnp.maximum(l0, l1)
        e0, e1 = jnp.exp(l0 - m), jnp.exp(l1 - m)
        return e0 / (e0 + e1)

    lbs = (lower_bound(0), lower_bound(1))
    st_ref[...] = jnp.zeros_like(st_ref)
    oacc_ref[...] = jnp.zeros_like(oacc_ref)
    sub = lax.broadcasted_iota(jnp.int32, (blk // sub_rows, sub_rows, width), 1)
    ri = lax.broadcasted_iota(jnp.int32, (blk, blk), 0)
    ci = lax.broadcasted_iota(jnp.int32, (blk, blk), 1)
    same_chunk = (ri // ch) == (ci // ch)
    masks = (same_chunk & (ri >= ci), same_chunk & (ri <= ci))
    row_chunk = (lax.broadcasted_iota(jnp.int32, (blk, HG_DIM), 0) // ch).astype(BF16)
    in_chunk = [row_chunk == j for j in range(cpb)]
    sl = lambda hh: slice(hh * HG_DIM, (hh + 1) * HG_DIM)

    def cumprod(x, backward):
        x3 = x.reshape(blk // sub_rows, sub_rows, width)
        s = 1
        while s < sub_rows:
            if backward:
                x3 = x3 * jnp.where(sub < sub_rows - s, pltpu.roll(x3, sub_rows - s, 1), 1.0)
            else:
                x3 = x3 * jnp.where(sub >= s, pltpu.roll(x3, s, 1), 1.0)
            s *= 2
        edge = 0 if backward else sub_rows - 1
        out = [None] * (blk // sub_rows)
        for c in range(cpb):
            carry = None
            order = range(tiles_per_chunk - 1, -1, -1) if backward else range(tiles_per_chunk)
            for t in order:
                idx = c * tiles_per_chunk + t
                out[idx] = x3[idx] if carry is None else x3[idx] * carry
                carry = out[idx][edge:edge + 1]
        return jnp.concatenate(out, axis=0)

    class Item:
        pass

    def prep(d, i):
        it = Item()
        it.d = d
        bi = i if d == 0 else nblk - 1 - i
        it.rows = pl.ds(pl.multiple_of(bi * blk, blk), blk)
        uf = (ff_ref if d == 0 else fb_ref)[it.rows, :]
        f = lbs[d] + (1.0 - lbs[d]) * jax.nn.sigmoid(uf)
        g = cumprod(f, d == 1)
        edge = ch - 1 if d == 0 else 0
        it.decays = [g[j * ch + edge:j * ch + edge + 1] for j in range(cpb)]
        kin = (1.0 - f) / g
        it.k_out = (kin * jnp.concatenate([jnp.broadcast_to(dj, (ch, width)) for dj in it.decays], axis=0)
                    ).astype(BF16)
        it.q_in = (q_ref[it.rows, :].astype(F32) * g).astype(BF16)
        it.k_in = kin.astype(BF16)
        it.vv = v_ref[it.rows, :]
        return it

    def scores(it):
        it.scores = [_dot_nt(it.q_in[:, sl(hh)], it.k_in[:, sl(hh)]) for hh in range(heads)]
        it.incs = []
        for hh in range(heads):
            k_diag = jnp.concatenate([jnp.where(m, it.k_out[:, sl(hh)], jnp.zeros((), BF16)) for m in in_chunk],
                                     axis=1)
            it.incs.append(_dot_tn(it.vv[:, sl(hh)], k_diag))

    def intra(it):
        it.intra = [_dot(jnp.where(masks[it.d], it.scores[hh], 0.0).astype(BF16), it.vv[:, sl(hh)])
                    for hh in range(heads)]

    def scan(it):
        d = it.d
        st = [st_ref[d * heads + hh] for hh in range(heads)]
        it.inter = [[None] * cpb for _ in range(heads)]
        for step in range(cpb):
            j = step if d == 0 else cpb - 1 - step
            for hh in range(heads):
                it.inter[hh][j] = _dot_nt(it.q_in[j * ch:(j + 1) * ch, sl(hh)], st[hh].astype(BF16))
            for hh in range(heads):
                st[hh] = st[hh] * it.decays[j][:, sl(hh)] + it.incs[hh][:, j * HG_DIM:(j + 1) * HG_DIM]
        for hh in range(heads):
            st_ref[d * heads + hh] = st[hh]

    def write(it):
        for hh in range(heads):
            oacc_ref[it.rows, sl(hh)] += it.intra[hh] + jnp.concatenate(it.inter[hh], axis=0)

    def body(i, carry):
        fwd = prep(0, i)
        scores(fwd)
        bwd = prep(1, i)
        intra(fwd)
        scores(bwd)
        scan(fwd)
        intra(bwd)
        write(fwd)
        scan(bwd)
        write(bwd)
        return carry

    lax.fori_loop(0, nblk, body, 0)

    rb = min(256, seq_len)

    def finish(i, carry):
        rows = pl.ds(pl.multiple_of(i * rb, rb), rb)
        o = oacc_ref[rows, :]
        gate = gt_ref[rows, :].astype(F32)
        ng = ng_ref[...]
        for hh in range(heads):
            sl = slice(hh * HG_DIM, (hh + 1) * HG_DIM)
            oh = o[:, sl]
            ms = jnp.mean(oh * oh, axis=-1, keepdims=True)
            o_ref[rows, sl] = (oh * lax.rsqrt(ms + RMS_EPS) * ng[:, sl] * gate[:, sl]).astype(BF16)
        return carry

    lax.fori_loop(0, seq_len // rb, finish, 0)


def _hgrn(q, v, gt, ff, fb, lb_logits, norm_g, batch, seq_len):
    width = q.shape[1]
    hw = width
    sp = lambda a: a.reshape(batch, seq_len, width)
    blk = pl.BlockSpec((None, seq_len, hw), lambda b, h: (b, 0, h))
    return pl.pallas_call(
        _hgrn_kernel,
        grid=(batch, width // hw),
        in_specs=[blk, blk, blk, blk, blk,
                  pl.BlockSpec((4, hw), lambda b, h: (0, h)), pl.BlockSpec((1, hw), lambda b, h: (0, h))],
        out_specs=blk,
        out_shape=jax.ShapeDtypeStruct((batch, seq_len, width), BF16),
        scratch_shapes=[pltpu.VMEM((2 * (hw // HG_DIM), HG_DIM, HG_DIM), F32), pltpu.VMEM((seq_len, hw), F32)],
        compiler_params=_cparams(("parallel", "parallel")),
        name="hgrn",
    )(sp(q), sp(v), sp(gt), sp(ff), sp(fb), lb_logits.reshape(4, width), norm_g.reshape(1, width))


def _dattn_kernel(q_ref, k_ref, vt_ref, lam_ref, sg_ref, o_ref, s_buf, e_buf, acc_buf, *, kc):
    lv = lam_ref[...]
    lam = (jnp.exp(jnp.sum(lv[0:1] * lv[1:2], axis=-1, keepdims=True))
           - jnp.exp(jnp.sum(lv[2:3] * lv[3:4], axis=-1, keepdims=True)) + LAMBDA_INIT)
    q = q_ref[...]
    lane = lax.broadcasted_iota(jnp.int32, q.shape, 1)
    zero = jnp.zeros_like(q)
    qs = (jnp.where(lane < DA_QK_DIM, q, zero), jnp.where(lane >= DA_QK_DIM, q, zero))
    items = [(j, c) for j in range(k_ref.shape[0] // kc) for c in (0, 1)]
    nbuf = s_buf.shape[0]
    dv = vt_ref.shape[0]
    ones = jnp.ones((acc_buf.shape[1] - dv, kc), BF16)
    m, alpha = [None, None], {}

    def scores(i):
        j, c = items[i]
        s_buf[i % nbuf] = _dot_nt(k_ref[j * kc:(j + 1) * kc, :], qs[c])

    def exponentials(i):
        j, c = items[i]
        s = s_buf[i % nbuf]
        mj = jnp.max(s, axis=0, keepdims=True)
        if j == 0:
            m[c], alpha[i] = mj, None
        else:
            mn = jnp.maximum(m[c], mj)
            alpha[i] = jnp.exp2(m[c] - mn)
            m[c] = mn
        e_buf[i % nbuf] = jnp.exp2(s - m[c]).astype(BF16)

    def values(i):
        j, c = items[i]
        pv = _dot(jnp.concatenate([vt_ref[:, j * kc:(j + 1) * kc], ones], axis=0), e_buf[i % nbuf])
        acc_buf[c] = pv if alpha[i] is None else alpha[i] * acc_buf[c] + pv

    scores(0)
    scores(1)
    exponentials(0)
    for i in range(len(items)):
        if i + 2 < len(items):
            scores(i + 2)
        if i + 1 < len(items):
            exponentials(i + 1)
        values(i)
    a0, a1 = acc_buf[0], acc_buf[1]
    ot = a0[:dv] * (1.0 / a0[dv:dv + 1]) - a1[:dv] * (lam / a1[dv:dv + 1])
    ms = jnp.mean(ot * ot, axis=0, keepdims=True)
    y = ot * lax.rsqrt(ms + RMS_EPS) * (sg_ref[...] * (1.0 - LAMBDA_INIT))
    o_ref[...] = y.T.astype(BF16)


def _dattn(dq, dk, dvt, lam_vecs, subln_g, batch, seq_len, tq, kc):
    width = dq.shape[1]
    hw = DA_V_DIM
    sp = lambda a: a.reshape(batch, seq_len, width)
    kk = pl.BlockSpec((None, seq_len, hw), lambda b, h, i: (b, 0, h))
    qo = pl.BlockSpec((None, tq, hw), lambda b, h, i: (b, i, h))
    nbuf = 4
    return pl.pallas_call(
        functools.partial(_dattn_kernel, kc=kc),
        grid=(batch, width // hw, seq_len // tq),
        in_specs=[qo, kk, pl.BlockSpec((hw, seq_len), lambda b, h, i: (h, b)),
                  pl.BlockSpec(lam_vecs.shape, lambda b, h, i: (0, 0)),
                  pl.BlockSpec((hw, 1), lambda b, h, i: (0, 0))],
        out_specs=qo,
        out_shape=jax.ShapeDtypeStruct((batch, seq_len, width), BF16),
        scratch_shapes=[pltpu.VMEM((nbuf, kc, tq), F32), pltpu.VMEM((nbuf, kc, tq), BF16),
                        pltpu.VMEM((2, hw + BF16_SUBLANES_V7X, tq), F32)],
        compiler_params=_cparams(("parallel", "parallel", "parallel")),
        name="dattn",
    )(sp(dq), sp(dk), dvt, lam_vecs, subln_g.reshape(hw, 1))


def _mixproj_kernel(hg_ref, da_ref, h0_ref, w_ref, g_ref, b_ref, o_ref, mix_buf, *, parts):
    half = hg_ref.shape[1]
    pr = hg_ref.shape[0] // parts
    rows = lambda r: slice(r * pr, (r + 1) * pr)

    def project(r):
        mix_buf[r % 2] = _dot(hg_ref[rows(r), :], w_ref[:half, :]) + _dot(da_ref[rows(r), :], w_ref[half:, :])

    def normalize(r):
        o_ref[rows(r), :] = _layer_norm(DEEPNORM_ALPHA * h0_ref[rows(r), :] + mix_buf[r % 2], g_ref[...], b_ref[...])

    project(0)
    for r in range(parts):
        if r + 1 < parts:
            project(r + 1)
        normalize(r)


def _mixproj(hg, da, h0, w_bf, g, b, tm):
    n, d = h0.shape
    half = hg.shape[1]
    parts = 4 if tm % 64 == 0 else 1
    row = lambda i: (i, 0)
    const = lambda i: (0, 0)
    return pl.pallas_call(
        functools.partial(_mixproj_kernel, parts=parts),
        grid=(n // tm,),
        in_specs=[pl.BlockSpec((tm, half), row), pl.BlockSpec((tm, half), row), pl.BlockSpec((tm, d), row),
                  pl.BlockSpec(w_bf.shape, const), pl.BlockSpec((1, d), const), pl.BlockSpec((1, d), const)],
        out_specs=pl.BlockSpec((tm, d), row),
        out_shape=jax.ShapeDtypeStruct((n, d), F32),
        scratch_shapes=[pltpu.VMEM((2, tm // parts, d), F32)],
        compiler_params=_cparams(("parallel",)),
        name="mixproj",
    )(hg, da, h0, w_bf, g, b)


def _kvproj_kernel(m_ref, wk_ref, wv_ref, k_ref, v_ref):
    mb = m_ref[...].astype(BF16)
    k_ref[...] = _dot(mb, wk_ref[...]).astype(BF16)
    v_ref[...] = _dot(mb, wv_ref[...]).astype(BF16)


def _kvproj(mem, wk_bf, wv_bf):
    batch, m, d = mem.shape
    blk = pl.BlockSpec((None, m, d), lambda b: (b, 0, 0))
    const = pl.BlockSpec((d, d), lambda b: (0, 0))
    return pl.pallas_call(
        _kvproj_kernel,
        grid=(batch,),
        in_specs=[blk, const, const],
        out_specs=[blk, blk],
        out_shape=[jax.ShapeDtypeStruct((batch, m, d), BF16)] * 2,
        compiler_params=_cparams(("parallel",)),
        name="kvproj",
    )(mem, wk_bf, wv_bf)


def _xattn_kernel(h_ref, k_ref, v_ref, wq_ref, wo_ref, g_ref, b_ref, wr_ref, h2_ref, aff_ref):
    h1 = h_ref[...]
    d = h1.shape[1]
    hd = d // XA_HEADS
    q = (_dot(h1.astype(BF16), wq_ref[...]) * (hd ** -0.5)).astype(BF16)
    outs = []
    for i in range(XA_HEADS):
        sl = slice(i * hd, (i + 1) * hd)
        s = _dot_nt(q[:, sl], k_ref[:, sl])
        e = jnp.exp(s - jnp.max(s, axis=-1, keepdims=True))
        p = (e / jnp.sum(e, axis=-1, keepdims=True)).astype(BF16)
        outs.append(_dot(p, v_ref[:, sl]).astype(BF16))
    xa = _dot(jnp.concatenate(outs, axis=-1), wo_ref[...])
    h2 = _layer_norm(DEEPNORM_ALPHA * h1 + xa, g_ref[...], b_ref[...])
    h2_ref[...] = h2
    logits = _dot_nt(wr_ref[...], h2.astype(BF16))
    e = jnp.exp(logits - jnp.max(logits, axis=0, keepdims=True))
    aff_ref[...] = e / jnp.sum(e, axis=0, keepdims=True)


def _xattn(h1, kx, vx, wq_bf, wo_bf, g, b, wr_t_bf, batch, seq_len, tm):
    n, d = h1.shape
    m = kx.shape[1]
    ne = wr_t_bf.shape[0]
    tiles = seq_len // tm
    row = pl.BlockSpec((tm, d), lambda bi, i: (bi * tiles + i, 0))
    kv = pl.BlockSpec((None, m, d), lambda bi, i: (bi, 0, 0))
    const = lambda shape: pl.BlockSpec(shape, lambda bi, i: (0, 0))
    return pl.pallas_call(
        _xattn_kernel,
        grid=(batch, tiles),
        in_specs=[row, kv, kv, const((d, d)), const((d, d)), const((1, d)), const((1, d)), const((ne, d))],
        out_specs=[row, pl.BlockSpec((None, ne, tm), lambda bi, i: (bi, 0, i))],
        out_shape=[jax.ShapeDtypeStruct((n, d), F32), jax.ShapeDtypeStruct((batch, ne, seq_len), F32)],
        compiler_params=_cparams(("parallel", "parallel")),
        name="xattn",
    )(h1, kx, vx, wq_bf, wo_bf, g, b, wr_t_bf)


def _route_kernel(aff_ref, slot_ref, *, cap):
    aff = aff_ref[...]
    ne, seq_len = aff.shape
    nblk = seq_len // LANES_V7X

    def count(mask):
        return jnp.sum(jnp.where(mask, 1.0, 0.0), axis=-1, keepdims=True)

    bits = jnp.zeros((ne, 1), jnp.int32)
    for bit in range(30, -1, -1):
        cand = bits | (1 << bit)
        bits = jnp.where(count(aff >= pltpu.bitcast(cand, F32)) >= cap, cand, bits)
    thr = pltpu.bitcast(bits, F32)

    ri = lax.broadcasted_iota(jnp.int32, (LANES_V7X, LANES_V7X), 0)
    ci = lax.broadcasted_iota(jnp.int32, (LANES_V7X, LANES_V7X), 1)
    tri = jnp.where(ri <= ci, 1.0, 0.0).astype(BF16)

    def prefix_count(mask):
        m = jnp.where(mask, 1.0, 0.0)
        stack = jnp.concatenate([m[:, j * LANES_V7X:(j + 1) * LANES_V7X] for j in range(nblk)], axis=0)
        within = _dot(stack.astype(BF16), tri)
        pieces, carry = [], jnp.zeros((ne, 1), F32)
        for j in range(nblk):
            blk = within[j * ne:(j + 1) * ne]
            pieces.append(blk + carry)
            carry = carry + blk[:, LANES_V7X - 1:LANES_V7X]
        return jnp.concatenate(pieces, axis=1), m

    gt = aff > thr
    eq = aff == thr
    need = cap - count(gt)
    eq_incl, eq_f = prefix_count(eq)
    sel = gt | (eq & ((eq_incl - eq_f) < need))
    sel_incl, sel_f = prefix_count(sel)
    slot_ref[...] = jnp.where(sel, sel_incl - sel_f, -1.0).astype(jnp.int32)


def _route(aff_t, cap):
    batch, ne, seq_len = aff_t.shape
    blk = pl.BlockSpec((None, ne, seq_len), lambda b: (b, 0, 0))
    return pl.pallas_call(
        functools.partial(_route_kernel, cap=cap),
        grid=(batch,),
        in_specs=[blk],
        out_specs=blk,
        out_shape=jax.ShapeDtypeStruct((batch, ne, seq_len), jnp.int32),
        compiler_params=_cparams(("parallel",)),
        name="route",
    )(aff_t)


def _gather_kernel(slot_ref, aff_ref, h_ref, xs_ref, gate_ref, *, cap):
    hb = h_ref[...].astype(BF16)
    ne, seq_len = slot_ref.shape
    rid = lax.broadcasted_iota(jnp.int32, (cap, seq_len), 0)
    for e in range(ne):
        pick = slot_ref[e:e + 1, :] == rid
        xs_ref[e] = _dot(jnp.where(pick, 1.0, 0.0).astype(BF16), hb).astype(BF16)
        gate_ref[e] = jnp.sum(jnp.where(pick, aff_ref[e:e + 1, :], 0.0), axis=-1, keepdims=True)


def _gather(slots, aff_t, h2, cap):
    batch, ne, seq_len = slots.shape
    d = h2.shape[1]
    blk = pl.BlockSpec((None, ne, seq_len), lambda b: (b, 0, 0))
    return pl.pallas_call(
        functools.partial(_gather_kernel, cap=cap),
        grid=(batch,),
        in_specs=[blk, blk, pl.BlockSpec((seq_len, d), lambda b: (b, 0))],
        out_specs=[pl.BlockSpec((ne, None, cap, d), lambda b: (0, b, 0, 0)),
                   pl.BlockSpec((ne, None, cap, 1), lambda b: (0, b, 0, 0))],
        out_shape=[jax.ShapeDtypeStruct((ne, batch, cap, d), BF16), jax.ShapeDtypeStruct((ne, batch, cap, 1), F32)],
        compiler_params=_cparams(("parallel",)),
        name="gather",
    )(slots, aff_t, h2)


def _ffn_kernel(xs_ref, gate_ref, wg_ref, wu_ref, wd_ref, y_ref, acc_ref, g_buf, u_buf, a_buf):
    c = pl.program_id(1)

    @pl.when(c == 0)
    def _():
        acc_ref[...] = jnp.zeros_like(acc_ref)

    wg, wu, wd = wg_ref[...].astype(BF16), wu_ref[...].astype(BF16), wd_ref[...].astype(BF16)
    parts = g_buf.shape[0]
    pr = xs_ref.shape[0] // parts
    rows = lambda r: slice(r * pr, (r + 1) * pr)

    def gate_up(r):
        xs = xs_ref[rows(r), :]
        g_buf[r] = _dot(xs, wg)
        u_buf[r] = _dot(xs, wu)

    def down(r):
        a_buf[r] = (_silu(g_buf[r]) * u_buf[r]).astype(BF16)
        acc_ref[rows(r), :] += _dot(a_buf[r], wd)

    gate_up(0)
    for r in range(parts):
        if r + 1 < parts:
            gate_up(r + 1)
        down(r)

    @pl.when(c == pl.num_programs(1) - 1)
    def _():
        y_ref[...] = (acc_ref[...] * gate_ref[...]).astype(BF16)


def _ffn(xs, gates, w_gate, w_up, w_down, fc):
    ne, rows, d = xs.shape
    dff = w_gate.shape[2]
    parts = 2 if rows % 32 == 0 else 1
    return pl.pallas_call(
        _ffn_kernel,
        grid=(ne, dff // fc),
        in_specs=[pl.BlockSpec((None, rows, d), lambda e, c: (e, 0, 0)),
                  pl.BlockSpec((None, rows, 1), lambda e, c: (e, 0, 0)),
                  pl.BlockSpec((None, d, fc), lambda e, c: (e, 0, c)),
                  pl.BlockSpec((None, d, fc), lambda e, c: (e, 0, c)),
                  pl.BlockSpec((None, fc, d), lambda e, c: (e, c, 0))],
        out_specs=pl.BlockSpec((None, rows, d), lambda e, c: (e, 0, 0)),
        out_shape=jax.ShapeDtypeStruct((ne, rows, d), BF16),
        scratch_shapes=[pltpu.VMEM((rows, d), F32), pltpu.VMEM((parts, rows // parts, fc), F32),
                        pltpu.VMEM((parts, rows // parts, fc), F32), pltpu.VMEM((parts, rows // parts, fc), BF16)],
        compiler_params=_cparams(("parallel", "arbitrary")),
        name="ffn",
    )(xs, gates, w_gate, w_up, w_down)


def _combine_kernel(slot_ref, y_ref, h_ref, g_ref, b_ref, o_ref, *, cap):
    ne, tl = slot_ref.shape
    slot_t = slot_ref[...].astype(F32).T
    lane = lax.broadcasted_iota(jnp.int32, (tl, cap), 1).astype(F32)
    place = jnp.concatenate(
        [jnp.where(slot_t[:, e:e + 1] == lane, 1.0, 0.0).astype(BF16) for e in range(ne)], axis=1)
    moe = _dot(place, y_ref[...].reshape(ne * cap, y_ref.shape[-1]))
    o_ref[...] = _layer_norm(DEEPNORM_ALPHA * h_ref[...] + moe, g_ref[...], b_ref[...])


def _combine(slots, y, h2, g, b, cap, tl):
    batch, ne, seq_len = slots.shape
    d = h2.shape[1]
    tiles = seq_len // tl
    row = pl.BlockSpec((tl, d), lambda bi, i: (bi * tiles + i, 0))
    const = pl.BlockSpec((1, d), lambda bi, i: (0, 0))
    return pl.pallas_call(
        functools.partial(_combine_kernel, cap=cap),
        grid=(batch, tiles),
        in_specs=[pl.BlockSpec((None, ne, tl), lambda bi, i: (bi, 0, i)),
                  pl.BlockSpec((ne, None, cap, d), lambda bi, i: (0, bi, 0, 0)),
                  row, const, const],
        out_specs=row,
        out_shape=jax.ShapeDtypeStruct((batch * seq_len, d), F32),
        compiler_params=_cparams(("parallel", "parallel")),
        name="combine",
    )(slots, y, h2, g, b)


def kernel(x, mem, emb_ln_g, emb_ln_b, w_in, hg_lb_logits, hg_norm_g, da_lambda_q1, da_lambda_k1, da_lambda_q2,
           da_lambda_k2, da_subln_g, w_mix_out, ln1_g, ln1_b, xa_wq, xa_wk, xa_wv, xa_wo, ln2_g, ln2_b, w_router,
           w_gate, w_up, w_down, ln3_g, ln3_b):
    batch, seq_len, d = x.shape
    assert w_in.shape[0] == DEPTH and seq_len % HG_CHUNK == 0 and seq_len % LANES_V7X == 0
    n = batch * seq_len
    cap = EC_FACTOR * seq_len // N_EXPERTS
    vec = lambda a: a.reshape(1, -1)
    bf = lambda a: a.astype(BF16)

    t = _tiles(seq_len)

    h0, q, v, gt, ff, fb, dq, dk, dv = _inproj(
        x.reshape(n, d), vec(emb_ln_g), vec(emb_ln_b), bf(w_in[0]), seq_len, t.inproj_rows)
    hg = _hgrn(q, v, gt, ff, fb, hg_lb_logits, hg_norm_g[0], batch, seq_len)
    lam_vecs = jnp.stack([da_lambda_q1[0], da_lambda_k1[0], da_lambda_q2[0], da_lambda_k2[0]])
    da = _dattn(dq, dk, dv, lam_vecs, da_subln_g[0], batch, seq_len, t.dattn_queries, t.dattn_keys)
    h1 = _mixproj(hg.reshape(n, -1), da.reshape(n, -1), h0, bf(w_mix_out[0]), ln1_g, ln1_b, t.mix_rows)

    kx, vx = _kvproj(mem, bf(xa_wk[0]), bf(xa_wv[0]))
    h2, aff_t = _xattn(h1, kx, vx, bf(xa_wq[0]), bf(xa_wo[0]), ln2_g, ln2_b, bf(w_router[0].T),
                       batch, seq_len, t.xattn_rows)

    slots = _route(aff_t, cap)
    xs, gates = _gather(slots, aff_t, h2, cap)
    y = _ffn(xs.reshape(N_EXPERTS, batch * cap, d), gates.reshape(N_EXPERTS, batch * cap, 1),
             w_gate[0], w_up[0], w_down[0], t.ffn_cols)
    out = _combine(slots, y.reshape(N_EXPERTS, batch, cap, d), h2, ln3_g, ln3_b, cap, t.combine_rows)
    return out.reshape(batch, seq_len, d)
```

```python
import functools
import math
from typing import NamedTuple

import jax
import jax.numpy as jnp
from jax import lax
from jax.experimental import pallas as pl
from jax.experimental.pallas import tpu as pltpu

F32 = jnp.float32
BF16 = jnp.bfloat16

HG_HEADS = 4
HG_DIM = 128
HG_CHUNK = 64
DA_HEADS = 4
DA_QK_DIM = 64
DA_V_DIM = 128
ROPE_DIM = DA_QK_DIM // 4
ROPE_THETA = 500000.0
XA_HEADS = 4
N_EXPERTS = 16
EC_FACTOR = 2
LN_EPS = 1e-5
RMS_EPS = 1e-6
DEPTH = 1
DEEPNORM_ALPHA = (2.0 * DEPTH) ** 0.25
LAMBDA_INIT = 0.8 - 0.6 * math.exp(-0.3 * 0)
LOG2_E = math.log2(math.e)

LANES_V7X = 128
BF16_SUBLANES_V7X = 16
VMEM_LIMIT_V7X = 56 * 1024 * 1024


class _Tiles(NamedTuple):
    inproj_rows: int
    dattn_queries: int
    dattn_keys: int
    mix_rows: int
    xattn_rows: int
    ffn_cols: int
    combine_rows: int


def _tiles(seq_len):
    fit = lambda rows: min(rows, seq_len)
    return _Tiles(inproj_rows=fit(1024), dattn_queries=fit(2048), dattn_keys=fit(512), mix_rows=fit(1024),
                  xattn_rows=fit(1024), ffn_cols=512, combine_rows=fit(1024))


def _cparams(sem):
    return pltpu.CompilerParams(dimension_semantics=sem, vmem_limit_bytes=VMEM_LIMIT_V7X)


def _layer_norm(x, g, b):
    mu = jnp.mean(x, axis=-1, keepdims=True)
    xc = x - mu
    var = jnp.mean(xc * xc, axis=-1, keepdims=True)
    return xc * lax.rsqrt(var + LN_EPS) * g + b


def _silu(x):
    return x * jax.nn.sigmoid(x)


def _dot(a, b):
    return jnp.dot(a, b, preferred_element_type=F32)


def _dot_nt(a, b):
    return lax.dot_general(a, b, (((1,), (1,)), ((), ())), preferred_element_type=F32)


def _dot_tn(a, b):
    return lax.dot_general(a, b, (((0,), (0,)), ((), ())), preferred_element_type=F32)


def _inproj_kernel(x_ref, g_ref, b_ref, w_ref, c_ref, sa_ref, sb_ref,
                   h0_ref, q_ref, v_ref, gt_ref, ff_ref, fb_ref, dq_ref, dk_ref, dv_ref, hb_buf):
    width = q_ref.shape[1]
    parts = hb_buf.shape[0]
    pr = x_ref.shape[0] // parts
    rows = lambda r: slice(r * pr, (r + 1) * pr)
    widen = lambda ref, r: jnp.concatenate([ref[rows(r), :]] * (width // ref.shape[1]), axis=1)

    def normalize(r):
        h = _layer_norm(x_ref[rows(r), :], g_ref[...], b_ref[...])
        h0_ref[rows(r), :] = h
        hb_buf[r] = h.astype(BF16)

    def project(r, columns):
        proj = lambda c: _dot(hb_buf[r], w_ref[:, c * width:(c + 1) * width])
        rope_c, rope_sa, rope_sb = widen(c_ref, r), widen(sa_ref, r), widen(sb_ref, r)

        def rope(t):
            return (t * rope_c + pltpu.roll(t, width - ROPE_DIM // 2, 1) * rope_sa
                    + pltpu.roll(t, ROPE_DIM // 2, 1) * rope_sb)

        for c in columns:
            u = proj(c)
            if c == 0:
                q_ref[rows(r), :] = _silu(u).astype(BF16)
            elif c == 1:
                v_ref[rows(r), :] = u.astype(BF16)
            elif c == 2:
                gt_ref[rows(r), :] = _silu(u).astype(BF16)
            elif c == 3:
                ff_ref[rows(r), :] = u
            elif c == 4:
                fb_ref[rows(r), :] = u
            elif c == 5:
                dq_ref[rows(r), :] = (rope(u) * (DA_QK_DIM ** -0.5 * LOG2_E)).astype(BF16)
            elif c == 6:
                dk_ref[rows(r), :] = rope(u).astype(BF16)
            else:
                dv_ref[:, rows(r)] = u.T.astype(BF16)

    normalize(0)
    for r in range(parts):
        project(r, range(0, 2))
        if r + 1 < parts:
            normalize(r + 1)
        project(r, range(2, 8))


def _rope_lane_tables(seq_len, width):
    half = ROPE_DIM // 2
    inv = 1.0 / (ROPE_THETA ** (jnp.arange(0, ROPE_DIM, 2, dtype=F32) / ROPE_DIM))
    ang = jnp.arange(seq_len, dtype=F32)[:, None] * inv[None, :]
    cos, sin = jnp.cos(ang), jnp.sin(ang)
    pad1 = jnp.ones((seq_len, DA_QK_DIM - ROPE_DIM), F32)
    pad0 = jnp.zeros((seq_len, DA_QK_DIM - ROPE_DIM), F32)
    z = jnp.zeros((seq_len, half), F32)
    reps = width // DA_QK_DIM
    c = jnp.tile(jnp.concatenate([cos, cos, pad1], -1), (1, reps))
    sa = jnp.tile(jnp.concatenate([-sin, z, pad0], -1), (1, reps))
    sb = jnp.tile(jnp.concatenate([z, sin, pad0], -1), (1, reps))
    return c, sa, sb


def _inproj(x2, g, b, w_bf, seq_len, tm):
    n, d = x2.shape
    width = w_bf.shape[1] // 8
    c, sa, sb = _rope_lane_tables(seq_len, LANES_V7X)
    tpb = seq_len // tm
    parts = 4 if tm % 512 == 0 else 1
    row = lambda p, bi: (bi * tpb + p, 0)
    const = lambda p, bi: (0, 0)
    tab = lambda p, bi: (p, 0)
    wide = lambda dt: jax.ShapeDtypeStruct((n, width), dt)
    return pl.pallas_call(
        _inproj_kernel,
        grid=(tpb, n // seq_len),
        in_specs=[pl.BlockSpec((tm, d), row), pl.BlockSpec((1, d), const), pl.BlockSpec((1, d), const),
                  pl.BlockSpec(w_bf.shape, const),
                  pl.BlockSpec((tm, LANES_V7X), tab), pl.BlockSpec((tm, LANES_V7X), tab),
                  pl.BlockSpec((tm, LANES_V7X), tab)],
        out_specs=([pl.BlockSpec((tm, d), row)] + [pl.BlockSpec((tm, width), row)] * 7
                   + [pl.BlockSpec((width, tm), lambda p, bi: (0, bi * tpb + p))]),
        out_shape=[jax.ShapeDtypeStruct((n, d), F32), wide(BF16), wide(BF16), wide(BF16), wide(F32), wide(F32),
                   wide(BF16), wide(BF16), jax.ShapeDtypeStruct((width, n), BF16)],
        scratch_shapes=[pltpu.VMEM((parts, tm // parts, d), BF16)],
        compiler_params=_cparams(("parallel", "parallel")),
        name="inproj",
    )(x2, g, b, w_bf, c, sa, sb)


def _hgrn_kernel(q_ref, v_ref, gt_ref, ff_ref, fb_ref, lbl_ref, ng_ref, o_ref, st_ref, oacc_ref):
    seq_len, width = q_ref.shape
    heads = width // HG_DIM
    ch = HG_CHUNK
    blk = min(4 * ch, seq_len)
    cpb = blk // ch
    nblk = seq_len // blk
    sub_rows = 8
    tiles_per_chunk = ch // sub_rows
    lbl = lbl_ref[...]

    def lower_bound(d):
        l0, l1 = lbl[2 * d:2 * d + 1], lbl[2 * d + 1:2 * d + 2]
        m = jnp.maximum(l0, l1)
        e0, e1 = jnp.exp(l0 - m), jnp.exp(l1 - m)
        return e0 / (e0 + e1)

    lbs = (lower_bound(0), lower_bound(1))
    st_ref[...] = jnp.zeros_like(st_ref)
    oacc_ref[...] = jnp.zeros_like(oacc_ref)
    sub = lax.broadcasted_iota(jnp.int32, (blk // sub_rows, sub_rows, width), 1)
    ri = lax.broadcasted_iota(jnp.int32, (blk, blk), 0)
    ci = lax.broadcasted_iota(jnp.int32, (blk, blk), 1)
    same_chunk = (ri // ch) == (ci // ch)
    masks = (same_chunk & (ri >= ci), same_chunk & (ri <= ci))
    row_chunk = (lax.broadcasted_iota(jnp.int32, (blk, HG_DIM), 0) // ch).astype(BF16)
    in_chunk = [row_chunk == j for j in range(cpb)]
    sl = lambda hh: slice(hh * HG_DIM, (hh + 1) * HG_DIM)

    def cumprod(x, backward):
        x3 = x.reshape(blk // sub_rows, sub_rows, width)
        s = 1
        while s < sub_rows:
            if backward:
                x3 = x3 * jnp.where(sub < sub_rows - s, pltpu.roll(x3, sub_rows - s, 1), 1.0)
            else:
                x3 = x3 * jnp.where(sub >= s, pltpu.roll(x3, s, 1), 1.0)
            s *= 2
        edge = 0 if backward else sub_rows - 1
        out = [None] * (blk // sub_rows)
        for c in range(cpb):
            carry = None
            order = range(tiles_per_chunk - 1, -1, -1) if backward else range(tiles_per_chunk)
            for t in order:
                idx = c * tiles_per_chunk + t
                out[idx] = x3[idx] if carry is None else x3[idx] * carry
                carry = out[idx][edge:edge + 1]
        return jnp.concatenate(out, axis=0)

    class Item:
        pass

    def prep(d, i):
        it = Item()
        it.d = d
        bi = i if d == 0 else nblk - 1 - i
        it.rows = pl.ds(pl.multiple_of(bi * blk, blk), blk)
        uf = (ff_ref if d == 0 else fb_ref)[it.rows, :]
        f = lbs[d] + (1.0 - lbs[d]) * jax.nn.sigmoid(uf)
        g = cumprod(f, d == 1)
        edge = ch - 1 if d == 0 else 0
        it.decays = [g[j * ch + edge:j * ch + edge + 1] for j in range(cpb)]
        kin = (1.0 - f) / g
        it.k_out = (kin * jnp.concatenate([jnp.broadcast_to(dj, (ch, width)) for dj in it.decays], axis=0)
                    ).astype(BF16)
        it.q_in = (q_ref[it.rows, :].astype(F32) * g).astype(BF16)
        it.k_in = kin.astype(BF16)
        it.vv = v_ref[it.rows, :]
        return it

    def scores(it):
        it.scores = [_dot_nt(it.q_in[:, sl(hh)], it.k_in[:, sl(hh)]) for hh in range(heads)]
        it.incs = []
        for hh in range(heads):
            k_diag = jnp.concatenate([jnp.where(m, it.k_out[:, sl(hh)], jnp.zeros((), BF16)) for m in in_chunk],
                                     axis=1)
            it.incs.append(_dot_tn(it.vv[:, sl(hh)], k_diag))

    def intra(it):
        it.intra = [_dot(jnp.where(masks[it.d], it.scores[hh], 0.0).astype(BF16), it.vv[:, sl(hh)])
                    for hh in range(heads)]

    def scan(it):
        d = it.d
        st = [st_ref[d * heads + hh] for hh in range(heads)]
        it.inter = [[None] * cpb for _ in range(heads)]
        for step in range(cpb):
            j = step if d == 0 else cpb - 1 - step
            for hh in range(heads):
                it.inter[hh][j] = _dot_nt(it.q_in[j * ch:(j + 1) * ch, sl(hh)], st[hh].astype(BF16))
            for hh in range(heads):
                st[hh] = st[hh] * it.decays[j][:, sl(hh)] + it.incs[hh][:, j * HG_DIM:(j + 1) * HG_DIM]
        for hh in range(heads):
            st_ref[d * heads + hh] = st[hh]

    def write(it):
        for hh in range(heads):
            oacc_ref[it.rows, sl(hh)] += it.intra[hh] + jnp.concatenate(it.inter[hh], axis=0)

    def body(i, carry):
        fwd = prep(0, i)
        scores(fwd)
        bwd = prep(1, i)
        intra(fwd)
        scores(bwd)
        scan(fwd)
        intra(bwd)
        write(fwd)
        scan(bwd)
        write(bwd)
        return carry

    lax.fori_loop(0, nblk, body, 0, unroll=2 if nblk % 2 == 0 else 1)

    rb = min(256, seq_len)

    def finish(i, carry):
        rows = pl.ds(pl.multiple_of(i * rb, rb), rb)
        o = oacc_ref[rows, :]
        gate = gt_ref[rows, :].astype(F32)
        ng = ng_ref[...]
        for hh in range(heads):
            sl = slice(hh * HG_DIM, (hh + 1) * HG_DIM)
            oh = o[:, sl]
            ms = jnp.mean(oh * oh, axis=-1, keepdims=True)
            o_ref[rows, sl] = (oh * lax.rsqrt(ms + RMS_EPS) * ng[:, sl] * gate[:, sl]).astype(BF16)
        return carry

    lax.fori_loop(0, seq_len // rb, finish, 0)


def _hgrn(q, v, gt, ff, fb, lb_logits, norm_g, batch, seq_len):
    width = q.shape[1]
    hw = width
    sp = lambda a: a.reshape(batch, seq_len, width)
    blk = pl.BlockSpec((None, seq_len, hw), lambda b, h: (b, 0, h))
    return pl.pallas_call(
        _hgrn_kernel,
        grid=(batch, width // hw),
        in_specs=[blk, blk, blk, blk, blk,
                  pl.BlockSpec((4, hw), lambda b, h: (0, h)), pl.BlockSpec((1, hw), lambda b, h: (0, h))],
        out_specs=blk,
        out_shape=jax.ShapeDtypeStruct((batch, seq_len, width), BF16),
        scratch_shapes=[pltpu.VMEM((2 * (hw // HG_DIM), HG_DIM, HG_DIM), F32), pltpu.VMEM((seq_len, hw), F32)],
        compiler_params=_cparams(("parallel", "parallel")),
        name="hgrn",
    )(sp(q), sp(v), sp(gt), sp(ff), sp(fb), lb_logits.reshape(4, width), norm_g.reshape(1, width))


def _dattn_kernel(q_ref, k_ref, vt_ref, lam_ref, sg_ref, o_ref, s_buf, e_buf, acc_buf, *, kc):
    lv = lam_ref[...]
    lam = (jnp.exp(jnp.sum(lv[0:1] * lv[1:2], axis=-1, keepdims=True))
           - jnp.exp(jnp.sum(lv[2:3] * lv[3:4], axis=-1, keepdims=True)) + LAMBDA_INIT)
    q = q_ref[...]
    lane = lax.broadcasted_iota(jnp.int32, q.shape, 1)
    zero = jnp.zeros_like(q)
    qs = (jnp.where(lane < DA_QK_DIM, q, zero), jnp.where(lane >= DA_QK_DIM, q, zero))
    items = [(j, c) for j in range(k_ref.shape[0] // kc) for c in (0, 1)]
    nbuf = s_buf.shape[0]
    dv = vt_ref.shape[0]
    ones = jnp.ones((acc_buf.shape[1] - dv, kc), BF16)
    m, alpha = [None, None], {}

    def scores(i):
        j, c = items[i]
        s_buf[i % nbuf] = _dot_nt(k_ref[j * kc:(j + 1) * kc, :], qs[c])

    def exponentials(i):
        j, c = items[i]
        s = s_buf[i % nbuf]
        mj = jnp.max(s, axis=0, keepdims=True)
        if j == 0:
            m[c], alpha[i] = mj, None
        else:
            mn = jnp.maximum(m[c], mj)
            alpha[i] = jnp.exp2(m[c] - mn)
            m[c] = mn
        e_buf[i % nbuf] = jnp.exp2(s - m[c]).astype(BF16)

    def values(i):
        j, c = items[i]
        pv = _dot(jnp.concatenate([vt_ref[:, j * kc:(j + 1) * kc], ones], axis=0), e_buf[i % nbuf])
        acc_buf[c] = pv if alpha[i] is None else alpha[i] * acc_buf[c] + pv

    scores(0)
    scores(1)
    exponentials(0)
    for i in range(len(items)):
        if i + 2 < len(items):
            scores(i + 2)
        if i + 1 < len(items):
            exponentials(i + 1)
        values(i)
    a0, a1 = acc_buf[0], acc_buf[1]
    ot = a0[:dv] * (1.0 / a0[dv:dv + 1]) - a1[:dv] * (lam / a1[dv:dv + 1])
    ms = jnp.mean(ot * ot, axis=0, keepdims=True)
    y = ot * lax.rsqrt(ms + RMS_EPS) * (sg_ref[...] * (1.0 - LAMBDA_INIT))
    o_ref[...] = y.T.astype(BF16)


def _dattn(dq, dk, dvt, lam_vecs, subln_g, batch, seq_len, tq, kc):
    width = dq.shape[1]
    hw = DA_V_DIM
    sp = lambda a: a.reshape(batch, seq_len, width)
    kk = pl.BlockSpec((None, seq_len, hw), lambda b, h, i: (b, 0, h))
    qo = pl.BlockSpec((None, tq, hw), lambda b, h, i: (b, i, h))
    nbuf = 4
    return pl.pallas_call(
        functools.partial(_dattn_kernel, kc=kc),
        grid=(batch, width // hw, seq_len // tq),
        in_specs=[qo, kk, pl.BlockSpec((hw, seq_len), lambda b, h, i: (h, b)),
                  pl.BlockSpec(lam_vecs.shape, lambda b, h, i: (0, 0)),
                  pl.BlockSpec((hw, 1), lambda b, h, i: (0, 0))],
        out_specs=qo,
        out_shape=jax.ShapeDtypeStruct((batch, seq_len, width), BF16),
        scratch_shapes=[pltpu.VMEM((nbuf, kc, tq), F32), pltpu.VMEM((nbuf, kc, tq), BF16),
                        pltpu.VMEM((2, hw + BF16_SUBLANES_V7X, tq), F32)],
        compiler_params=_cparams(("parallel", "parallel", "parallel")),
        name="dattn",
    )(sp(dq), sp(dk), dvt, lam_vecs, subln_g.reshape(hw, 1))


def _mixproj_kernel(hg_ref, da_ref, h0_ref, w_ref, g_ref, b_ref, o_ref, mix_buf, *, parts):
    half = hg_ref.shape[1]
    pr = hg_ref.shape[0] // parts
    rows = lambda r: slice(r * pr, (r + 1) * pr)

    def project(r):
        mix_buf[r % 2] = _dot(hg_ref[rows(r), :], w_ref[:half, :]) + _dot(da_ref[rows(r), :], w_ref[half:, :])

    def normalize(r):
        o_ref[rows(r), :] = _layer_norm(DEEPNORM_ALPHA * h0_ref[rows(r), :] + mix_buf[r % 2], g_ref[...], b_ref[...])

    project(0)
    for r in range(parts):
        if r + 1 < parts:
            project(r + 1)
        normalize(r)


def _mixproj(hg, da, h0, w_bf, g, b, tm):
    n, d = h0.shape
    half = hg.shape[1]
    parts = 4 if tm % 64 == 0 else 1
    row = lambda i: (i, 0)
    const = lambda i: (0, 0)
    return pl.pallas_call(
        functools.partial(_mixproj_kernel, parts=parts),
        grid=(n // tm,),
        in_specs=[pl.BlockSpec((tm, half), row), pl.BlockSpec((tm, half), row), pl.BlockSpec((tm, d), row),
                  pl.BlockSpec(w_bf.shape, const), pl.BlockSpec((1, d), const), pl.BlockSpec((1, d), const)],
        out_specs=pl.BlockSpec((tm, d), row),
        out_shape=jax.ShapeDtypeStruct((n, d), F32),
        scratch_shapes=[pltpu.VMEM((2, tm // parts, d), F32)],
        compiler_params=_cparams(("parallel",)),
        name="mixproj",
    )(hg, da, h0, w_bf, g, b)


def _kvproj_kernel(m_ref, wk_ref, wv_ref, k_ref, v_ref):
    mb = m_ref[...].astype(BF16)
    k_ref[...] = _dot(mb, wk_ref[...]).astype(BF16)
    v_ref[...] = _dot(mb, wv_ref[...]).astype(BF16)


def _kvproj(mem, wk_bf, wv_bf):
    batch, m, d = mem.shape
    blk = pl.BlockSpec((None, m, d), lambda b: (b, 0, 0))
    const = pl.BlockSpec((d, d), lambda b: (0, 0))
    return pl.pallas_call(
        _kvproj_kernel,
        grid=(batch,),
        in_specs=[blk, const, const],
        out_specs=[blk, blk],
        out_shape=[jax.ShapeDtypeStruct((batch, m, d), BF16)] * 2,
        compiler_params=_cparams(("parallel",)),
        name="kvproj",
    )(mem, wk_bf, wv_bf)


def _xattn_kernel(h_ref, k_ref, v_ref, wq_ref, wo_ref, g_ref, b_ref, wr_ref, h2_ref, aff_ref):
    h1 = h_ref[...]
    d = h1.shape[1]
    hd = d // XA_HEADS
    q = (_dot(h1.astype(BF16), wq_ref[...]) * (hd ** -0.5)).astype(BF16)
    outs = []
    for i in range(XA_HEADS):
        sl = slice(i * hd, (i + 1) * hd)
        s = _dot_nt(q[:, sl], k_ref[:, sl])
        e = jnp.exp(s - jnp.max(s, axis=-1, keepdims=True))
        p = (e / jnp.sum(e, axis=-1, keepdims=True)).astype(BF16)
        outs.append(_dot(p, v_ref[:, sl]).astype(BF16))
    xa = _dot(jnp.concatenate(outs, axis=-1), wo_ref[...])
    h2 = _layer_norm(DEEPNORM_ALPHA * h1 + xa, g_ref[...], b_ref[...])
    h2_ref[...] = h2
    logits = _dot_nt(wr_ref[...], h2.astype(BF16))
    e = jnp.exp(logits - jnp.max(logits, axis=0, keepdims=True))
    aff_ref[...] = e / jnp.sum(e, axis=0, keepdims=True)


def _xattn(h1, kx, vx, wq_bf, wo_bf, g, b, wr_t_bf, batch, seq_len, tm):
    n, d = h1.shape
    m = kx.shape[1]
    ne = wr_t_bf.shape[0]
    tiles = seq_len // tm
    row = pl.BlockSpec((tm, d), lambda bi, i: (bi * tiles + i, 0))
    kv = pl.BlockSpec((None, m, d), lambda bi, i: (bi, 0, 0))
    const = lambda shape: pl.BlockSpec(shape, lambda bi, i: (0, 0))
    return pl.pallas_call(
        _xattn_kernel,
        grid=(batch, tiles),
        in_specs=[row, kv, kv, const((d, d)), const((d, d)), const((1, d)), const((1, d)), const((ne, d))],
        out_specs=[row, pl.BlockSpec((None, ne, tm), lambda bi, i: (bi, 0, i))],
        out_shape=[jax.ShapeDtypeStruct((n, d), F32), jax.ShapeDtypeStruct((batch, ne, seq_len), F32)],
        compiler_params=_cparams(("parallel", "parallel")),
        name="xattn",
    )(h1, kx, vx, wq_bf, wo_bf, g, b, wr_t_bf)


def _route_kernel(aff_ref, slot_ref, *, cap):
    aff = aff_ref[...]
    ne, seq_len = aff.shape
    nblk = seq_len // LANES_V7X

    def count(mask):
        return jnp.sum(jnp.where(mask, 1.0, 0.0), axis=-1, keepdims=True)

    bits = jnp.zeros((ne, 1), jnp.int32)
    for bit in range(30, -1, -1):
        cand = bits | (1 << bit)
        bits = jnp.where(count(aff >= pltpu.bitcast(cand, F32)) >= cap, cand, bits)
    thr = pltpu.bitcast(bits, F32)

    ri = lax.broadcasted_iota(jnp.int32, (LANES_V7X, LANES_V7X), 0)
    ci = lax.broadcasted_iota(jnp.int32, (LANES_V7X, LANES_V7X), 1)
    tri = jnp.where(ri <= ci, 1.0, 0.0).astype(BF16)

    def prefix_count(mask):
        m = jnp.where(mask, 1.0, 0.0)
        stack = jnp.concatenate([m[:, j * LANES_V7X:(j + 1) * LANES_V7X] for j in range(nblk)], axis=0)
        within = _dot(stack.astype(BF16), tri)
        pieces, carry = [], jnp.zeros((ne, 1), F32)
        for j in range(nblk):
            blk = within[j * ne:(j + 1) * ne]
            pieces.append(blk + carry)
            carry = carry + blk[:, LANES_V7X - 1:LANES_V7X]
        return jnp.concatenate(pieces, axis=1), m

    gt = aff > thr
    eq = aff == thr
    need = cap - count(gt)
    eq_incl, eq_f = prefix_count(eq)
    sel = gt | (eq & ((eq_incl - eq_f) < need))
    sel_incl, sel_f = prefix_count(sel)
    slot_ref[...] = jnp.where(sel, sel_incl - sel_f, -1.0).astype(jnp.int32)


def _route(aff_t, cap):
    batch, ne, seq_len = aff_t.shape
    blk = pl.BlockSpec((batch * ne, seq_len), lambda i: (0, 0))
    slots = pl.pallas_call(
        functools.partial(_route_kernel, cap=cap),
        grid=(1,),
        in_specs=[blk],
        out_specs=blk,
        out_shape=jax.ShapeDtypeStruct((batch * ne, seq_len), jnp.int32),
        compiler_params=_cparams(("arbitrary",)),
        name="route",
    )(aff_t.reshape(batch * ne, seq_len))
    return slots.reshape(batch, ne, seq_len)


def _gather_kernel(slot_ref, aff_ref, h_ref, xs_ref, gate_ref, *, cap):
    hb = h_ref[...].astype(BF16)
    ne, seq_len = slot_ref.shape
    rid = lax.broadcasted_iota(jnp.int32, (cap, seq_len), 0)
    for e in range(ne):
        pick = slot_ref[e:e + 1, :] == rid
        xs_ref[e] = _dot(jnp.where(pick, 1.0, 0.0).astype(BF16), hb).astype(BF16)
        gate_ref[e] = jnp.sum(jnp.where(pick, aff_ref[e:e + 1, :], 0.0), axis=-1, keepdims=True)


def _gather(slots, aff_t, h2, cap):
    batch, ne, seq_len = slots.shape
    d = h2.shape[1]
    blk = pl.BlockSpec((None, ne, seq_len), lambda b: (b, 0, 0))
    return pl.pallas_call(
        functools.partial(_gather_kernel, cap=cap),
        grid=(batch,),
        in_specs=[blk, blk, pl.BlockSpec((seq_len, d), lambda b: (b, 0))],
        out_specs=[pl.BlockSpec((ne, None, cap, d), lambda b: (0, b, 0, 0)),
                   pl.BlockSpec((ne, None, cap, 1), lambda b: (0, b, 0, 0))],
        out_shape=[jax.ShapeDtypeStruct((ne, batch, cap, d), BF16), jax.ShapeDtypeStruct((ne, batch, cap, 1), F32)],
        compiler_params=_cparams(("parallel",)),
        name="gather",
    )(slots, aff_t, h2)


def _ffn_kernel(xs_ref, gate_ref, wg_ref, wu_ref, wd_ref, y_ref, acc_ref, g_buf, u_buf, a_buf, *, n_chunks):
    c = pl.program_id(1)
    parts = g_buf.shape[0]
    pr = xs_ref.shape[0] // parts
    rows = lambda r: slice(r * pr, (r + 1) * pr)

    def chunk(first, final):
        wg, wu, wd = wg_ref[...].astype(BF16), wu_ref[...].astype(BF16), wd_ref[...].astype(BF16)

        def gate_up(r):
            xs = xs_ref[rows(r), :]
            g_buf[r] = _dot(xs, wg)
            u_buf[r] = _dot(xs, wu)

        def down(r):
            a_buf[r] = (_silu(g_buf[r]) * u_buf[r]).astype(BF16)
            part = _dot(a_buf[r], wd)
            total = part if first else acc_ref[rows(r), :] + part
            if final:
                y_ref[rows(r), :] = (total * gate_ref[rows(r), :]).astype(BF16)
            else:
                acc_ref[rows(r), :] = total

        gate_up(0)
        for r in range(parts):
            if r + 1 < parts:
                gate_up(r + 1)
            down(r)

    if n_chunks == 1:
        chunk(True, True)
    else:
        pl.when(c == 0)(lambda: chunk(True, False))
        if n_chunks > 2:
            pl.when((c > 0) & (c < n_chunks - 1))(lambda: chunk(False, False))
        pl.when(c == n_chunks - 1)(lambda: chunk(False, True))


def _ffn(xs, gates, w_gate, w_up, w_down, fc):
    ne, rows, d = xs.shape
    dff = w_gate.shape[2]
    parts = 2 if rows % 32 == 0 else 1
    return pl.pallas_call(
        functools.partial(_ffn_kernel, n_chunks=dff // fc),
        grid=(ne, dff // fc),
        in_specs=[pl.BlockSpec((None, rows, d), lambda e, c: (e, 0, 0)),
                  pl.BlockSpec((None, rows, 1), lambda e, c: (e, 0, 0)),
                  pl.BlockSpec((None, d, fc), lambda e, c: (e, 0, c)),
                  pl.BlockSpec((None, d, fc), lambda e, c: (e, 0, c)),
                  pl.BlockSpec((None, fc, d), lambda e, c: (e, c, 0))],
        out_specs=pl.BlockSpec((None, rows, d), lambda e, c: (e, 0, 0)),
        out_shape=jax.ShapeDtypeStruct((ne, rows, d), BF16),
        scratch_shapes=[pltpu.VMEM((rows, d), F32), pltpu.VMEM((parts, rows // parts, fc), F32),
                        pltpu.VMEM((parts, rows // parts, fc), F32), pltpu.VMEM((parts, rows // parts, fc), BF16)],
        compiler_params=_cparams(("parallel", "arbitrary")),
        name="ffn",
    )(xs, gates, w_gate, w_up, w_down)


def _combine_kernel(slot_ref, y_ref, h_ref, g_ref, b_ref, o_ref, *, cap):
    ne, tl = slot_ref.shape
    slot_t = slot_ref[...].astype(F32).T
    lane = lax.broadcasted_iota(jnp.int32, (tl, cap), 1).astype(F32)
    place = jnp.concatenate(
        [jnp.where(slot_t[:, e:e + 1] == lane, 1.0, 0.0).astype(BF16) for e in range(ne)], axis=1)
    moe = _dot(place, y_ref[...].reshape(ne * cap, y_ref.shape[-1]))
    o_ref[...] = _layer_norm(DEEPNORM_ALPHA * h_ref[...] + moe, g_ref[...], b_ref[...])


def _combine(slots, y, h2, g, b, cap, tl):
    batch, ne, seq_len = slots.shape
    d = h2.shape[1]
    tiles = seq_len // tl
    row = pl.BlockSpec((tl, d), lambda bi, i: (bi * tiles + i, 0))
    const = pl.BlockSpec((1, d), lambda bi, i: (0, 0))
    return pl.pallas_call(
        functools.partial(_combine_kernel, cap=cap),
        grid=(batch, tiles),
        in_specs=[pl.BlockSpec((None, ne, tl), lambda bi, i: (bi, 0, i)),
                  pl.BlockSpec((ne, None, cap, d), lambda bi, i: (0, bi, 0, 0)),
                  row, const, const],
        out_specs=row,
        out_shape=jax.ShapeDtypeStruct((batch * seq_len, d), F32),
        compiler_params=_cparams(("parallel", "parallel")),
        name="combine",
    )(slots, y, h2, g, b)


def kernel(x, mem, emb_ln_g, emb_ln_b, w_in, hg_lb_logits, hg_norm_g, da_lambda_q1, da_lambda_k1, da_lambda_q2,
           da_lambda_k2, da_subln_g, w_mix_out, ln1_g, ln1_b, xa_wq, xa_wk, xa_wv, xa_wo, ln2_g, ln2_b, w_router,
           w_gate, w_up, w_down, ln3_g, ln3_b):
    batch, seq_len, d = x.shape
    assert w_in.shape[0] == DEPTH and seq_len % HG_CHUNK == 0 and seq_len % LANES_V7X == 0
    n = batch * seq_len
    cap = EC_FACTOR * seq_len // N_EXPERTS
    vec = lambda a: a.reshape(1, -1)
    bf = lambda a: a.astype(BF16)

    t = _tiles(seq_len)

    h0, q, v, gt, ff, fb, dq, dk, dv = _inproj(
        x.reshape(n, d), vec(emb_ln_g), vec(emb_ln_b), bf(w_in[0]), seq_len, t.inproj_rows)
    hg = _hgrn(q, v, gt, ff, fb, hg_lb_logits, hg_norm_g[0], batch, seq_len)
    lam_vecs = jnp.stack([da_lambda_q1[0], da_lambda_k1[0], da_lambda_q2[0], da_lambda_k2[0]])
    da = _dattn(dq, dk, dv, lam_vecs, da_subln_g[0], batch, seq_len, t.dattn_queries, t.dattn_keys)
    h1 = _mixproj(hg.reshape(n, -1), da.reshape(n, -1), h0, bf(w_mix_out[0]), ln1_g, ln1_b, t.mix_rows)

    kx, vx = _kvproj(mem, bf(xa_wk[0]), bf(xa_wv[0]))
    h2, aff_t = _xattn(h1, kx, vx, bf(xa_wq[0]), bf(xa_wo[0]), ln2_g, ln2_b, bf(w_router[0].T),
                       batch, seq_len, t.xattn_rows)

    slots = _route(aff_t, cap)
    xs, gates = _gather(slots, aff_t, h2, cap)
    y = _ffn(xs.reshape(N_EXPERTS, batch * cap, d), gates.reshape(N_EXPERTS, batch * cap, 1),
             w_gate[0], w_up[0], w_down[0], t.ffn_cols)
    out = _combine(slots, y.reshape(N_EXPERTS, batch, cap, d), h2, ln3_g, ln3_b, cap, t.combine_rows)
    return out.reshape(batch, seq_len, d)
```

```python
import functools
import math
from typing import NamedTuple

import jax
import jax.numpy as jnp
from jax import lax
from jax.experimental import pallas as pl
from jax.experimental.pallas import tpu as pltpu

F32 = jnp.float32
BF16 = jnp.bfloat16

HG_HEADS = 4
HG_DIM = 128
HG_CHUNK = 64
DA_HEADS = 4
DA_QK_DIM = 64
DA_V_DIM = 128
ROPE_DIM = DA_QK_DIM // 4
ROPE_THETA = 500000.0
XA_HEADS = 4
N_EXPERTS = 16
EC_FACTOR = 2
LN_EPS = 1e-5
RMS_EPS = 1e-6
DEPTH = 1
DEEPNORM_ALPHA = (2.0 * DEPTH) ** 0.25
LAMBDA_INIT = 0.8 - 0.6 * math.exp(-0.3 * 0)
LOG2_E = math.log2(math.e)

LANES_V7X = 128
BF16_SUBLANES_V7X = 16
VMEM_LIMIT_V7X = 56 * 1024 * 1024


class _Tiles(NamedTuple):
    inproj_rows: int
    dattn_queries: int
    dattn_keys: int
    mixattn_rows: int
    ffn_cols: int
    combine_rows: int


def _tiles(seq_len):
    fit = lambda rows: min(rows, seq_len)
    return _Tiles(inproj_rows=fit(1024), dattn_queries=fit(2048), dattn_keys=fit(512), mixattn_rows=fit(1024),
                  ffn_cols=512, combine_rows=fit(1024))


def _cparams(sem):
    return pltpu.CompilerParams(dimension_semantics=sem, vmem_limit_bytes=VMEM_LIMIT_V7X)


def _layer_norm(x, g, b):
    mu = jnp.mean(x, axis=-1, keepdims=True)
    xc = x - mu
    var = jnp.mean(xc * xc, axis=-1, keepdims=True)
    return xc * lax.rsqrt(var + LN_EPS) * g + b


def _silu(x):
    return x * jax.nn.sigmoid(x)


def _dot(a, b):
    return jnp.dot(a, b, preferred_element_type=F32)


def _dot_nt(a, b):
    return lax.dot_general(a, b, (((1,), (1,)), ((), ())), preferred_element_type=F32)


def _dot_tn(a, b):
    return lax.dot_general(a, b, (((0,), (0,)), ((), ())), preferred_element_type=F32)


def _inproj_kernel(x_ref, g_ref, b_ref, w_ref, c_ref, sa_ref, sb_ref,
                   h0_ref, q_ref, v_ref, gt_ref, ff_ref, fb_ref, dq_ref, dk_ref, dv_ref, hb_buf):
    width = q_ref.shape[1]
    parts = hb_buf.shape[0]
    pr = x_ref.shape[0] // parts
    rows = lambda r: slice(r * pr, (r + 1) * pr)
    widen = lambda ref, r: jnp.concatenate([ref[rows(r), :]] * (width // ref.shape[1]), axis=1)

    def normalize(r):
        h = _layer_norm(x_ref[rows(r), :], g_ref[...], b_ref[...])
        h0_ref[rows(r), :] = h
        hb_buf[r] = h.astype(BF16)

    def project(r, columns):
        proj = lambda c: _dot(hb_buf[r], w_ref[:, c * width:(c + 1) * width])
        rope_c, rope_sa, rope_sb = widen(c_ref, r), widen(sa_ref, r), widen(sb_ref, r)

        def rope(t):
            return (t * rope_c + pltpu.roll(t, width - ROPE_DIM // 2, 1) * rope_sa
                    + pltpu.roll(t, ROPE_DIM // 2, 1) * rope_sb)

        for c in columns:
            u = proj(c)
            if c == 0:
                q_ref[rows(r), :] = _silu(u).astype(BF16)
            elif c == 1:
                v_ref[rows(r), :] = u.astype(BF16)
            elif c == 2:
                gt_ref[rows(r), :] = _silu(u).astype(BF16)
            elif c == 3:
                ff_ref[rows(r), :] = u
            elif c == 4:
                fb_ref[rows(r), :] = u
            elif c == 5:
                dq_ref[rows(r), :] = (rope(u) * (DA_QK_DIM ** -0.5 * LOG2_E)).astype(BF16)
            elif c == 6:
                dk_ref[rows(r), :] = rope(u).astype(BF16)
            else:
                dv_ref[:, rows(r)] = u.T.astype(BF16)

    normalize(0)
    for r in range(parts):
        project(r, range(0, 2))
        if r + 1 < parts:
            normalize(r + 1)
        project(r, range(2, 8))


def _rope_lane_tables(seq_len, width):
    half = ROPE_DIM // 2
    inv = 1.0 / (ROPE_THETA ** (jnp.arange(0, ROPE_DIM, 2, dtype=F32) / ROPE_DIM))
    ang = jnp.arange(seq_len, dtype=F32)[:, None] * inv[None, :]
    cos, sin = jnp.cos(ang), jnp.sin(ang)
    pad1 = jnp.ones((seq_len, DA_QK_DIM - ROPE_DIM), F32)
    pad0 = jnp.zeros((seq_len, DA_QK_DIM - ROPE_DIM), F32)
    z = jnp.zeros((seq_len, half), F32)
    reps = width // DA_QK_DIM
    c = jnp.tile(jnp.concatenate([cos, cos, pad1], -1), (1, reps))
    sa = jnp.tile(jnp.concatenate([-sin, z, pad0], -1), (1, reps))
    sb = jnp.tile(jnp.concatenate([z, sin, pad0], -1), (1, reps))
    return c, sa, sb


def _inproj(x2, g, b, w_bf, seq_len, tm):
    n, d = x2.shape
    width = w_bf.shape[1] // 8
    c, sa, sb = _rope_lane_tables(seq_len, LANES_V7X)
    tpb = seq_len // tm
    parts = 4 if tm % 512 == 0 else 1
    row = lambda p, bi: (bi * tpb + p, 0)
    const = lambda p, bi: (0, 0)
    tab = lambda p, bi: (p, 0)
    wide = lambda dt: jax.ShapeDtypeStruct((n, width), dt)
    return pl.pallas_call(
        _inproj_kernel,
        grid=(tpb, n // seq_len),
        in_specs=[pl.BlockSpec((tm, d), row), pl.BlockSpec((1, d), const), pl.BlockSpec((1, d), const),
                  pl.BlockSpec(w_bf.shape, const),
                  pl.BlockSpec((tm, LANES_V7X), tab), pl.BlockSpec((tm, LANES_V7X), tab),
                  pl.BlockSpec((tm, LANES_V7X), tab)],
        out_specs=([pl.BlockSpec((tm, d), row)] + [pl.BlockSpec((tm, width), row)] * 7
                   + [pl.BlockSpec((width, tm), lambda p, bi: (0, bi * tpb + p))]),
        out_shape=[jax.ShapeDtypeStruct((n, d), F32), wide(BF16), wide(BF16), wide(BF16), wide(F32), wide(F32),
                   wide(BF16), wide(BF16), jax.ShapeDtypeStruct((width, n), BF16)],
        scratch_shapes=[pltpu.VMEM((parts, tm // parts, d), BF16)],
        compiler_params=_cparams(("parallel", "parallel")),
        name="inproj",
    )(x2, g, b, w_bf, c, sa, sb)


def _hgrn_kernel(q_ref, v_ref, gt_ref, ff_ref, fb_ref, lbl_ref, ng_ref, o_ref, st_ref, oacc_ref):
    seq_len, width = q_ref.shape
    heads = width // HG_DIM
    ch = HG_CHUNK
    blk = min(4 * ch, seq_len)
    cpb = blk // ch
    nblk = seq_len // blk
    sub_rows = 8
    tiles_per_chunk = ch // sub_rows
    lbl = lbl_ref[...]

    def lower_bound(d):
        l0, l1 = lbl[2 * d:2 * d + 1], lbl[2 * d + 1:2 * d + 2]
        m = jnp.maximum(l0, l1)
        e0, e1 = jnp.exp(l0 - m), jnp.exp(l1 - m)
        return e0 / (e0 + e1)

    lbs = (lower_bound(0), lower_bound(1))
    st_ref[...] = jnp.zeros_like(st_ref)
    oacc_ref[...] = jnp.zeros_like(oacc_ref)
    sub = lax.broadcasted_iota(jnp.int32, (blk // sub_rows, sub_rows, width), 1)
    ri = lax.broadcasted_iota(jnp.int32, (blk, blk), 0)
    ci = lax.broadcasted_iota(jnp.int32, (blk, blk), 1)
    same_chunk = (ri // ch) == (ci // ch)
    masks = (same_chunk & (ri >= ci), same_chunk & (ri <= ci))
    row_chunk = (lax.broadcasted_iota(jnp.int32, (blk, HG_DIM), 0) // ch).astype(BF16)
    in_chunk = [row_chunk == j for j in range(cpb)]
    sl = lambda hh: slice(hh * HG_DIM, (hh + 1) * HG_DIM)

    def cumprod(x, backward):
        x3 = x.reshape(blk // sub_rows, sub_rows, width)
        s = 1
        while s < sub_rows:
            if backward:
                x3 = x3 * jnp.where(sub < sub_rows - s, pltpu.roll(x3, sub_rows - s, 1), 1.0)
            else:
                x3 = x3 * jnp.where(sub >= s, pltpu.roll(x3, s, 1), 1.0)
            s *= 2
        edge = 0 if backward else sub_rows - 1
        out = [None] * (blk // sub_rows)
        for c in range(cpb):
            carry = None
            order = range(tiles_per_chunk - 1, -1, -1) if backward else range(tiles_per_chunk)
            for t in order:
                idx = c * tiles_per_chunk + t
                out[idx] = x3[idx] if carry is None else x3[idx] * carry
                carry = out[idx][edge:edge + 1]
        return jnp.concatenate(out, axis=0)

    class Item:
        pass

    def prep(d, i):
        it = Item()
        it.d = d
        bi = i if d == 0 else nblk - 1 - i
        it.rows = pl.ds(pl.multiple_of(bi * blk, blk), blk)
        uf = (ff_ref if d == 0 else fb_ref)[it.rows, :]
        f = lbs[d] + (1.0 - lbs[d]) * jax.nn.sigmoid(uf)
        g = cumprod(f, d == 1)
        edge = ch - 1 if d == 0 else 0
        it.decays = [g[j * ch + edge:j * ch + edge + 1] for j in range(cpb)]
        kin = (1.0 - f) / g
        it.k_out = (kin * jnp.concatenate([jnp.broadcast_to(dj, (ch, width)) for dj in it.decays], axis=0)
                    ).astype(BF16)
        it.q_in = (q_ref[it.rows, :].astype(F32) * g).astype(BF16)
        it.k_in = kin.astype(BF16)
        it.vv = v_ref[it.rows, :]
        return it

    def scores(it):
        it.scores = [_dot_nt(it.q_in[:, sl(hh)], it.k_in[:, sl(hh)]) for hh in range(heads)]
        it.incs = []
        for hh in range(heads):
            k_diag = jnp.concatenate([jnp.where(m, it.k_out[:, sl(hh)], jnp.zeros((), BF16)) for m in in_chunk],
                                     axis=1)
            it.incs.append(_dot_tn(it.vv[:, sl(hh)], k_diag))

    def intra(it):
        it.intra = [_dot(jnp.where(masks[it.d], it.scores[hh], 0.0).astype(BF16), it.vv[:, sl(hh)])
                    for hh in range(heads)]

    def scan(it):
        d = it.d
        st = [st_ref[d * heads + hh] for hh in range(heads)]
        it.inter = [[None] * cpb for _ in range(heads)]
        for step in range(cpb):
            j = step if d == 0 else cpb - 1 - step
            for hh in range(heads):
                it.inter[hh][j] = _dot_nt(it.q_in[j * ch:(j + 1) * ch, sl(hh)], st[hh].astype(BF16))
            for hh in range(heads):
                st[hh] = st[hh] * it.decays[j][:, sl(hh)] + it.incs[hh][:, j * HG_DIM:(j + 1) * HG_DIM]
        for hh in range(heads):
            st_ref[d * heads + hh] = st[hh]

    def write(it):
        for hh in range(heads):
            oacc_ref[it.rows, sl(hh)] += it.intra[hh] + jnp.concatenate(it.inter[hh], axis=0)

    def body(i, carry):
        fwd = prep(0, i)
        scores(fwd)
        bwd = prep(1, i)
        intra(fwd)
        scores(bwd)
        scan(fwd)
        intra(bwd)
        write(fwd)
        scan(bwd)
        write(bwd)
        return carry

    lax.fori_loop(0, nblk, body, 0, unroll=2 if nblk % 2 == 0 else 1)

    rb = min(256, seq_len)

    def finish(i, carry):
        rows = pl.ds(pl.multiple_of(i * rb, rb), rb)
        o = oacc_ref[rows, :]
        gate = gt_ref[rows, :].astype(F32)
        ng = ng_ref[...]
        for hh in range(heads):
            sl = slice(hh * HG_DIM, (hh + 1) * HG_DIM)
            oh = o[:, sl]
            ms = jnp.mean(oh * oh, axis=-1, keepdims=True)
            o_ref[rows, sl] = (oh * lax.rsqrt(ms + RMS_EPS) * ng[:, sl] * gate[:, sl]).astype(BF16)
        return carry

    lax.fori_loop(0, seq_len // rb, finish, 0)


def _hgrn(q, v, gt, ff, fb, lb_logits, norm_g, batch, seq_len):
    width = q.shape[1]
    hw = width
    sp = lambda a: a.reshape(batch, seq_len, width)
    blk = pl.BlockSpec((None, seq_len, hw), lambda b, h: (b, 0, h))
    return pl.pallas_call(
        _hgrn_kernel,
        grid=(batch, width // hw),
        in_specs=[blk, blk, blk, blk, blk,
                  pl.BlockSpec((4, hw), lambda b, h: (0, h)), pl.BlockSpec((1, hw), lambda b, h: (0, h))],
        out_specs=blk,
        out_shape=jax.ShapeDtypeStruct((batch, seq_len, width), BF16),
        scratch_shapes=[pltpu.VMEM((2 * (hw // HG_DIM), HG_DIM, HG_DIM), F32), pltpu.VMEM((seq_len, hw), F32)],
        compiler_params=_cparams(("parallel", "parallel")),
        name="hgrn",
    )(sp(q), sp(v), sp(gt), sp(ff), sp(fb), lb_logits.reshape(4, width), norm_g.reshape(1, width))


def _dattn_kernel(q_ref, k_ref, vt_ref, lam_ref, sg_ref, o_ref, s_buf, e_buf, acc_buf, *, kc):
    lv = lam_ref[...]
    lam = (jnp.exp(jnp.sum(lv[0:1] * lv[1:2], axis=-1, keepdims=True))
           - jnp.exp(jnp.sum(lv[2:3] * lv[3:4], axis=-1, keepdims=True)) + LAMBDA_INIT)
    q = q_ref[...]
    lane = lax.broadcasted_iota(jnp.int32, q.shape, 1)
    zero = jnp.zeros_like(q)
    qs = (jnp.where(lane < DA_QK_DIM, q, zero), jnp.where(lane >= DA_QK_DIM, q, zero))
    items = [(j, c) for j in range(k_ref.shape[0] // kc) for c in (0, 1)]
    nbuf = s_buf.shape[0]
    dv = vt_ref.shape[0]
    ones = jnp.ones((acc_buf.shape[1] - dv, kc), BF16)
    m, alpha = [None, None], {}

    def scores(i):
        j, c = items[i]
        s_buf[i % nbuf] = _dot_nt(k_ref[j * kc:(j + 1) * kc, :], qs[c])

    def exponentials(i):
        j, c = items[i]
        s = s_buf[i % nbuf]
        mj = jnp.max(s, axis=0, keepdims=True)
        if j == 0:
            m[c], alpha[i] = mj, None
        else:
            mn = jnp.maximum(m[c], mj)
            alpha[i] = jnp.exp2(m[c] - mn)
            m[c] = mn
        e_buf[i % nbuf] = jnp.exp2(s - m[c]).astype(BF16)

    def values(i):
        j, c = items[i]
        pv = _dot(jnp.concatenate([vt_ref[:, j * kc:(j + 1) * kc], ones], axis=0), e_buf[i % nbuf])
        acc_buf[c] = pv if alpha[i] is None else alpha[i] * acc_buf[c] + pv

    scores(0)
    scores(1)
    exponentials(0)
    for i in range(len(items)):
        if i + 2 < len(items):
            scores(i + 2)
        if i + 1 < len(items):
            exponentials(i + 1)
        values(i)
    a0, a1 = acc_buf[0], acc_buf[1]
    ot = a0[:dv] * (1.0 / a0[dv:dv + 1]) - a1[:dv] * (lam / a1[dv:dv + 1])
    ms = jnp.mean(ot * ot, axis=0, keepdims=True)
    y = ot * lax.rsqrt(ms + RMS_EPS) * (sg_ref[...] * (1.0 - LAMBDA_INIT))
    o_ref[...] = y.T.astype(BF16)


def _dattn(dq, dk, dvt, lam_vecs, subln_g, batch, seq_len, tq, kc):
    width = dq.shape[1]
    hw = DA_V_DIM
    sp = lambda a: a.reshape(batch, seq_len, width)
    kk = pl.BlockSpec((None, seq_len, hw), lambda b, h, i: (b, 0, h))
    qo = pl.BlockSpec((None, tq, hw), lambda b, h, i: (b, i, h))
    nbuf = 4
    return pl.pallas_call(
        functools.partial(_dattn_kernel, kc=kc),
        grid=(batch, width // hw, seq_len // tq),
        in_specs=[qo, kk, pl.BlockSpec((hw, seq_len), lambda b, h, i: (h, b)),
                  pl.BlockSpec(lam_vecs.shape, lambda b, h, i: (0, 0)),
                  pl.BlockSpec((hw, 1), lambda b, h, i: (0, 0))],
        out_specs=qo,
        out_shape=jax.ShapeDtypeStruct((batch, seq_len, width), BF16),
        scratch_shapes=[pltpu.VMEM((nbuf, kc, tq), F32), pltpu.VMEM((nbuf, kc, tq), BF16),
                        pltpu.VMEM((2, hw + BF16_SUBLANES_V7X, tq), F32)],
        compiler_params=_cparams(("parallel", "parallel", "parallel")),
        name="dattn",
    )(sp(dq), sp(dk), dvt, lam_vecs, subln_g.reshape(hw, 1))


def _kvproj_kernel(m_ref, wk_ref, wv_ref, k_ref, v_ref):
    mb = m_ref[...].astype(BF16)
    k_ref[...] = _dot(mb, wk_ref[...]).astype(BF16)
    v_ref[...] = _dot(mb, wv_ref[...]).astype(BF16)


def _kvproj(mem, wk_bf, wv_bf):
    batch, m, d = mem.shape
    blk = pl.BlockSpec((None, m, d), lambda b: (b, 0, 0))
    const = pl.BlockSpec((d, d), lambda b: (0, 0))
    return pl.pallas_call(
        _kvproj_kernel,
        grid=(batch,),
        in_specs=[blk, const, const],
        out_specs=[blk, blk],
        out_shape=[jax.ShapeDtypeStruct((batch, m, d), BF16)] * 2,
        compiler_params=_cparams(("parallel",)),
        name="kvproj",
    )(mem, wk_bf, wv_bf)


def _mixattn_kernel(hg_ref, da_ref, h0_ref, wm_ref, g1_ref, b1_ref, k_ref, v_ref, wq_ref, wo_ref, g2_ref, b2_ref,
                    wr_ref, h2_ref, aff_ref, mix_buf, h1_buf, *, parts):
    half = hg_ref.shape[1]
    pr = hg_ref.shape[0] // parts
    rows = lambda r: slice(r * pr, (r + 1) * pr)

    def project(r):
        mix_buf[r % 2] = _dot(hg_ref[rows(r), :], wm_ref[:half, :]) + _dot(da_ref[rows(r), :], wm_ref[half:, :])

    def normalize(r):
        h1_buf[rows(r), :] = _layer_norm(DEEPNORM_ALPHA * h0_ref[rows(r), :] + mix_buf[r % 2],
                                         g1_ref[...], b1_ref[...])

    project(0)
    for r in range(parts):
        if r + 1 < parts:
            project(r + 1)
        normalize(r)

    h1 = h1_buf[...]
    d = h1.shape[1]
    hd = d // XA_HEADS
    q = (_dot(h1.astype(BF16), wq_ref[...]) * (hd ** -0.5)).astype(BF16)
    outs = []
    for i in range(XA_HEADS):
        sl = slice(i * hd, (i + 1) * hd)
        s = _dot_nt(q[:, sl], k_ref[:, sl])
        e = jnp.exp(s - jnp.max(s, axis=-1, keepdims=True))
        inv_l = 1.0 / jnp.sum(e, axis=-1, keepdims=True)
        outs.append((_dot(e.astype(BF16), v_ref[:, sl]) * inv_l).astype(BF16))
    xa = _dot(jnp.concatenate(outs, axis=-1), wo_ref[...])
    h2 = _layer_norm(DEEPNORM_ALPHA * h1 + xa, g2_ref[...], b2_ref[...])
    h2_ref[...] = h2
    logits = _dot_nt(wr_ref[...], h2.astype(BF16))
    e = jnp.exp(logits - jnp.max(logits, axis=0, keepdims=True))
    aff_ref[...] = e / jnp.sum(e, axis=0, keepdims=True)


def _mixattn(hg, da, h0, wm_bf, g1, b1, kx, vx, wq_bf, wo_bf, g2, b2, wr_t_bf, batch, seq_len, tm):
    n, d = h0.shape
    half = hg.shape[1]
    m = kx.shape[1]
    ne = wr_t_bf.shape[0]
    tiles = seq_len // tm
    parts = 4 if tm % 64 == 0 else 1
    rows = lambda w: pl.BlockSpec((tm, w), lambda bi, i: (bi * tiles + i, 0))
    kv = pl.BlockSpec((None, m, d), lambda bi, i: (bi, 0, 0))
    const = lambda shape: pl.BlockSpec(shape, lambda bi, i: (0, 0))
    return pl.pallas_call(
        functools.partial(_mixattn_kernel, parts=parts),
        grid=(batch, tiles),
        in_specs=[rows(half), rows(half), rows(d), const((d, d)), const((1, d)), const((1, d)), kv, kv,
                  const((d, d)), const((d, d)), const((1, d)), const((1, d)), const((ne, d))],
        out_specs=[rows(d), pl.BlockSpec((None, ne, tm), lambda bi, i: (bi, 0, i))],
        out_shape=[jax.ShapeDtypeStruct((n, d), F32), jax.ShapeDtypeStruct((batch, ne, seq_len), F32)],
        scratch_shapes=[pltpu.VMEM((2, tm // parts, d), F32), pltpu.VMEM((tm, d), F32)],
        compiler_params=_cparams(("parallel", "parallel")),
        name="mixattn",
    )(hg, da, h0, wm_bf, g1, b1, kx, vx, wq_bf, wo_bf, g2, b2, wr_t_bf)


def _route_kernel(aff_ref, slot_ref, *, cap):
    aff = aff_ref[...]
    ne, seq_len = aff.shape
    nblk = seq_len // LANES_V7X

    def count(mask):
        return jnp.sum(jnp.where(mask, 1.0, 0.0), axis=-1, keepdims=True)

    bits = jnp.zeros((ne, 1), jnp.int32)
    for bit in range(30, -1, -1):
        cand = bits | (1 << bit)
        bits = jnp.where(count(aff >= pltpu.bitcast(cand, F32)) >= cap, cand, bits)
    thr = pltpu.bitcast(bits, F32)

    ri = lax.broadcasted_iota(jnp.int32, (LANES_V7X, LANES_V7X), 0)
    ci = lax.broadcasted_iota(jnp.int32, (LANES_V7X, LANES_V7X), 1)
    tri = jnp.where(ri <= ci, 1.0, 0.0).astype(BF16)

    def prefix_count(mask):
        m = jnp.where(mask, 1.0, 0.0)
        stack = jnp.concatenate([m[:, j * LANES_V7X:(j + 1) * LANES_V7X] for j in range(nblk)], axis=0)
        within = _dot(stack.astype(BF16), tri)
        pieces, carry = [], jnp.zeros((ne, 1), F32)
        for j in range(nblk):
            blk = within[j * ne:(j + 1) * ne]
            pieces.append(blk + carry)
            carry = carry + blk[:, LANES_V7X - 1:LANES_V7X]
        return jnp.concatenate(pieces, axis=1), m

    gt = aff > thr
    eq = aff == thr
    need = cap - count(gt)
    eq_incl, eq_f = prefix_count(eq)
    sel = gt | (eq & ((eq_incl - eq_f) < need))
    sel_incl, sel_f = prefix_count(sel)
    slot_ref[...] = jnp.where(sel, sel_incl - sel_f, -1.0).astype(jnp.int32)


def _route(aff_t, cap):
    batch, ne, seq_len = aff_t.shape
    blk = pl.BlockSpec((batch * ne, seq_len), lambda i: (0, 0))
    slots = pl.pallas_call(
        functools.partial(_route_kernel, cap=cap),
        grid=(1,),
        in_specs=[blk],
        out_specs=blk,
        out_shape=jax.ShapeDtypeStruct((batch * ne, seq_len), jnp.int32),
        compiler_params=_cparams(("arbitrary",)),
        name="route",
    )(aff_t.reshape(batch * ne, seq_len))
    return slots.reshape(batch, ne, seq_len)


def _gather_kernel(slot_ref, aff_ref, h_ref, xs_ref, gate_ref, *, cap):
    hb = h_ref[...].astype(BF16)
    ne, seq_len = slot_ref.shape
    rid = lax.broadcasted_iota(jnp.int32, (cap, seq_len), 0)
    for e in range(ne):
        pick = slot_ref[e:e + 1, :] == rid
        xs_ref[e] = _dot(jnp.where(pick, 1.0, 0.0).astype(BF16), hb).astype(BF16)
        gate_ref[e] = jnp.sum(jnp.where(pick, aff_ref[e:e + 1, :], 0.0), axis=-1, keepdims=True)


def _gather(slots, aff_t, h2, cap):
    batch, ne, seq_len = slots.shape
    d = h2.shape[1]
    blk = pl.BlockSpec((None, ne, seq_len), lambda b: (b, 0, 0))
    return pl.pallas_call(
        functools.partial(_gather_kernel, cap=cap),
        grid=(batch,),
        in_specs=[blk, blk, pl.BlockSpec((seq_len, d), lambda b: (b, 0))],
        out_specs=[pl.BlockSpec((ne, None, cap, d), lambda b: (0, b, 0, 0)),
                   pl.BlockSpec((ne, None, cap, 1), lambda b: (0, b, 0, 0))],
        out_shape=[jax.ShapeDtypeStruct((ne, batch, cap, d), BF16), jax.ShapeDtypeStruct((ne, batch, cap, 1), F32)],
        compiler_params=_cparams(("parallel",)),
        name="gather",
    )(slots, aff_t, h2)


def _ffn_kernel(xs_ref, gate_ref, wg_ref, wu_ref, wd_ref, y_ref, acc_ref, g_buf, u_buf, a_buf, *, n_chunks):
    c = pl.program_id(1)
    parts = g_buf.shape[0]
    pr = xs_ref.shape[0] // parts
    rows = lambda r: slice(r * pr, (r + 1) * pr)

    def chunk(first, final):
        wg, wu, wd = wg_ref[...].astype(BF16), wu_ref[...].astype(BF16), wd_ref[...].astype(BF16)

        def gate_up(r):
            xs = xs_ref[rows(r), :]
            g_buf[r] = _dot(xs, wg)
            u_buf[r] = _dot(xs, wu)

        def down(r):
            a_buf[r] = (_silu(g_buf[r]) * u_buf[r]).astype(BF16)
            part = _dot(a_buf[r], wd)
            total = part if first else acc_ref[rows(r), :] + part
            if final:
                y_ref[rows(r), :] = (total * gate_ref[rows(r), :]).astype(BF16)
            else:
                acc_ref[rows(r), :] = total

        gate_up(0)
        for r in range(parts):
            if r + 1 < parts:
                gate_up(r + 1)
            down(r)

    if n_chunks == 1:
        chunk(True, True)
    else:
        pl.when(c == 0)(lambda: chunk(True, False))
        if n_chunks > 2:
            pl.when((c > 0) & (c < n_chunks - 1))(lambda: chunk(False, False))
        pl.when(c == n_chunks - 1)(lambda: chunk(False, True))


def _ffn(xs, gates, w_gate, w_up, w_down, fc):
    ne, rows, d = xs.shape
    dff = w_gate.shape[2]
    parts = 2 if rows % 32 == 0 else 1
    return pl.pallas_call(
        functools.partial(_ffn_kernel, n_chunks=dff // fc),
        grid=(ne, dff // fc),
        in_specs=[pl.BlockSpec((None, rows, d), lambda e, c: (e, 0, 0)),
                  pl.BlockSpec((None, rows, 1), lambda e, c: (e, 0, 0)),
                  pl.BlockSpec((None, d, fc), lambda e, c: (e, 0, c)),
                  pl.BlockSpec((None, d, fc), lambda e, c: (e, 0, c)),
                  pl.BlockSpec((None, fc, d), lambda e, c: (e, c, 0))],
        out_specs=pl.BlockSpec((None, rows, d), lambda e, c: (e, 0, 0)),
        out_shape=jax.ShapeDtypeStruct((ne, rows, d), BF16),
        scratch_shapes=[pltpu.VMEM((rows, d), F32), pltpu.VMEM((parts, rows // parts, fc), F32),
                        pltpu.VMEM((parts, rows // parts, fc), F32), pltpu.VMEM((parts, rows // parts, fc), BF16)],
        compiler_params=_cparams(("parallel", "arbitrary")),
        name="ffn",
    )(xs, gates, w_gate, w_up, w_down)


def _combine_kernel(slot_ref, y_ref, h_ref, g_ref, b_ref, o_ref, *, cap):
    ne, tl = slot_ref.shape
    slot_t = slot_ref[...].astype(F32).T
    lane = lax.broadcasted_iota(jnp.int32, (tl, cap), 1).astype(F32)
    place = jnp.concatenate(
        [jnp.where(slot_t[:, e:e + 1] == lane, 1.0, 0.0).astype(BF16) for e in range(ne)], axis=1)
    moe = _dot(place, y_ref[...].reshape(ne * cap, y_ref.shape[-1]))
    o_ref[...] = _layer_norm(DEEPNORM_ALPHA * h_ref[...] + moe, g_ref[...], b_ref[...])


def _combine(slots, y, h2, g, b, cap, tl):
    batch, ne, seq_len = slots.shape
    d = h2.shape[1]
    tiles = seq_len // tl
    row = pl.BlockSpec((tl, d), lambda bi, i: (bi * tiles + i, 0))
    const = pl.BlockSpec((1, d), lambda bi, i: (0, 0))
    return pl.pallas_call(
        functools.partial(_combine_kernel, cap=cap),
        grid=(batch, tiles),
        in_specs=[pl.BlockSpec((None, ne, tl), lambda bi, i: (bi, 0, i)),
                  pl.BlockSpec((ne, None, cap, d), lambda bi, i: (0, bi, 0, 0)),
                  row, const, const],
        out_specs=row,
        out_shape=jax.ShapeDtypeStruct((batch * seq_len, d), F32),
        compiler_params=_cparams(("parallel", "parallel")),
        name="combine",
    )(slots, y, h2, g, b)


def kernel(x, mem, emb_ln_g, emb_ln_b, w_in, hg_lb_logits, hg_norm_g, da_lambda_q1, da_lambda_k1, da_lambda_q2,
           da_lambda_k2, da_subln_g, w_mix_out, ln1_g, ln1_b, xa_wq, xa_wk, xa_wv, xa_wo, ln2_g, ln2_b, w_router,
           w_gate, w_up, w_down, ln3_g, ln3_b):
    batch, seq_len, d = x.shape
    assert w_in.shape[0] == DEPTH and seq_len % HG_CHUNK == 0 and seq_len % LANES_V7X == 0
    n = batch * seq_len
    cap = EC_FACTOR * seq_len // N_EXPERTS
    vec = lambda a: a.reshape(1, -1)
    bf = lambda a: a.astype(BF16)

    t = _tiles(seq_len)

    h0, q, v, gt, ff, fb, dq, dk, dv = _inproj(
        x.reshape(n, d), vec(emb_ln_g), vec(emb_ln_b), bf(w_in[0]), seq_len, t.inproj_rows)
    hg = _hgrn(q, v, gt, ff, fb, hg_lb_logits, hg_norm_g[0], batch, seq_len)
    lam_vecs = jnp.stack([da_lambda_q1[0], da_lambda_k1[0], da_lambda_q2[0], da_lambda_k2[0]])
    da = _dattn(dq, dk, dv, lam_vecs, da_subln_g[0], batch, seq_len, t.dattn_queries, t.dattn_keys)
    kx, vx = _kvproj(mem, bf(xa_wk[0]), bf(xa_wv[0]))
    h2, aff_t = _mixattn(hg.reshape(n, -1), da.reshape(n, -1), h0, bf(w_mix_out[0]), ln1_g, ln1_b, kx, vx,
                         bf(xa_wq[0]), bf(xa_wo[0]), ln2_g, ln2_b, bf(w_router[0].T), batch, seq_len, t.mixattn_rows)

    slots = _route(aff_t, cap)
    xs, gates = _gather(slots, aff_t, h2, cap)
    y = _ffn(xs.reshape(N_EXPERTS, batch * cap, d), gates.reshape(N_EXPERTS, batch * cap, 1),
             w_gate[0], w_up[0], w_down[0], t.ffn_cols)
    out = _combine(slots, y.reshape(N_EXPERTS, batch, cap, d), h2, ln3_g, ln3_b, cap, t.combine_rows)
    return out.reshape(batch, seq_len, d)
```

```python
import functools
import math
from typing import NamedTuple

import jax
import jax.numpy as jnp
from jax import lax
from jax.experimental import pallas as pl
from jax.experimental.pallas import tpu as pltpu

F32 = jnp.float32
BF16 = jnp.bfloat16

HG_HEADS = 4
HG_DIM = 128
HG_CHUNK = 64
DA_HEADS = 4
DA_QK_DIM = 64
DA_V_DIM = 128
ROPE_DIM = DA_QK_DIM // 4
ROPE_THETA = 500000.0
XA_HEADS = 4
N_EXPERTS = 16
EC_FACTOR = 2
LN_EPS = 1e-5
RMS_EPS = 1e-6
DEPTH = 1
DEEPNORM_ALPHA = (2.0 * DEPTH) ** 0.25
LAMBDA_INIT = 0.8 - 0.6 * math.exp(-0.3 * 0)
LOG2_E = math.log2(math.e)

LANES_V7X = 128
BF16_SUBLANES_V7X = 16
VMEM_LIMIT_V7X = 56 * 1024 * 1024


class _Tiles(NamedTuple):
    inproj_rows: int
    dattn_queries: int
    dattn_keys: int
    mixattn_rows: int
    ffn_cols: int
    combine_rows: int


def _tiles(seq_len):
    fit = lambda rows: min(rows, seq_len)
    return _Tiles(inproj_rows=fit(1024), dattn_queries=fit(2048), dattn_keys=fit(512), mixattn_rows=fit(1024),
                  ffn_cols=512, combine_rows=fit(1024))


def _cparams(sem):
    return pltpu.CompilerParams(dimension_semantics=sem, vmem_limit_bytes=VMEM_LIMIT_V7X)


def _layer_norm(x, g, b):
    mu = jnp.mean(x, axis=-1, keepdims=True)
    xc = x - mu
    var = jnp.mean(xc * xc, axis=-1, keepdims=True)
    return xc * lax.rsqrt(var + LN_EPS) * g + b


def _silu(x):
    return x * jax.nn.sigmoid(x)


def _dot(a, b):
    return jnp.dot(a, b, preferred_element_type=F32)


def _dot_nt(a, b):
    return lax.dot_general(a, b, (((1,), (1,)), ((), ())), preferred_element_type=F32)


def _dot_tn(a, b):
    return lax.dot_general(a, b, (((0,), (0,)), ((), ())), preferred_element_type=F32)


def _inproj_kernel(x_ref, g_ref, b_ref, w_ref, c_ref, sa_ref, sb_ref,
                   h0_ref, q_ref, v_ref, gt_ref, ff_ref, fb_ref, dq_ref, dk_ref, dv_ref, hb_buf):
    width = q_ref.shape[1]
    parts = hb_buf.shape[0]
    pr = x_ref.shape[0] // parts
    rows = lambda r: slice(r * pr, (r + 1) * pr)
    widen = lambda ref, r: jnp.concatenate([ref[rows(r), :]] * (width // ref.shape[1]), axis=1)

    def normalize(r):
        h = _layer_norm(x_ref[rows(r), :], g_ref[...], b_ref[...])
        h0_ref[rows(r), :] = h
        hb_buf[r] = h.astype(BF16)

    def project(r, columns):
        proj = lambda c: _dot(hb_buf[r], w_ref[:, c * width:(c + 1) * width])
        rope_c, rope_sa, rope_sb = widen(c_ref, r), widen(sa_ref, r), widen(sb_ref, r)

        def rope(t):
            return (t * rope_c + pltpu.roll(t, width - ROPE_DIM // 2, 1) * rope_sa
                    + pltpu.roll(t, ROPE_DIM // 2, 1) * rope_sb)

        for c in columns:
            u = proj(c)
            if c == 0:
                q_ref[rows(r), :] = _silu(u).astype(BF16)
            elif c == 1:
                v_ref[rows(r), :] = u.astype(BF16)
            elif c == 2:
                gt_ref[rows(r), :] = _silu(u).astype(BF16)
            elif c == 3:
                ff_ref[rows(r), :] = u
            elif c == 4:
                fb_ref[rows(r), :] = u
            elif c == 5:
                dq_ref[rows(r), :] = (rope(u) * (DA_QK_DIM ** -0.5 * LOG2_E)).astype(BF16)
            elif c == 6:
                dk_ref[rows(r), :] = rope(u).astype(BF16)
            else:
                dv_ref[:, rows(r)] = u.T.astype(BF16)

    normalize(0)
    for r in range(parts):
        project(r, range(0, 2))
        if r + 1 < parts:
            normalize(r + 1)
        project(r, range(2, 8))


def _rope_lane_tables(seq_len, width):
    half = ROPE_DIM // 2
    inv = 1.0 / (ROPE_THETA ** (jnp.arange(0, ROPE_DIM, 2, dtype=F32) / ROPE_DIM))
    ang = jnp.arange(seq_len, dtype=F32)[:, None] * inv[None, :]
    cos, sin = jnp.cos(ang), jnp.sin(ang)
    pad1 = jnp.ones((seq_len, DA_QK_DIM - ROPE_DIM), F32)
    pad0 = jnp.zeros((seq_len, DA_QK_DIM - ROPE_DIM), F32)
    z = jnp.zeros((seq_len, half), F32)
    reps = width // DA_QK_DIM
    c = jnp.tile(jnp.concatenate([cos, cos, pad1], -1), (1, reps))
    sa = jnp.tile(jnp.concatenate([-sin, z, pad0], -1), (1, reps))
    sb = jnp.tile(jnp.concatenate([z, sin, pad0], -1), (1, reps))
    return c, sa, sb


def _inproj(x2, g, b, w_bf, seq_len, tm):
    n, d = x2.shape
    width = w_bf.shape[1] // 8
    c, sa, sb = _rope_lane_tables(seq_len, LANES_V7X)
    tpb = seq_len // tm
    parts = 4 if tm % 512 == 0 else 1
    row = lambda p, bi: (bi * tpb + p, 0)
    const = lambda p, bi: (0, 0)
    tab = lambda p, bi: (p, 0)
    wide = lambda dt: jax.ShapeDtypeStruct((n, width), dt)
    return pl.pallas_call(
        _inproj_kernel,
        grid=(tpb, n // seq_len),
        in_specs=[pl.BlockSpec((tm, d), row), pl.BlockSpec((1, d), const), pl.BlockSpec((1, d), const),
                  pl.BlockSpec(w_bf.shape, const),
                  pl.BlockSpec((tm, LANES_V7X), tab), pl.BlockSpec((tm, LANES_V7X), tab),
                  pl.BlockSpec((tm, LANES_V7X), tab)],
        out_specs=([pl.BlockSpec((tm, d), row)] + [pl.BlockSpec((tm, width), row)] * 7
                   + [pl.BlockSpec((width, tm), lambda p, bi: (0, bi * tpb + p))]),
        out_shape=[jax.ShapeDtypeStruct((n, d), F32), wide(BF16), wide(BF16), wide(BF16), wide(F32), wide(F32),
                   wide(BF16), wide(BF16), jax.ShapeDtypeStruct((width, n), BF16)],
        scratch_shapes=[pltpu.VMEM((parts, tm // parts, d), BF16)],
        compiler_params=_cparams(("parallel", "parallel")),
        name="inproj",
    )(x2, g, b, w_bf, c, sa, sb)


def _hgrn_kernel(q_ref, v_ref, gt_ref, ff_ref, fb_ref, lbl_ref, ng_ref, o_ref, st_ref, oacc_ref):
    seq_len, width = q_ref.shape
    heads = width // HG_DIM
    ch = HG_CHUNK
    blk = min(2 * ch, seq_len)
    cpb = blk // ch
    nblk = seq_len // blk
    sub_rows = 8
    tiles_per_chunk = ch // sub_rows
    lbl = lbl_ref[...]

    def lower_bound(d):
        l0, l1 = lbl[2 * d:2 * d + 1], lbl[2 * d + 1:2 * d + 2]
        m = jnp.maximum(l0, l1)
        e0, e1 = jnp.exp(l0 - m), jnp.exp(l1 - m)
        return e0 / (e0 + e1)

    lbs = (lower_bound(0), lower_bound(1))
    st_ref[...] = jnp.zeros_like(st_ref)
    oacc_ref[...] = jnp.zeros_like(oacc_ref)
    sub = lax.broadcasted_iota(jnp.int32, (blk // sub_rows, sub_rows, width), 1)
    ri = lax.broadcasted_iota(jnp.int32, (blk, blk), 0)
    ci = lax.broadcasted_iota(jnp.int32, (blk, blk), 1)
    same_chunk = (ri // ch) == (ci // ch)
    masks = (same_chunk & (ri >= ci), same_chunk & (ri <= ci))
    row_chunk = (lax.broadcasted_iota(jnp.int32, (blk, HG_DIM), 0) // ch).astype(BF16)
    in_chunk = [row_chunk == j for j in range(cpb)]
    sl = lambda hh: slice(hh * HG_DIM, (hh + 1) * HG_DIM)

    def cumprod(x, backward):
        x3 = x.reshape(blk // sub_rows, sub_rows, width)
        s = 1
        while s < sub_rows:
            if backward:
                x3 = x3 * jnp.where(sub < sub_rows - s, pltpu.roll(x3, sub_rows - s, 1), 1.0)
            else:
                x3 = x3 * jnp.where(sub >= s, pltpu.roll(x3, s, 1), 1.0)
            s *= 2
        edge = 0 if backward else sub_rows - 1
        out = [None] * (blk // sub_rows)
        for c in range(cpb):
            carry = None
            order = range(tiles_per_chunk - 1, -1, -1) if backward else range(tiles_per_chunk)
            for t in order:
                idx = c * tiles_per_chunk + t
                out[idx] = x3[idx] if carry is None else x3[idx] * carry
                carry = out[idx][edge:edge + 1]
        return jnp.concatenate(out, axis=0)

    class Item:
        pass

    def prep(d, i):
        it = Item()
        it.d = d
        bi = i if d == 0 else nblk - 1 - i
        it.rows = pl.ds(pl.multiple_of(bi * blk, blk), blk)
        uf = (ff_ref if d == 0 else fb_ref)[it.rows, :]
        f = lbs[d] + (1.0 - lbs[d]) * jax.nn.sigmoid(uf)
        g = cumprod(f, d == 1)
        edge = ch - 1 if d == 0 else 0
        it.decays = [g[j * ch + edge:j * ch + edge + 1] for j in range(cpb)]
        kin = (1.0 - f) / g
        it.k_out = (kin * jnp.concatenate([jnp.broadcast_to(dj, (ch, width)) for dj in it.decays], axis=0)
                    ).astype(BF16)
        it.q_in = (q_ref[it.rows, :].astype(F32) * g).astype(BF16)
        it.k_in = kin.astype(BF16)
        it.vv = v_ref[it.rows, :]
        return it

    def scores(it):
        it.scores = [_dot_nt(it.q_in[:, sl(hh)], it.k_in[:, sl(hh)]) for hh in range(heads)]
        it.incs = []
        for hh in range(heads):
            k_diag = jnp.concatenate([jnp.where(m, it.k_out[:, sl(hh)], jnp.zeros((), BF16)) for m in in_chunk],
                                     axis=1)
            it.incs.append(_dot_tn(it.vv[:, sl(hh)], k_diag))

    def intra(it):
        it.intra = [_dot(jnp.where(masks[it.d], it.scores[hh], 0.0).astype(BF16), it.vv[:, sl(hh)])
                    for hh in range(heads)]

    def scan(it):
        d = it.d
        st = [st_ref[d * heads + hh] for hh in range(heads)]
        it.inter = [[None] * cpb for _ in range(heads)]
        for step in range(cpb):
            j = step if d == 0 else cpb - 1 - step
            for hh in range(heads):
                it.inter[hh][j] = _dot_nt(it.q_in[j * ch:(j + 1) * ch, sl(hh)], st[hh].astype(BF16))
            for hh in range(heads):
                st[hh] = st[hh] * it.decays[j][:, sl(hh)] + it.incs[hh][:, j * HG_DIM:(j + 1) * HG_DIM]
        for hh in range(heads):
            st_ref[d * heads + hh] = st[hh]

    def write(it):
        for hh in range(heads):
            oacc_ref[it.rows, sl(hh)] += it.intra[hh] + jnp.concatenate(it.inter[hh], axis=0)

    def body(i, carry):
        fwd = prep(0, i)
        scores(fwd)
        bwd = prep(1, i)
        intra(fwd)
        scores(bwd)
        scan(fwd)
        intra(bwd)
        write(fwd)
        scan(bwd)
        write(bwd)
        return carry

    lax.fori_loop(0, nblk, body, 0, unroll=8 if nblk % 8 == 0 else 1)

    rb = min(256, seq_len)

    def finish(i, carry):
        rows = pl.ds(pl.multiple_of(i * rb, rb), rb)
        o = oacc_ref[rows, :]
        gate = gt_ref[rows, :].astype(F32)
        ng = ng_ref[...]
        for hh in range(heads):
            sl = slice(hh * HG_DIM, (hh + 1) * HG_DIM)
            oh = o[:, sl]
            ms = jnp.mean(oh * oh, axis=-1, keepdims=True)
            o_ref[rows, sl] = (oh * lax.rsqrt(ms + RMS_EPS) * ng[:, sl] * gate[:, sl]).astype(BF16)
        return carry

    lax.fori_loop(0, seq_len // rb, finish, 0)


def _hgrn(q, v, gt, ff, fb, lb_logits, norm_g, batch, seq_len):
    width = q.shape[1]
    hw = width
    sp = lambda a: a.reshape(batch, seq_len, width)
    blk = pl.BlockSpec((None, seq_len, hw), lambda b, h: (b, 0, h))
    return pl.pallas_call(
        _hgrn_kernel,
        grid=(batch, width // hw),
        in_specs=[blk, blk, blk, blk, blk,
                  pl.BlockSpec((4, hw), lambda b, h: (0, h)), pl.BlockSpec((1, hw), lambda b, h: (0, h))],
        out_specs=blk,
        out_shape=jax.ShapeDtypeStruct((batch, seq_len, width), BF16),
        scratch_shapes=[pltpu.VMEM((2 * (hw // HG_DIM), HG_DIM, HG_DIM), F32), pltpu.VMEM((seq_len, hw), F32)],
        compiler_params=_cparams(("parallel", "parallel")),
        name="hgrn",
    )(sp(q), sp(v), sp(gt), sp(ff), sp(fb), lb_logits.reshape(4, width), norm_g.reshape(1, width))


def _dattn_kernel(q_ref, k_ref, vt_ref, lam_ref, sg_ref, o_ref, s_buf, e_buf, acc_buf, *, kc):
    lv = lam_ref[...]
    lam = (jnp.exp(jnp.sum(lv[0:1] * lv[1:2], axis=-1, keepdims=True))
           - jnp.exp(jnp.sum(lv[2:3] * lv[3:4], axis=-1, keepdims=True)) + LAMBDA_INIT)
    q = q_ref[...]
    lane = lax.broadcasted_iota(jnp.int32, q.shape, 1)
    zero = jnp.zeros_like(q)
    qs = (jnp.where(lane < DA_QK_DIM, q, zero), jnp.where(lane >= DA_QK_DIM, q, zero))
    items = [(j, c) for j in range(k_ref.shape[0] // kc) for c in (0, 1)]
    nbuf = s_buf.shape[0]
    dv = vt_ref.shape[0]
    ones = jnp.ones((acc_buf.shape[1] - dv, kc), BF16)
    m, alpha = [None, None], {}

    def scores(i):
        j, c = items[i]
        s_buf[i % nbuf] = _dot_nt(k_ref[j * kc:(j + 1) * kc, :], qs[c])

    def exponentials(i):
        j, c = items[i]
        s = s_buf[i % nbuf]
        mj = jnp.max(s, axis=0, keepdims=True)
        if j == 0:
            m[c], alpha[i] = mj, None
        else:
            mn = jnp.maximum(m[c], mj)
            alpha[i] = jnp.exp2(m[c] - mn)
            m[c] = mn
        e_buf[i % nbuf] = jnp.exp2(s - m[c]).astype(BF16)

    def values(i):
        j, c = items[i]
        pv = _dot(jnp.concatenate([vt_ref[:, j * kc:(j + 1) * kc], ones], axis=0), e_buf[i % nbuf])
        acc_buf[c] = pv if alpha[i] is None else alpha[i] * acc_buf[c] + pv

    scores(0)
    scores(1)
    exponentials(0)
    for i in range(len(items)):
        if i + 2 < len(items):
            scores(i + 2)
        if i + 1 < len(items):
            exponentials(i + 1)
        values(i)
    a0, a1 = acc_buf[0], acc_buf[1]
    ot = a0[:dv] * (1.0 / a0[dv:dv + 1]) - a1[:dv] * (lam / a1[dv:dv + 1])
    ms = jnp.mean(ot * ot, axis=0, keepdims=True)
    y = ot * lax.rsqrt(ms + RMS_EPS) * (sg_ref[...] * (1.0 - LAMBDA_INIT))
    o_ref[...] = y.T.astype(BF16)


def _dattn(dq, dk, dvt, lam_vecs, subln_g, batch, seq_len, tq, kc):
    width = dq.shape[1]
    hw = DA_V_DIM
    sp = lambda a: a.reshape(batch, seq_len, width)
    kk = pl.BlockSpec((None, seq_len, hw), lambda b, h, i: (b, 0, h))
    qo = pl.BlockSpec((None, tq, hw), lambda b, h, i: (b, i, h))
    nbuf = 4
    return pl.pallas_call(
        functools.partial(_dattn_kernel, kc=kc),
        grid=(batch, width // hw, seq_len // tq),
        in_specs=[qo, kk, pl.BlockSpec((hw, seq_len), lambda b, h, i: (h, b)),
                  pl.BlockSpec(lam_vecs.shape, lambda b, h, i: (0, 0)),
                  pl.BlockSpec((hw, 1), lambda b, h, i: (0, 0))],
        out_specs=qo,
        out_shape=jax.ShapeDtypeStruct((batch, seq_len, width), BF16),
        scratch_shapes=[pltpu.VMEM((nbuf, kc, tq), F32), pltpu.VMEM((nbuf, kc, tq), BF16),
                        pltpu.VMEM((2, hw + BF16_SUBLANES_V7X, tq), F32)],
        compiler_params=_cparams(("parallel", "parallel", "parallel")),
        name="dattn",
    )(sp(dq), sp(dk), dvt, lam_vecs, subln_g.reshape(hw, 1))


def _kvproj_kernel(m_ref, wk_ref, wv_ref, k_ref, v_ref):
    mb = m_ref[...].astype(BF16)
    k_ref[...] = _dot(mb, wk_ref[...]).astype(BF16)
    v_ref[...] = _dot(mb, wv_ref[...]).astype(BF16)


def _kvproj(mem, wk_bf, wv_bf):
    batch, m, d = mem.shape
    blk = pl.BlockSpec((None, m, d), lambda b: (b, 0, 0))
    const = pl.BlockSpec((d, d), lambda b: (0, 0))
    return pl.pallas_call(
        _kvproj_kernel,
        grid=(batch,),
        in_specs=[blk, const, const],
        out_specs=[blk, blk],
        out_shape=[jax.ShapeDtypeStruct((batch, m, d), BF16)] * 2,
        compiler_params=_cparams(("parallel",)),
        name="kvproj",
    )(mem, wk_bf, wv_bf)


def _mixattn_kernel(hg_ref, da_ref, h0_ref, wm_ref, g1_ref, b1_ref, k_ref, v_ref, wq_ref, wo_ref, g2_ref, b2_ref,
                    wr_ref, h2_ref, aff_ref, mix_buf, h1_buf, *, parts):
    half = hg_ref.shape[1]
    pr = hg_ref.shape[0] // parts
    rows = lambda r: slice(r * pr, (r + 1) * pr)

    def project(r):
        mix_buf[r % 2] = _dot(hg_ref[rows(r), :], wm_ref[:half, :]) + _dot(da_ref[rows(r), :], wm_ref[half:, :])

    def normalize(r):
        h1_buf[rows(r), :] = _layer_norm(DEEPNORM_ALPHA * h0_ref[rows(r), :] + mix_buf[r % 2],
                                         g1_ref[...], b1_ref[...])

    project(0)
    for r in range(parts):
        if r + 1 < parts:
            project(r + 1)
        normalize(r)

    h1 = h1_buf[...]
    d = h1.shape[1]
    hd = d // XA_HEADS
    q = (_dot(h1.astype(BF16), wq_ref[...]) * (hd ** -0.5)).astype(BF16)
    outs = []
    for i in range(XA_HEADS):
        sl = slice(i * hd, (i + 1) * hd)
        s = _dot_nt(q[:, sl], k_ref[:, sl])
        e = jnp.exp(s - jnp.max(s, axis=-1, keepdims=True))
        inv_l = 1.0 / jnp.sum(e, axis=-1, keepdims=True)
        outs.append((_dot(e.astype(BF16), v_ref[:, sl]) * inv_l).astype(BF16))
    xa = _dot(jnp.concatenate(outs, axis=-1), wo_ref[...])
    h2 = _layer_norm(DEEPNORM_ALPHA * h1 + xa, g2_ref[...], b2_ref[...])
    h2_ref[...] = h2
    logits = _dot_nt(wr_ref[...], h2.astype(BF16))
    e = jnp.exp(logits - jnp.max(logits, axis=0, keepdims=True))
    aff_ref[...] = e / jnp.sum(e, axis=0, keepdims=True)


def _mixattn(hg, da, h0, wm_bf, g1, b1, kx, vx, wq_bf, wo_bf, g2, b2, wr_t_bf, batch, seq_len, tm):
    n, d = h0.shape
    half = hg.shape[1]
    m = kx.shape[1]
    ne = wr_t_bf.shape[0]
    tiles = seq_len // tm
    parts = 4 if tm % 64 == 0 else 1
    rows = lambda w: pl.BlockSpec((tm, w), lambda bi, i: (bi * tiles + i, 0))
    kv = pl.BlockSpec((None, m, d), lambda bi, i: (bi, 0, 0))
    const = lambda shape: pl.BlockSpec(shape, lambda bi, i: (0, 0))
    return pl.pallas_call(
        functools.partial(_mixattn_kernel, parts=parts),
        grid=(batch, tiles),
        in_specs=[rows(half), rows(half), rows(d), const((d, d)), const((1, d)), const((1, d)), kv, kv,
                  const((d, d)), const((d, d)), const((1, d)), const((1, d)), const((ne, d))],
        out_specs=[rows(d), pl.BlockSpec((None, ne, tm), lambda bi, i: (bi, 0, i))],
        out_shape=[jax.ShapeDtypeStruct((n, d), F32), jax.ShapeDtypeStruct((batch, ne, seq_len), F32)],
        scratch_shapes=[pltpu.VMEM((2, tm // parts, d), F32), pltpu.VMEM((tm, d), F32)],
        compiler_params=_cparams(("parallel", "parallel")),
        name="mixattn",
    )(hg, da, h0, wm_bf, g1, b1, kx, vx, wq_bf, wo_bf, g2, b2, wr_t_bf)


def _route_kernel(aff_ref, slot_ref, *, cap):
    aff = aff_ref[...]
    ne, seq_len = aff.shape
    nblk = seq_len // LANES_V7X

    def count(mask):
        return jnp.sum(jnp.where(mask, 1.0, 0.0), axis=-1, keepdims=True)

    bits = jnp.zeros((ne, 1), jnp.int32)
    for bit in range(30, -1, -1):
        cand = bits | (1 << bit)
        bits = jnp.where(count(aff >= pltpu.bitcast(cand, F32)) >= cap, cand, bits)
    thr = pltpu.bitcast(bits, F32)

    ri = lax.broadcasted_iota(jnp.int32, (LANES_V7X, LANES_V7X), 0)
    ci = lax.broadcasted_iota(jnp.int32, (LANES_V7X, LANES_V7X), 1)
    tri = jnp.where(ri <= ci, 1.0, 0.0).astype(BF16)

    def prefix_count(mask):
        m = jnp.where(mask, 1.0, 0.0)
        stack = jnp.concatenate([m[:, j * LANES_V7X:(j + 1) * LANES_V7X] for j in range(nblk)], axis=0)
        within = _dot(stack.astype(BF16), tri)
        pieces, carry = [], jnp.zeros((ne, 1), F32)
        for j in range(nblk):
            blk = within[j * ne:(j + 1) * ne]
            pieces.append(blk + carry)
            carry = carry + blk[:, LANES_V7X - 1:LANES_V7X]
        return jnp.concatenate(pieces, axis=1), m

    gt = aff > thr
    eq = aff == thr
    need = cap - count(gt)
    eq_incl, eq_f = prefix_count(eq)
    sel = gt | (eq & ((eq_incl - eq_f) < need))
    sel_incl, sel_f = prefix_count(sel)
    slot_ref[...] = jnp.where(sel, sel_incl - sel_f, -1.0).astype(jnp.int32)


def _route(aff_t, cap):
    batch, ne, seq_len = aff_t.shape
    blk = pl.BlockSpec((batch * ne, seq_len), lambda i: (0, 0))
    slots = pl.pallas_call(
        functools.partial(_route_kernel, cap=cap),
        grid=(1,),
        in_specs=[blk],
        out_specs=blk,
        out_shape=jax.ShapeDtypeStruct((batch * ne, seq_len), jnp.int32),
        compiler_params=_cparams(("arbitrary",)),
        name="route",
    )(aff_t.reshape(batch * ne, seq_len))
    return slots.reshape(batch, ne, seq_len)


def _gather_kernel(slot_ref, aff_ref, h_ref, xs_ref, gate_ref, *, cap):
    hb = h_ref[...].astype(BF16)
    ne, seq_len = slot_ref.shape
    rid = lax.broadcasted_iota(jnp.int32, (cap, seq_len), 0)
    for e in range(ne):
        pick = slot_ref[e:e + 1, :] == rid
        xs_ref[e] = _dot(jnp.where(pick, 1.0, 0.0).astype(BF16), hb).astype(BF16)
        gate_ref[e] = jnp.sum(jnp.where(pick, aff_ref[e:e + 1, :], 0.0), axis=-1, keepdims=True)


def _gather(slots, aff_t, h2, cap):
    batch, ne, seq_len = slots.shape
    d = h2.shape[1]
    blk = pl.BlockSpec((None, ne, seq_len), lambda b: (b, 0, 0))
    return pl.pallas_call(
        functools.partial(_gather_kernel, cap=cap),
        grid=(batch,),
        in_specs=[blk, blk, pl.BlockSpec((seq_len, d), lambda b: (b, 0))],
        out_specs=[pl.BlockSpec((ne, None, cap, d), lambda b: (0, b, 0, 0)),
                   pl.BlockSpec((ne, None, cap, 1), lambda b: (0, b, 0, 0))],
        out_shape=[jax.ShapeDtypeStruct((ne, batch, cap, d), BF16), jax.ShapeDtypeStruct((ne, batch, cap, 1), F32)],
        compiler_params=_cparams(("parallel",)),
        name="gather",
    )(slots, aff_t, h2)


def _ffn_kernel(xs_ref, gate_ref, wg_ref, wu_ref, wd_ref, y_ref, acc_ref, g_buf, u_buf, a_buf, *, n_chunks):
    c = pl.program_id(1)
    parts = g_buf.shape[0]
    pr = xs_ref.shape[0] // parts
    rows = lambda r: slice(r * pr, (r + 1) * pr)

    def chunk(first, final):
        wg, wu, wd = wg_ref[...].astype(BF16), wu_ref[...].astype(BF16), wd_ref[...].astype(BF16)

        def gate_up(r):
            xs = xs_ref[rows(r), :]
            g_buf[r] = _dot(xs, wg)
            u_buf[r] = _dot(xs, wu)

        def down(r):
            a_buf[r] = (_silu(g_buf[r]) * u_buf[r]).astype(BF16)
            part = _dot(a_buf[r], wd)
            total = part if first else acc_ref[rows(r), :] + part
            if final:
                y_ref[rows(r), :] = (total * gate_ref[rows(r), :]).astype(BF16)
            else:
                acc_ref[rows(r), :] = total

        gate_up(0)
        for r in range(parts):
            if r + 1 < parts:
                gate_up(r + 1)
            down(r)

    if n_chunks == 1:
        chunk(True, True)
    else:
        pl.when(c == 0)(lambda: chunk(True, False))
        if n_chunks > 2:
            pl.when((c > 0) & (c < n_chunks - 1))(lambda: chunk(False, False))
        pl.when(c == n_chunks - 1)(lambda: chunk(False, True))


def _ffn(xs, gates, w_gate, w_up, w_down, fc):
    ne, rows, d = xs.shape
    dff = w_gate.shape[2]
    parts = 2 if rows % 32 == 0 else 1
    return pl.pallas_call(
        functools.partial(_ffn_kernel, n_chunks=dff // fc),
        grid=(ne, dff // fc),
        in_specs=[pl.BlockSpec((None, rows, d), lambda e, c: (e, 0, 0)),
                  pl.BlockSpec((None, rows, 1), lambda e, c: (e, 0, 0)),
                  pl.BlockSpec((None, d, fc), lambda e, c: (e, 0, c)),
                  pl.BlockSpec((None, d, fc), lambda e, c: (e, 0, c)),
                  pl.BlockSpec((None, fc, d), lambda e, c: (e, c, 0))],
        out_specs=pl.BlockSpec((None, rows, d), lambda e, c: (e, 0, 0)),
        out_shape=jax.ShapeDtypeStruct((ne, rows, d), BF16),
        scratch_shapes=[pltpu.VMEM((rows, d), F32), pltpu.VMEM((parts, rows // parts, fc), F32),
                        pltpu.VMEM((parts, rows // parts, fc), F32), pltpu.VMEM((parts, rows // parts, fc), BF16)],
        compiler_params=_cparams(("parallel", "arbitrary")),
        name="ffn",
    )(xs, gates, w_gate, w_up, w_down)


def _combine_kernel(slot_ref, y_ref, h_ref, g_ref, b_ref, o_ref, moe_buf, *, cap, parts):
    ne, tl = slot_ref.shape
    pr = tl // parts
    rows = lambda r: slice(r * pr, (r + 1) * pr)
    slot_t = slot_ref[...].astype(F32).T
    lane = lax.broadcasted_iota(jnp.int32, (pr, cap), 1).astype(F32)
    y = y_ref[...].reshape(ne * cap, y_ref.shape[-1])

    def scatter(r):
        place = jnp.concatenate(
            [jnp.where(slot_t[rows(r), e:e + 1] == lane, 1.0, 0.0).astype(BF16) for e in range(ne)], axis=1)
        moe_buf[r % 2] = _dot(place, y)

    def normalize(r):
        o_ref[rows(r), :] = _layer_norm(DEEPNORM_ALPHA * h_ref[rows(r), :] + moe_buf[r % 2], g_ref[...], b_ref[...])

    scatter(0)
    for r in range(parts):
        if r + 1 < parts:
            scatter(r + 1)
        normalize(r)


def _combine(slots, y, h2, g, b, cap, tl):
    batch, ne, seq_len = slots.shape
    d = h2.shape[1]
    tiles = seq_len // tl
    parts = 4 if tl % 64 == 0 else 1
    row = pl.BlockSpec((tl, d), lambda bi, i: (bi * tiles + i, 0))
    const = pl.BlockSpec((1, d), lambda bi, i: (0, 0))
    return pl.pallas_call(
        functools.partial(_combine_kernel, cap=cap, parts=parts),
        grid=(batch, tiles),
        in_specs=[pl.BlockSpec((None, ne, tl), lambda bi, i: (bi, 0, i)),
                  pl.BlockSpec((ne, None, cap, d), lambda bi, i: (0, bi, 0, 0)),
                  row, const, const],
        out_specs=row,
        out_shape=jax.ShapeDtypeStruct((batch * seq_len, d), F32),
        scratch_shapes=[pltpu.VMEM((2, tl // parts, d), F32)],
        compiler_params=_cparams(("parallel", "parallel")),
        name="combine",
    )(slots, y, h2, g, b)


def kernel(x, mem, emb_ln_g, emb_ln_b, w_in, hg_lb_logits, hg_norm_g, da_lambda_q1, da_lambda_k1, da_lambda_q2,
           da_lambda_k2, da_subln_g, w_mix_out, ln1_g, ln1_b, xa_wq, xa_wk, xa_wv, xa_wo, ln2_g, ln2_b, w_router,
           w_gate, w_up, w_down, ln3_g, ln3_b):
    batch, seq_len, d = x.shape
    assert w_in.shape[0] == DEPTH and seq_len % HG_CHUNK == 0 and seq_len % LANES_V7X == 0
    n = batch * seq_len
    cap = EC_FACTOR * seq_len // N_EXPERTS
    vec = lambda a: a.reshape(1, -1)
    bf = lambda a: a.astype(BF16)

    t = _tiles(seq_len)

    h0, q, v, gt, ff, fb, dq, dk, dv = _inproj(
        x.reshape(n, d), vec(emb_ln_g), vec(emb_ln_b), bf(w_in[0]), seq_len, t.inproj_rows)
    hg = _hgrn(q, v, gt, ff, fb, hg_lb_logits, hg_norm_g[0], batch, seq_len)
    lam_vecs = jnp.stack([da_lambda_q1[0], da_lambda_k1[0], da_lambda_q2[0], da_lambda_k2[0]])
    da = _dattn(dq, dk, dv, lam_vecs, da_subln_g[0], batch, seq_len, t.dattn_queries, t.dattn_keys)
    kx, vx = _kvproj(mem, bf(xa_wk[0]), bf(xa_wv[0]))
    h2, aff_t = _mixattn(hg.reshape(n, -1), da.reshape(n, -1), h0, bf(w_mix_out[0]), ln1_g, ln1_b, kx, vx,
                         bf(xa_wq[0]), bf(xa_wo[0]), ln2_g, ln2_b, bf(w_router[0].T), batch, seq_len, t.mixattn_rows)

    slots = _route(aff_t, cap)
    xs, gates = _gather(slots, aff_t, h2, cap)
    y = _ffn(xs.reshape(N_EXPERTS, batch * cap, d), gates.reshape(N_EXPERTS, batch * cap, 1),
             w_gate[0], w_up[0], w_down[0], t.ffn_cols)
    out = _combine(slots, y.reshape(N_EXPERTS, batch, cap, d), h2, ln3_g, ln3_b, cap, t.combine_rows)
    return out.reshape(batch, seq_len, d)
```

```python
import functools
import math
from typing import NamedTuple

import jax
import jax.numpy as jnp
from jax import lax
from jax.experimental import pallas as pl
from jax.experimental.pallas import tpu as pltpu

F32 = jnp.float32
BF16 = jnp.bfloat16

HG_HEADS = 4
HG_DIM = 128
HG_CHUNK = 64
DA_HEADS = 4
DA_QK_DIM = 64
DA_V_DIM = 128
ROPE_DIM = DA_QK_DIM // 4
ROPE_THETA = 500000.0
XA_HEADS = 4
N_EXPERTS = 16
EC_FACTOR = 2
LN_EPS = 1e-5
RMS_EPS = 1e-6
DEPTH = 1
DEEPNORM_ALPHA = (2.0 * DEPTH) ** 0.25
LAMBDA_INIT = 0.8 - 0.6 * math.exp(-0.3 * 0)
LOG2_E = math.log2(math.e)

LANES_V7X = 128
BF16_SUBLANES_V7X = 16
VMEM_LIMIT_V7X = 56 * 1024 * 1024


class _Tiles(NamedTuple):
    inproj_rows: int
    dattn_queries: int
    dattn_keys: int
    mixattn_rows: int
    ffn_cols: int
    combine_rows: int


def _tiles(seq_len):
    fit = lambda rows: min(rows, seq_len)
    return _Tiles(inproj_rows=fit(1024), dattn_queries=fit(2048), dattn_keys=fit(512), mixattn_rows=fit(1024),
                  ffn_cols=512, combine_rows=fit(1024))


def _cparams(sem):
    return pltpu.CompilerParams(dimension_semantics=sem, vmem_limit_bytes=VMEM_LIMIT_V7X)


def _layer_norm(x, g, b):
    mu = jnp.mean(x, axis=-1, keepdims=True)
    xc = x - mu
    var = jnp.mean(xc * xc, axis=-1, keepdims=True)
    return xc * lax.rsqrt(var + LN_EPS) * g + b


def _silu(x):
    return x * jax.nn.sigmoid(x)


def _dot(a, b):
    return jnp.dot(a, b, preferred_element_type=F32)


def _dot_nt(a, b):
    return lax.dot_general(a, b, (((1,), (1,)), ((), ())), preferred_element_type=F32)


def _dot_tn(a, b):
    return lax.dot_general(a, b, (((0,), (0,)), ((), ())), preferred_element_type=F32)


def _inproj_kernel(x_ref, g_ref, b_ref, w_ref, c_ref, sa_ref, sb_ref,
                   h0_ref, q_ref, v_ref, gt_ref, ff_ref, fb_ref, dq_ref, dk_ref, dv_ref, hb_buf):
    width = q_ref.shape[1]
    parts = hb_buf.shape[0]
    pr = x_ref.shape[0] // parts
    rows = lambda r: slice(r * pr, (r + 1) * pr)
    widen = lambda ref, r: jnp.concatenate([ref[rows(r), :]] * (width // ref.shape[1]), axis=1)

    def normalize(r):
        h = _layer_norm(x_ref[rows(r), :], g_ref[...], b_ref[...])
        h0_ref[rows(r), :] = h
        hb_buf[r] = h.astype(BF16)

    def project(r, columns):
        proj = lambda c: _dot(hb_buf[r], w_ref[:, c * width:(c + 1) * width])
        rope_c, rope_sa, rope_sb = widen(c_ref, r), widen(sa_ref, r), widen(sb_ref, r)

        def rope(t):
            return (t * rope_c + pltpu.roll(t, width - ROPE_DIM // 2, 1) * rope_sa
                    + pltpu.roll(t, ROPE_DIM // 2, 1) * rope_sb)

        for c in columns:
            u = proj(c)
            if c == 0:
                q_ref[rows(r), :] = _silu(u).astype(BF16)
            elif c == 1:
                v_ref[rows(r), :] = u.astype(BF16)
            elif c == 2:
                gt_ref[rows(r), :] = _silu(u).astype(BF16)
            elif c == 3:
                ff_ref[rows(r), :] = u
            elif c == 4:
                fb_ref[rows(r), :] = u
            elif c == 5:
                dq_ref[rows(r), :] = (rope(u) * (DA_QK_DIM ** -0.5 * LOG2_E)).astype(BF16)
            elif c == 6:
                dk_ref[rows(r), :] = rope(u).astype(BF16)
            else:
                dv_ref[:, rows(r)] = u.T.astype(BF16)

    normalize(0)
    for r in range(parts):
        project(r, range(0, 2))
        if r + 1 < parts:
            normalize(r + 1)
        project(r, range(2, 8))


def _rope_lane_tables(seq_len, width):
    half = ROPE_DIM // 2
    inv = 1.0 / (ROPE_THETA ** (jnp.arange(0, ROPE_DIM, 2, dtype=F32) / ROPE_DIM))
    ang = jnp.arange(seq_len, dtype=F32)[:, None] * inv[None, :]
    cos, sin = jnp.cos(ang), jnp.sin(ang)
    pad1 = jnp.ones((seq_len, DA_QK_DIM - ROPE_DIM), F32)
    pad0 = jnp.zeros((seq_len, DA_QK_DIM - ROPE_DIM), F32)
    z = jnp.zeros((seq_len, half), F32)
    reps = width // DA_QK_DIM
    c = jnp.tile(jnp.concatenate([cos, cos, pad1], -1), (1, reps))
    sa = jnp.tile(jnp.concatenate([-sin, z, pad0], -1), (1, reps))
    sb = jnp.tile(jnp.concatenate([z, sin, pad0], -1), (1, reps))
    return c, sa, sb


def _inproj(x2, g, b, w_bf, seq_len, tm):
    n, d = x2.shape
    width = w_bf.shape[1] // 8
    c, sa, sb = _rope_lane_tables(seq_len, LANES_V7X)
    tpb = seq_len // tm
    parts = 4 if tm % 512 == 0 else 1
    row = lambda p, bi: (bi * tpb + p, 0)
    const = lambda p, bi: (0, 0)
    tab = lambda p, bi: (p, 0)
    wide = lambda dt: jax.ShapeDtypeStruct((n, width), dt)
    return pl.pallas_call(
        _inproj_kernel,
        grid=(tpb, n // seq_len),
        in_specs=[pl.BlockSpec((tm, d), row), pl.BlockSpec((1, d), const), pl.BlockSpec((1, d), const),
                  pl.BlockSpec(w_bf.shape, const),
                  pl.BlockSpec((tm, LANES_V7X), tab), pl.BlockSpec((tm, LANES_V7X), tab),
                  pl.BlockSpec((tm, LANES_V7X), tab)],
        out_specs=([pl.BlockSpec((tm, d), row)] + [pl.BlockSpec((tm, width), row)] * 7
                   + [pl.BlockSpec((width, tm), lambda p, bi: (0, bi * tpb + p))]),
        out_shape=[jax.ShapeDtypeStruct((n, d), F32), wide(BF16), wide(BF16), wide(BF16), wide(F32), wide(F32),
                   wide(BF16), wide(BF16), jax.ShapeDtypeStruct((width, n), BF16)],
        scratch_shapes=[pltpu.VMEM((parts, tm // parts, d), BF16)],
        compiler_params=_cparams(("parallel", "parallel")),
        name="inproj",
    )(x2, g, b, w_bf, c, sa, sb)


def _hgrn_kernel(q_ref, v_ref, gt_ref, ff_ref, fb_ref, lbl_ref, ng_ref, o_ref, st_ref, oacc_ref):
    seq_len, width = q_ref.shape
    heads = width // HG_DIM
    ch = HG_CHUNK
    blk = min(2 * ch, seq_len)
    cpb = blk // ch
    nblk = seq_len // blk
    sub_rows = 8
    tiles_per_chunk = ch // sub_rows
    lbl = lbl_ref[...]

    def lower_bound(d):
        l0, l1 = lbl[2 * d:2 * d + 1], lbl[2 * d + 1:2 * d + 2]
        m = jnp.maximum(l0, l1)
        e0, e1 = jnp.exp(l0 - m), jnp.exp(l1 - m)
        return e0 / (e0 + e1)

    lbs = (lower_bound(0), lower_bound(1))
    st_ref[...] = jnp.zeros_like(st_ref)
    sub = lax.broadcasted_iota(jnp.int32, (blk // sub_rows, sub_rows, width), 1)
    ri = lax.broadcasted_iota(jnp.int32, (blk, blk), 0)
    ci = lax.broadcasted_iota(jnp.int32, (blk, blk), 1)
    same_chunk = (ri // ch) == (ci // ch)
    masks = (same_chunk & (ri >= ci), same_chunk & (ri <= ci))
    row_chunk = (lax.broadcasted_iota(jnp.int32, (blk, HG_DIM), 0) // ch).astype(BF16)
    in_chunk = [row_chunk == j for j in range(cpb)]
    sl = lambda hh: slice(hh * HG_DIM, (hh + 1) * HG_DIM)

    def cumprod(x, backward):
        x3 = x.reshape(blk // sub_rows, sub_rows, width)
        s = 1
        while s < sub_rows:
            if backward:
                x3 = x3 * jnp.where(sub < sub_rows - s, pltpu.roll(x3, sub_rows - s, 1), 1.0)
            else:
                x3 = x3 * jnp.where(sub >= s, pltpu.roll(x3, s, 1), 1.0)
            s *= 2
        edge = 0 if backward else sub_rows - 1
        out = [None] * (blk // sub_rows)
        for c in range(cpb):
            carry = None
            order = range(tiles_per_chunk - 1, -1, -1) if backward else range(tiles_per_chunk)
            for t in order:
                idx = c * tiles_per_chunk + t
                out[idx] = x3[idx] if carry is None else x3[idx] * carry
                carry = out[idx][edge:edge + 1]
        return jnp.concatenate(out, axis=0)

    class Item:
        pass

    def prep(d, i):
        it = Item()
        it.d = d
        bi = i if d == 0 else nblk - 1 - i
        it.rows = pl.ds(pl.multiple_of(bi * blk, blk), blk)
        uf = (ff_ref if d == 0 else fb_ref)[it.rows, :]
        f = lbs[d] + (1.0 - lbs[d]) * jax.nn.sigmoid(uf)
        g = cumprod(f, d == 1)
        edge = ch - 1 if d == 0 else 0
        it.decays = [g[j * ch + edge:j * ch + edge + 1] for j in range(cpb)]
        kin = (1.0 - f) / g
        it.k_out = (kin * jnp.concatenate([jnp.broadcast_to(dj, (ch, width)) for dj in it.decays], axis=0)
                    ).astype(BF16)
        it.q_in = (q_ref[it.rows, :].astype(F32) * g).astype(BF16)
        it.k_in = kin.astype(BF16)
        it.vv = v_ref[it.rows, :]
        return it

    def scores(it):
        it.scores = [_dot_nt(it.q_in[:, sl(hh)], it.k_in[:, sl(hh)]) for hh in range(heads)]
        it.incs = []
        for hh in range(heads):
            k_diag = jnp.concatenate([jnp.where(m, it.k_out[:, sl(hh)], jnp.zeros((), BF16)) for m in in_chunk],
                                     axis=1)
            it.incs.append(_dot_tn(it.vv[:, sl(hh)], k_diag))

    def intra(it):
        it.intra = [_dot(jnp.where(masks[it.d], it.scores[hh], 0.0).astype(BF16), it.vv[:, sl(hh)])
                    for hh in range(heads)]

    def scan(it):
        d = it.d
        st = [st_ref[d * heads + hh] for hh in range(heads)]
        it.inter = [[None] * cpb for _ in range(heads)]
        for step in range(cpb):
            j = step if d == 0 else cpb - 1 - step
            for hh in range(heads):
                it.inter[hh][j] = _dot_nt(it.q_in[j * ch:(j + 1) * ch, sl(hh)], st[hh].astype(BF16))
            for hh in range(heads):
                st[hh] = st[hh] * it.decays[j][:, sl(hh)] + it.incs[hh][:, j * HG_DIM:(j + 1) * HG_DIM]
        for hh in range(heads):
            st_ref[d * heads + hh] = st[hh]

    def write(it, completes):
        ng = ng_ref[...]
        for hh in range(heads):
            o = it.intra[hh] + jnp.concatenate(it.inter[hh], axis=0)
            if not completes:
                oacc_ref[it.rows, sl(hh)] = o
                continue
            o = o + oacc_ref[it.rows, sl(hh)]
            ms = jnp.mean(o * o, axis=-1, keepdims=True)
            gate = gt_ref[it.rows, sl(hh)].astype(F32)
            o_ref[it.rows, sl(hh)] = (o * lax.rsqrt(ms + RMS_EPS) * ng[:, sl(hh)] * gate).astype(BF16)

    def body(i, fwd_completes, bwd_completes):
        fwd = prep(0, i)
        scores(fwd)
        bwd = prep(1, i)
        intra(fwd)
        scores(bwd)
        scan(fwd)
        intra(bwd)
        write(fwd, fwd_completes)
        scan(bwd)
        write(bwd, bwd_completes)

    half = nblk // 2

    def loop(lo, hi, completes):
        trips = hi - lo
        if trips > 0:
            lax.fori_loop(lo, hi, lambda i, c: (body(i, completes, completes), c)[1], 0,
                          unroll=8 if trips % 8 == 0 else 1)

    loop(0, half, False)
    if nblk % 2:
        body(half, False, True)
    loop(nblk - half, nblk, True)


def _hgrn(q, v, gt, ff, fb, lb_logits, norm_g, batch, seq_len):
    width = q.shape[1]
    hw = width
    sp = lambda a: a.reshape(batch, seq_len, width)
    blk = pl.BlockSpec((None, seq_len, hw), lambda b, h: (b, 0, h))
    return pl.pallas_call(
        _hgrn_kernel,
        grid=(batch, width // hw),
        in_specs=[blk, blk, blk, blk, blk,
                  pl.BlockSpec((4, hw), lambda b, h: (0, h)), pl.BlockSpec((1, hw), lambda b, h: (0, h))],
        out_specs=blk,
        out_shape=jax.ShapeDtypeStruct((batch, seq_len, width), BF16),
        scratch_shapes=[pltpu.VMEM((2 * (hw // HG_DIM), HG_DIM, HG_DIM), F32), pltpu.VMEM((seq_len, hw), F32)],
        compiler_params=_cparams(("parallel", "parallel")),
        name="hgrn",
    )(sp(q), sp(v), sp(gt), sp(ff), sp(fb), lb_logits.reshape(4, width), norm_g.reshape(1, width))


def _dattn_kernel(q_ref, k_ref, vt_ref, lam_ref, sg_ref, o_ref, s_buf, e_buf, acc_buf, *, kc):
    lv = lam_ref[...]
    lam = (jnp.exp(jnp.sum(lv[0:1] * lv[1:2], axis=-1, keepdims=True))
           - jnp.exp(jnp.sum(lv[2:3] * lv[3:4], axis=-1, keepdims=True)) + LAMBDA_INIT)
    q = q_ref[...]
    lane = lax.broadcasted_iota(jnp.int32, q.shape, 1)
    zero = jnp.zeros_like(q)
    qs = (jnp.where(lane < DA_QK_DIM, q, zero), jnp.where(lane >= DA_QK_DIM, q, zero))
    items = [(j, c) for j in range(k_ref.shape[0] // kc) for c in (0, 1)]
    nbuf = s_buf.shape[0]
    dv = vt_ref.shape[0]
    ones = jnp.ones((acc_buf.shape[1] - dv, kc), BF16)
    m, alpha = [None, None], {}

    def scores(i):
        j, c = items[i]
        s_buf[i % nbuf] = _dot_nt(k_ref[j * kc:(j + 1) * kc, :], qs[c])

    def exponentials(i):
        j, c = items[i]
        s = s_buf[i % nbuf]
        mj = jnp.max(s, axis=0, keepdims=True)
        if j == 0:
            m[c], alpha[i] = mj, None
        else:
            mn = jnp.maximum(m[c], mj)
            alpha[i] = jnp.exp2(m[c] - mn)
            m[c] = mn
        e_buf[i % nbuf] = jnp.exp2(s - m[c]).astype(BF16)

    def values(i):
        j, c = items[i]
        pv = _dot(jnp.concatenate([vt_ref[:, j * kc:(j + 1) * kc], ones], axis=0), e_buf[i % nbuf])
        acc_buf[c] = pv if alpha[i] is None else alpha[i] * acc_buf[c] + pv

    scores(0)
    scores(1)
    exponentials(0)
    for i in range(len(items)):
        if i + 2 < len(items):
            scores(i + 2)
        if i + 1 < len(items):
            exponentials(i + 1)
        values(i)
    a0, a1 = acc_buf[0], acc_buf[1]
    ot = a0[:dv] * (1.0 / a0[dv:dv + 1]) - a1[:dv] * (lam / a1[dv:dv + 1])
    ms = jnp.mean(ot * ot, axis=0, keepdims=True)
    y = ot * lax.rsqrt(ms + RMS_EPS) * (sg_ref[...] * (1.0 - LAMBDA_INIT))
    o_ref[...] = y.T.astype(BF16)


def _dattn(dq, dk, dvt, lam_vecs, subln_g, batch, seq_len, tq, kc):
    width = dq.shape[1]
    hw = DA_V_DIM
    sp = lambda a: a.reshape(batch, seq_len, width)
    kk = pl.BlockSpec((None, seq_len, hw), lambda b, h, i: (b, 0, h))
    qo = pl.BlockSpec((None, tq, hw), lambda b, h, i: (b, i, h))
    nbuf = 4
    return pl.pallas_call(
        functools.partial(_dattn_kernel, kc=kc),
        grid=(batch, width // hw, seq_len // tq),
        in_specs=[qo, kk, pl.BlockSpec((hw, seq_len), lambda b, h, i: (h, b)),
                  pl.BlockSpec(lam_vecs.shape, lambda b, h, i: (0, 0)),
                  pl.BlockSpec((hw, 1), lambda b, h, i: (0, 0))],
        out_specs=qo,
        out_shape=jax.ShapeDtypeStruct((batch, seq_len, width), BF16),
        scratch_shapes=[pltpu.VMEM((nbuf, kc, tq), F32), pltpu.VMEM((nbuf, kc, tq), BF16),
                        pltpu.VMEM((2, hw + BF16_SUBLANES_V7X, tq), F32)],
        compiler_params=_cparams(("parallel", "parallel", "parallel")),
        name="dattn",
    )(sp(dq), sp(dk), dvt, lam_vecs, subln_g.reshape(hw, 1))


def _kvproj_kernel(m_ref, wk_ref, wv_ref, k_ref, v_ref):
    mb = m_ref[...].astype(BF16)
    k_ref[...] = _dot(mb, wk_ref[...]).astype(BF16)
    v_ref[...] = _dot(mb, wv_ref[...]).astype(BF16)


def _kvproj(mem, wk_bf, wv_bf):
    batch, m, d = mem.shape
    blk = pl.BlockSpec((None, m, d), lambda b: (b, 0, 0))
    const = pl.BlockSpec((d, d), lambda b: (0, 0))
    return pl.pallas_call(
        _kvproj_kernel,
        grid=(batch,),
        in_specs=[blk, const, const],
        out_specs=[blk, blk],
        out_shape=[jax.ShapeDtypeStruct((batch, m, d), BF16)] * 2,
        compiler_params=_cparams(("parallel",)),
        name="kvproj",
    )(mem, wk_bf, wv_bf)


def _mixattn_kernel(hg_ref, da_ref, h0_ref, wm_ref, g1_ref, b1_ref, k_ref, v_ref, wq_ref, wo_ref, g2_ref, b2_ref,
                    wr_ref, h2_ref, aff_ref, mix_buf, h1_buf, *, parts):
    half = hg_ref.shape[1]
    pr = hg_ref.shape[0] // parts
    rows = lambda r: slice(r * pr, (r + 1) * pr)

    def project(r):
        mix_buf[r % 2] = _dot(hg_ref[rows(r), :], wm_ref[:half, :]) + _dot(da_ref[rows(r), :], wm_ref[half:, :])

    def normalize(r):
        h1_buf[rows(r), :] = _layer_norm(DEEPNORM_ALPHA * h0_ref[rows(r), :] + mix_buf[r % 2],
                                         g1_ref[...], b1_ref[...])

    project(0)
    for r in range(parts):
        if r + 1 < parts:
            project(r + 1)
        normalize(r)

    h1 = h1_buf[...]
    d = h1.shape[1]
    hd = d // XA_HEADS
    q = (_dot(h1.astype(BF16), wq_ref[...]) * (hd ** -0.5)).astype(BF16)
    outs = []
    for i in range(XA_HEADS):
        sl = slice(i * hd, (i + 1) * hd)
        s = _dot_nt(q[:, sl], k_ref[:, sl])
        e = jnp.exp(s - jnp.max(s, axis=-1, keepdims=True))
        inv_l = 1.0 / jnp.sum(e, axis=-1, keepdims=True)
        outs.append((_dot(e.astype(BF16), v_ref[:, sl]) * inv_l).astype(BF16))
    xa = _dot(jnp.concatenate(outs, axis=-1), wo_ref[...])
    h2 = _layer_norm(DEEPNORM_ALPHA * h1 + xa, g2_ref[...], b2_ref[...])
    h2_ref[...] = h2
    logits = _dot_nt(wr_ref[...], h2.astype(BF16))
    e = jnp.exp(logits - jnp.max(logits, axis=0, keepdims=True))
    aff_ref[...] = e / jnp.sum(e, axis=0, keepdims=True)


def _mixattn(hg, da, h0, wm_bf, g1, b1, kx, vx, wq_bf, wo_bf, g2, b2, wr_t_bf, batch, seq_len, tm):
    n, d = h0.shape
    half = hg.shape[1]
    m = kx.shape[1]
    ne = wr_t_bf.shape[0]
    tiles = seq_len // tm
    parts = 4 if tm % 64 == 0 else 1
    rows = lambda w: pl.BlockSpec((tm, w), lambda bi, i: (bi * tiles + i, 0))
    kv = pl.BlockSpec((None, m, d), lambda bi, i: (bi, 0, 0))
    const = lambda shape: pl.BlockSpec(shape, lambda bi, i: (0, 0))
    return pl.pallas_call(
        functools.partial(_mixattn_kernel, parts=parts),
        grid=(batch, tiles),
        in_specs=[rows(half), rows(half), rows(d), const((d, d)), const((1, d)), const((1, d)), kv, kv,
                  const((d, d)), const((d, d)), const((1, d)), const((1, d)), const((ne, d))],
        out_specs=[rows(d), pl.BlockSpec((None, ne, tm), lambda bi, i: (bi, 0, i))],
        out_shape=[jax.ShapeDtypeStruct((n, d), F32), jax.ShapeDtypeStruct((batch, ne, seq_len), F32)],
        scratch_shapes=[pltpu.VMEM((2, tm // parts, d), F32), pltpu.VMEM((tm, d), F32)],
        compiler_params=_cparams(("parallel", "parallel")),
        name="mixattn",
    )(hg, da, h0, wm_bf, g1, b1, kx, vx, wq_bf, wo_bf, g2, b2, wr_t_bf)


def _route_kernel(aff_ref, slot_ref, *, cap):
    aff = aff_ref[...]
    ne, seq_len = aff.shape
    nblk = seq_len // LANES_V7X

    def count(mask):
        return jnp.sum(jnp.where(mask, 1.0, 0.0), axis=-1, keepdims=True)

    bits = jnp.zeros((ne, 1), jnp.int32)
    for bit in range(30, -1, -1):
        cand = bits | (1 << bit)
        bits = jnp.where(count(aff >= pltpu.bitcast(cand, F32)) >= cap, cand, bits)
    thr = pltpu.bitcast(bits, F32)

    ri = lax.broadcasted_iota(jnp.int32, (LANES_V7X, LANES_V7X), 0)
    ci = lax.broadcasted_iota(jnp.int32, (LANES_V7X, LANES_V7X), 1)
    tri = jnp.where(ri <= ci, 1.0, 0.0).astype(BF16)

    def prefix_count(mask):
        m = jnp.where(mask, 1.0, 0.0)
        stack = jnp.concatenate([m[:, j * LANES_V7X:(j + 1) * LANES_V7X] for j in range(nblk)], axis=0)
        within = _dot(stack.astype(BF16), tri)
        pieces, carry = [], jnp.zeros((ne, 1), F32)
        for j in range(nblk):
            blk = within[j * ne:(j + 1) * ne]
            pieces.append(blk + carry)
            carry = carry + blk[:, LANES_V7X - 1:LANES_V7X]
        return jnp.concatenate(pieces, axis=1), m

    gt = aff > thr
    eq = aff == thr
    need = cap - count(gt)
    eq_incl, eq_f = prefix_count(eq)
    sel = gt | (eq & ((eq_incl - eq_f) < need))
    sel_incl, sel_f = prefix_count(sel)
    slot_ref[...] = jnp.where(sel, sel_incl - sel_f, -1.0).astype(jnp.int32)


def _route(aff_t, cap):
    batch, ne, seq_len = aff_t.shape
    blk = pl.BlockSpec((batch * ne, seq_len), lambda i: (0, 0))
    slots = pl.pallas_call(
        functools.partial(_route_kernel, cap=cap),
        grid=(1,),
        in_specs=[blk],
        out_specs=blk,
        out_shape=jax.ShapeDtypeStruct((batch * ne, seq_len), jnp.int32),
        compiler_params=_cparams(("arbitrary",)),
        name="route",
    )(aff_t.reshape(batch * ne, seq_len))
    return slots.reshape(batch, ne, seq_len)


def _gather_kernel(slot_ref, aff_ref, h_ref, xs_ref, gate_ref, *, cap):
    hb = h_ref[...].astype(BF16)
    ne, seq_len = slot_ref.shape
    rid = lax.broadcasted_iota(jnp.int32, (cap, seq_len), 0)
    for e in range(ne):
        pick = slot_ref[e:e + 1, :] == rid
        xs_ref[e] = _dot(jnp.where(pick, 1.0, 0.0).astype(BF16), hb).astype(BF16)
        gate_ref[e] = jnp.sum(jnp.where(pick, aff_ref[e:e + 1, :], 0.0), axis=-1, keepdims=True)


def _gather(slots, aff_t, h2, cap):
    batch, ne, seq_len = slots.shape
    d = h2.shape[1]
    blk = pl.BlockSpec((None, ne, seq_len), lambda b: (b, 0, 0))
    return pl.pallas_call(
        functools.partial(_gather_kernel, cap=cap),
        grid=(batch,),
        in_specs=[blk, blk, pl.BlockSpec((seq_len, d), lambda b: (b, 0))],
        out_specs=[pl.BlockSpec((ne, None, cap, d), lambda b: (0, b, 0, 0)),
                   pl.BlockSpec((ne, None, cap, 1), lambda b: (0, b, 0, 0))],
        out_shape=[jax.ShapeDtypeStruct((ne, batch, cap, d), BF16), jax.ShapeDtypeStruct((ne, batch, cap, 1), F32)],
        compiler_params=_cparams(("parallel",)),
        name="gather",
    )(slots, aff_t, h2)


def _ffn_kernel(xs_ref, gate_ref, wg_ref, wu_ref, wd_ref, y_ref, acc_ref, g_buf, u_buf, a_buf, *, n_chunks):
    c = pl.program_id(1)
    parts = g_buf.shape[0]
    pr = xs_ref.shape[0] // parts
    rows = lambda r: slice(r * pr, (r + 1) * pr)

    def chunk(first, final):
        wg, wu, wd = wg_ref[...].astype(BF16), wu_ref[...].astype(BF16), wd_ref[...].astype(BF16)

        def gate_up(r):
            xs = xs_ref[rows(r), :]
            g_buf[r] = _dot(xs, wg)
            u_buf[r] = _dot(xs, wu)

        def down(r):
            a_buf[r] = (_silu(g_buf[r]) * u_buf[r]).astype(BF16)
            part = _dot(a_buf[r], wd)
            total = part if first else acc_ref[rows(r), :] + part
            if final:
                y_ref[rows(r), :] = (total * gate_ref[rows(r), :]).astype(BF16)
            else:
                acc_ref[rows(r), :] = total

        gate_up(0)
        for r in range(parts):
            if r + 1 < parts:
                gate_up(r + 1)
            down(r)

    if n_chunks == 1:
        chunk(True, True)
    else:
        pl.when(c == 0)(lambda: chunk(True, False))
        if n_chunks > 2:
            pl.when((c > 0) & (c < n_chunks - 1))(lambda: chunk(False, False))
        pl.when(c == n_chunks - 1)(lambda: chunk(False, True))


def _ffn(xs, gates, w_gate, w_up, w_down, fc):
    ne, rows, d = xs.shape
    dff = w_gate.shape[2]
    parts = 2 if rows % 32 == 0 else 1
    return pl.pallas_call(
        functools.partial(_ffn_kernel, n_chunks=dff // fc),
        grid=(ne, dff // fc),
        in_specs=[pl.BlockSpec((None, rows, d), lambda e, c: (e, 0, 0)),
                  pl.BlockSpec((None, rows, 1), lambda e, c: (e, 0, 0)),
                  pl.BlockSpec((None, d, fc), lambda e, c: (e, 0, c)),
                  pl.BlockSpec((None, d, fc), lambda e, c: (e, 0, c)),
                  pl.BlockSpec((None, fc, d), lambda e, c: (e, c, 0))],
        out_specs=pl.BlockSpec((None, rows, d), lambda e, c: (e, 0, 0)),
        out_shape=jax.ShapeDtypeStruct((ne, rows, d), BF16),
        scratch_shapes=[pltpu.VMEM((rows, d), F32), pltpu.VMEM((parts, rows // parts, fc), F32),
                        pltpu.VMEM((parts, rows // parts, fc), F32), pltpu.VMEM((parts, rows // parts, fc), BF16)],
        compiler_params=_cparams(("parallel", "arbitrary")),
        name="ffn",
    )(xs, gates, w_gate, w_up, w_down)


def _combine_kernel(slot_ref, y_ref, h_ref, g_ref, b_ref, o_ref, moe_buf, *, cap, parts):
    ne, tl = slot_ref.shape
    pr = tl // parts
    rows = lambda r: slice(r * pr, (r + 1) * pr)
    slot_t = slot_ref[...].astype(F32).T
    lane = lax.broadcasted_iota(jnp.int32, (pr, cap), 1).astype(F32)
    y = y_ref[...].reshape(ne * cap, y_ref.shape[-1])

    def scatter(r):
        place = jnp.concatenate(
            [jnp.where(slot_t[rows(r), e:e + 1] == lane, 1.0, 0.0).astype(BF16) for e in range(ne)], axis=1)
        moe_buf[r % 2] = _dot(place, y)

    def normalize(r):
        o_ref[rows(r), :] = _layer_norm(DEEPNORM_ALPHA * h_ref[rows(r), :] + moe_buf[r % 2], g_ref[...], b_ref[...])

    scatter(0)
    for r in range(parts):
        if r + 1 < parts:
            scatter(r + 1)
        normalize(r)


def _combine(slots, y, h2, g, b, cap, tl):
    batch, ne, seq_len = slots.shape
    d = h2.shape[1]
    tiles = seq_len // tl
    parts = 4 if tl % 64 == 0 else 1
    row = pl.BlockSpec((tl, d), lambda bi, i: (bi * tiles + i, 0))
    const = pl.BlockSpec((1, d), lambda bi, i: (0, 0))
    return pl.pallas_call(
        functools.partial(_combine_kernel, cap=cap, parts=parts),
        grid=(batch, tiles),
        in_specs=[pl.BlockSpec((None, ne, tl), lambda bi, i: (bi, 0, i)),
                  pl.BlockSpec((ne, None, cap, d), lambda bi, i: (0, bi, 0, 0)),
                  row, const, const],
        out_specs=row,
        out_shape=jax.ShapeDtypeStruct((batch * seq_len, d), F32),
        scratch_shapes=[pltpu.VMEM((2, tl // parts, d), F32)],
        compiler_params=_cparams(("parallel", "parallel")),
        name="combine",
    )(slots, y, h2, g, b)


def kernel(x, mem, emb_ln_g, emb_ln_b, w_in, hg_lb_logits, hg_norm_g, da_lambda_q1, da_lambda_k1, da_lambda_q2,
           da_lambda_k2, da_subln_g, w_mix_out, ln1_g, ln1_b, xa_wq, xa_wk, xa_wv, xa_wo, ln2_g, ln2_b, w_router,
           w_gate, w_up, w_down, ln3_g, ln3_b):
    batch, seq_len, d = x.shape
    assert w_in.shape[0] == DEPTH and seq_len % HG_CHUNK == 0 and seq_len % LANES_V7X == 0
    n = batch * seq_len
    cap = EC_FACTOR * seq_len // N_EXPERTS
    vec = lambda a: a.reshape(1, -1)
    bf = lambda a: a.astype(BF16)

    t = _tiles(seq_len)

    h0, q, v, gt, ff, fb, dq, dk, dv = _inproj(
        x.reshape(n, d), vec(emb_ln_g), vec(emb_ln_b), bf(w_in[0]), seq_len, t.inproj_rows)
    hg = _hgrn(q, v, gt, ff, fb, hg_lb_logits, hg_norm_g[0], batch, seq_len)
    lam_vecs = jnp.stack([da_lambda_q1[0], da_lambda_k1[0], da_lambda_q2[0], da_lambda_k2[0]])
    da = _dattn(dq, dk, dv, lam_vecs, da_subln_g[0], batch, seq_len, t.dattn_queries, t.dattn_keys)
    kx, vx = _kvproj(mem, bf(xa_wk[0]), bf(xa_wv[0]))
    h2, aff_t = _mixattn(hg.reshape(n, -1), da.reshape(n, -1), h0, bf(w_mix_out[0]), ln1_g, ln1_b, kx, vx,
                         bf(xa_wq[0]), bf(xa_wo[0]), ln2_g, ln2_b, bf(w_router[0].T), batch, seq_len, t.mixattn_rows)

    slots = _route(aff_t, cap)
    xs, gates = _gather(slots, aff_t, h2, cap)
    y = _ffn(xs.reshape(N_EXPERTS, batch * cap, d), gates.reshape(N_EXPERTS, batch * cap, 1),
             w_gate[0], w_up[0], w_down[0], t.ffn_cols)
    out = _combine(slots, y.reshape(N_EXPERTS, batch, cap, d), h2, ln3_g, ln3_b, cap, t.combine_rows)
    return out.reshape(batch, seq_len, d)
```

```python
import functools
import math
from typing import NamedTuple

import jax
import jax.numpy as jnp
from jax import lax
from jax.experimental import pallas as pl
from jax.experimental.pallas import tpu as pltpu

F32 = jnp.float32
BF16 = jnp.bfloat16

HG_HEADS = 4
HG_DIM = 128
HG_CHUNK = 64
DA_HEADS = 4
DA_QK_DIM = 64
DA_V_DIM = 128
ROPE_DIM = DA_QK_DIM // 4
ROPE_THETA = 500000.0
XA_HEADS = 4
N_EXPERTS = 16
EC_FACTOR = 2
LN_EPS = 1e-5
RMS_EPS = 1e-6
DEPTH = 1
DEEPNORM_ALPHA = (2.0 * DEPTH) ** 0.25
LAMBDA_INIT = 0.8 - 0.6 * math.exp(-0.3 * 0)
LOG2_E = math.log2(math.e)

LANES_V7X = 128
BF16_SUBLANES_V7X = 16
VMEM_LIMIT_V7X = 56 * 1024 * 1024


class _Tiles(NamedTuple):
    inproj_rows: int
    dattn_queries: int
    dattn_keys: int
    mixattn_rows: int
    ffn_cols: int
    combine_rows: int


def _tiles(seq_len):
    fit = lambda rows: min(rows, seq_len)
    return _Tiles(inproj_rows=fit(1024), dattn_queries=fit(2048), dattn_keys=fit(512), mixattn_rows=fit(1024),
                  ffn_cols=512, combine_rows=fit(1024))


def _cparams(sem):
    return pltpu.CompilerParams(dimension_semantics=sem, vmem_limit_bytes=VMEM_LIMIT_V7X)


def _layer_norm(x, g, b):
    mu = jnp.mean(x, axis=-1, keepdims=True)
    xc = x - mu
    var = jnp.mean(xc * xc, axis=-1, keepdims=True)
    return xc * lax.rsqrt(var + LN_EPS) * g + b


def _silu(x):
    return x * jax.nn.sigmoid(x)


def _dot(a, b):
    return jnp.dot(a, b, preferred_element_type=F32)


def _dot_nt(a, b):
    return lax.dot_general(a, b, (((1,), (1,)), ((), ())), preferred_element_type=F32)


def _dot_tn(a, b):
    return lax.dot_general(a, b, (((0,), (0,)), ((), ())), preferred_element_type=F32)


def _inproj_kernel(x_ref, g_ref, b_ref, w_ref, c_ref, sa_ref, sb_ref,
                   h0_ref, q_ref, v_ref, gt_ref, ff_ref, fb_ref, dq_ref, dk_ref, dv_ref, hb_buf):
    width = q_ref.shape[1]
    parts = hb_buf.shape[0]
    pr = x_ref.shape[0] // parts
    rows = lambda r: slice(r * pr, (r + 1) * pr)
    widen = lambda ref, r: jnp.concatenate([ref[rows(r), :]] * (width // ref.shape[1]), axis=1)

    def normalize(r):
        h = _layer_norm(x_ref[rows(r), :], g_ref[...], b_ref[...])
        h0_ref[rows(r), :] = h
        hb_buf[r] = h.astype(BF16)

    def project(r, columns):
        proj = lambda c: _dot(hb_buf[r], w_ref[:, c * width:(c + 1) * width])
        rope_c, rope_sa, rope_sb = widen(c_ref, r), widen(sa_ref, r), widen(sb_ref, r)

        def rope(t):
            return (t * rope_c + pltpu.roll(t, width - ROPE_DIM // 2, 1) * rope_sa
                    + pltpu.roll(t, ROPE_DIM // 2, 1) * rope_sb)

        for c in columns:
            u = proj(c)
            if c == 0:
                q_ref[rows(r), :] = _silu(u).astype(BF16)
            elif c == 1:
                v_ref[rows(r), :] = u.astype(BF16)
            elif c == 2:
                gt_ref[rows(r), :] = _silu(u).astype(BF16)
            elif c == 3:
                ff_ref[rows(r), :] = u
            elif c == 4:
                fb_ref[rows(r), :] = u
            elif c == 5:
                dq_ref[rows(r), :] = (rope(u) * (DA_QK_DIM ** -0.5 * LOG2_E)).astype(BF16)
            elif c == 6:
                dk_ref[rows(r), :] = rope(u).astype(BF16)
            else:
                dv_ref[:, rows(r)] = u.T.astype(BF16)

    normalize(0)
    for r in range(parts):
        project(r, range(0, 2))
        if r + 1 < parts:
            normalize(r + 1)
        project(r, range(2, 8))


def _rope_lane_tables(seq_len, width):
    half = ROPE_DIM // 2
    inv = 1.0 / (ROPE_THETA ** (jnp.arange(0, ROPE_DIM, 2, dtype=F32) / ROPE_DIM))
    ang = jnp.arange(seq_len, dtype=F32)[:, None] * inv[None, :]
    cos, sin = jnp.cos(ang), jnp.sin(ang)
    pad1 = jnp.ones((seq_len, DA_QK_DIM - ROPE_DIM), F32)
    pad0 = jnp.zeros((seq_len, DA_QK_DIM - ROPE_DIM), F32)
    z = jnp.zeros((seq_len, half), F32)
    reps = width // DA_QK_DIM
    c = jnp.tile(jnp.concatenate([cos, cos, pad1], -1), (1, reps))
    sa = jnp.tile(jnp.concatenate([-sin, z, pad0], -1), (1, reps))
    sb = jnp.tile(jnp.concatenate([z, sin, pad0], -1), (1, reps))
    return c, sa, sb


def _inproj(x2, g, b, w_bf, seq_len, tm):
    n, d = x2.shape
    width = w_bf.shape[1] // 8
    c, sa, sb = _rope_lane_tables(seq_len, LANES_V7X)
    tpb = seq_len // tm
    parts = 4 if tm % 512 == 0 else 1
    row = lambda p, bi: (bi * tpb + p, 0)
    const = lambda p, bi: (0, 0)
    tab = lambda p, bi: (p, 0)
    wide = lambda dt: jax.ShapeDtypeStruct((n, width), dt)
    return pl.pallas_call(
        _inproj_kernel,
        grid=(tpb, n // seq_len),
        in_specs=[pl.BlockSpec((tm, d), row), pl.BlockSpec((1, d), const), pl.BlockSpec((1, d), const),
                  pl.BlockSpec(w_bf.shape, const),
                  pl.BlockSpec((tm, LANES_V7X), tab), pl.BlockSpec((tm, LANES_V7X), tab),
                  pl.BlockSpec((tm, LANES_V7X), tab)],
        out_specs=([pl.BlockSpec((tm, d), row)] + [pl.BlockSpec((tm, width), row)] * 7
                   + [pl.BlockSpec((width, tm), lambda p, bi: (0, bi * tpb + p))]),
        out_shape=[jax.ShapeDtypeStruct((n, d), F32), wide(BF16), wide(BF16), wide(BF16), wide(F32), wide(F32),
                   wide(BF16), wide(BF16), jax.ShapeDtypeStruct((width, n), BF16)],
        scratch_shapes=[pltpu.VMEM((parts, tm // parts, d), BF16)],
        compiler_params=_cparams(("parallel", "parallel")),
        name="inproj",
    )(x2, g, b, w_bf, c, sa, sb)


def _hgrn_kernel(q_ref, v_ref, gt_ref, ff_ref, fb_ref, lbl_ref, ng_ref, o_ref, st_ref, oacc_ref):
    seq_len, width = q_ref.shape
    heads = width // HG_DIM
    ch = HG_CHUNK
    blk = min(2 * ch, seq_len)
    cpb = blk // ch
    nblk = seq_len // blk
    sub_rows = 8
    tiles_per_chunk = ch // sub_rows
    lbl = lbl_ref[...]

    def lower_bound(d):
        l0, l1 = lbl[2 * d:2 * d + 1], lbl[2 * d + 1:2 * d + 2]
        m = jnp.maximum(l0, l1)
        e0, e1 = jnp.exp(l0 - m), jnp.exp(l1 - m)
        return e0 / (e0 + e1)

    lbs = (lower_bound(0), lower_bound(1))
    st_ref[...] = jnp.zeros_like(st_ref)
    sub = lax.broadcasted_iota(jnp.int32, (blk // sub_rows, sub_rows, width), 1)
    ri = lax.broadcasted_iota(jnp.int32, (blk, blk), 0)
    ci = lax.broadcasted_iota(jnp.int32, (blk, blk), 1)
    same_chunk = (ri // ch) == (ci // ch)
    masks = (same_chunk & (ri >= ci), same_chunk & (ri <= ci))
    row_chunk = (lax.broadcasted_iota(jnp.int32, (blk, HG_DIM), 0) // ch).astype(BF16)
    in_chunk = [row_chunk == j for j in range(cpb)]
    sl = lambda hh: slice(hh * HG_DIM, (hh + 1) * HG_DIM)

    def cumprod(x, backward):
        x3 = x.reshape(blk // sub_rows, sub_rows, width)
        s = 1
        while s < sub_rows:
            if backward:
                x3 = x3 * jnp.where(sub < sub_rows - s, pltpu.roll(x3, sub_rows - s, 1), 1.0)
            else:
                x3 = x3 * jnp.where(sub >= s, pltpu.roll(x3, s, 1), 1.0)
            s *= 2
        edge = 0 if backward else sub_rows - 1
        out = [None] * (blk // sub_rows)
        for c in range(cpb):
            carry = None
            order = range(tiles_per_chunk - 1, -1, -1) if backward else range(tiles_per_chunk)
            for t in order:
                idx = c * tiles_per_chunk + t
                out[idx] = x3[idx] if carry is None else x3[idx] * carry
                carry = out[idx][edge:edge + 1]
        return jnp.concatenate(out, axis=0)

    class Item:
        pass

    def prep(d, i):
        it = Item()
        it.d = d
        bi = i if d == 0 else nblk - 1 - i
        it.rows = pl.ds(pl.multiple_of(bi * blk, blk), blk)
        uf = (ff_ref if d == 0 else fb_ref)[it.rows, :]
        f = lbs[d] + (1.0 - lbs[d]) * jax.nn.sigmoid(uf)
        g = cumprod(f, d == 1)
        edge = ch - 1 if d == 0 else 0
        it.decays = [g[j * ch + edge:j * ch + edge + 1] for j in range(cpb)]
        kin = (1.0 - f) / g
        it.k_out = (kin * jnp.concatenate([jnp.broadcast_to(dj, (ch, width)) for dj in it.decays], axis=0)
                    ).astype(BF16)
        it.q_in = (q_ref[it.rows, :].astype(F32) * g).astype(BF16)
        it.k_in = kin.astype(BF16)
        it.vv = v_ref[it.rows, :]
        return it

    def scores(it):
        it.scores = [_dot_nt(it.q_in[:, sl(hh)], it.k_in[:, sl(hh)]) for hh in range(heads)]
        it.incs = []
        for hh in range(heads):
            k_diag = jnp.concatenate([jnp.where(m, it.k_out[:, sl(hh)], jnp.zeros((), BF16)) for m in in_chunk],
                                     axis=1)
            it.incs.append(_dot_tn(it.vv[:, sl(hh)], k_diag))

    def intra(it):
        it.intra = [_dot(jnp.where(masks[it.d], it.scores[hh], 0.0).astype(BF16), it.vv[:, sl(hh)])
                    for hh in range(heads)]

    def scan(it):
        d = it.d
        st = [st_ref[d * heads + hh] for hh in range(heads)]
        it.inter = [[None] * cpb for _ in range(heads)]
        for step in range(cpb):
            j = step if d == 0 else cpb - 1 - step
            for hh in range(heads):
                it.inter[hh][j] = _dot_nt(it.q_in[j * ch:(j + 1) * ch, sl(hh)], st[hh].astype(BF16))
            for hh in range(heads):
                st[hh] = st[hh] * it.decays[j][:, sl(hh)] + it.incs[hh][:, j * HG_DIM:(j + 1) * HG_DIM]
        for hh in range(heads):
            st_ref[d * heads + hh] = st[hh]

    def write(it, completes):
        ng = ng_ref[...]
        for hh in range(heads):
            o = it.intra[hh] + jnp.concatenate(it.inter[hh], axis=0)
            if not completes:
                oacc_ref[it.rows, sl(hh)] = o
                continue
            o = o + oacc_ref[it.rows, sl(hh)]
            ms = jnp.mean(o * o, axis=-1, keepdims=True)
            gate = gt_ref[it.rows, sl(hh)].astype(F32)
            o_ref[it.rows, sl(hh)] = (o * lax.rsqrt(ms + RMS_EPS) * ng[:, sl(hh)] * gate).astype(BF16)

    def body(i, fwd_completes, bwd_completes):
        fwd = prep(0, i)
        scores(fwd)
        bwd = prep(1, i)
        intra(fwd)
        scores(bwd)
        scan(fwd)
        intra(bwd)
        write(fwd, fwd_completes)
        scan(bwd)
        write(bwd, bwd_completes)

    half = nblk // 2

    def loop(lo, hi, completes):
        trips = hi - lo
        if trips > 0:
            lax.fori_loop(lo, hi, lambda i, c: (body(i, completes, completes), c)[1], 0,
                          unroll=8 if trips % 8 == 0 else 1)

    loop(0, half, False)
    if nblk % 2:
        body(half, False, True)
    loop(nblk - half, nblk, True)


def _hgrn(q, v, gt, ff, fb, lb_logits, norm_g, batch, seq_len):
    width = q.shape[1]
    hw = width
    sp = lambda a: a.reshape(batch, seq_len, width)
    blk = pl.BlockSpec((None, seq_len, hw), lambda b, h: (b, 0, h))
    return pl.pallas_call(
        _hgrn_kernel,
        grid=(batch, width // hw),
        in_specs=[blk, blk, blk, blk, blk,
                  pl.BlockSpec((4, hw), lambda b, h: (0, h)), pl.BlockSpec((1, hw), lambda b, h: (0, h))],
        out_specs=blk,
        out_shape=jax.ShapeDtypeStruct((batch, seq_len, width), BF16),
        scratch_shapes=[pltpu.VMEM((2 * (hw // HG_DIM), HG_DIM, HG_DIM), F32), pltpu.VMEM((seq_len, hw), F32)],
        compiler_params=_cparams(("parallel", "parallel")),
        name="hgrn",
    )(sp(q), sp(v), sp(gt), sp(ff), sp(fb), lb_logits.reshape(4, width), norm_g.reshape(1, width))


def _dattn_kernel(q_ref, k_ref, vt_ref, lam_ref, sg_ref, o_ref, s_buf, e_buf, acc_buf, *, kc):
    lv = lam_ref[...]
    lam = (jnp.exp(jnp.sum(lv[0:1] * lv[1:2], axis=-1, keepdims=True))
           - jnp.exp(jnp.sum(lv[2:3] * lv[3:4], axis=-1, keepdims=True)) + LAMBDA_INIT)
    q = q_ref[...]
    lane = lax.broadcasted_iota(jnp.int32, q.shape, 1)
    zero = jnp.zeros_like(q)
    qs = (jnp.where(lane < DA_QK_DIM, q, zero), jnp.where(lane >= DA_QK_DIM, q, zero))
    items = [(j, c) for j in range(k_ref.shape[0] // kc) for c in (0, 1)]
    nbuf = s_buf.shape[0]
    dv = vt_ref.shape[0]
    ones = jnp.ones((acc_buf.shape[1] - dv, kc), BF16)
    m, alpha = [None, None], {}

    def scores(i):
        j, c = items[i]
        s_buf[i % nbuf] = _dot_nt(k_ref[j * kc:(j + 1) * kc, :], qs[c])

    def exponentials(i):
        j, c = items[i]
        s = s_buf[i % nbuf]
        mj = jnp.max(s, axis=0, keepdims=True)
        if j == 0:
            m[c], alpha[i] = mj, None
        else:
            mn = jnp.maximum(m[c], mj)
            alpha[i] = jnp.exp2(m[c] - mn)
            m[c] = mn
        e_buf[i % nbuf] = jnp.exp2(s - m[c]).astype(BF16)

    def values(i):
        j, c = items[i]
        pv = _dot(jnp.concatenate([vt_ref[:, j * kc:(j + 1) * kc], ones], axis=0), e_buf[i % nbuf])
        acc_buf[c] = pv if alpha[i] is None else alpha[i] * acc_buf[c] + pv

    scores(0)
    scores(1)
    exponentials(0)
    for i in range(len(items)):
        if i + 2 < len(items):
            scores(i + 2)
        if i + 1 < len(items):
            exponentials(i + 1)
        values(i)
    a0, a1 = acc_buf[0], acc_buf[1]
    ot = a0[:dv] * (1.0 / a0[dv:dv + 1]) - a1[:dv] * (lam / a1[dv:dv + 1])
    ms = jnp.mean(ot * ot, axis=0, keepdims=True)
    y = ot * lax.rsqrt(ms + RMS_EPS) * (sg_ref[...] * (1.0 - LAMBDA_INIT))
    o_ref[...] = y.T.astype(BF16)


def _dattn(dq, dk, dvt, lam_vecs, subln_g, batch, seq_len, tq, kc):
    width = dq.shape[1]
    hw = DA_V_DIM
    sp = lambda a: a.reshape(batch, seq_len, width)
    kk = pl.BlockSpec((None, seq_len, hw), lambda b, h, i: (b, 0, h))
    qo = pl.BlockSpec((None, tq, hw), lambda b, h, i: (b, i, h))
    nbuf = 4
    return pl.pallas_call(
        functools.partial(_dattn_kernel, kc=kc),
        grid=(batch, width // hw, seq_len // tq),
        in_specs=[qo, kk, pl.BlockSpec((hw, seq_len), lambda b, h, i: (h, b)),
                  pl.BlockSpec(lam_vecs.shape, lambda b, h, i: (0, 0)),
                  pl.BlockSpec((hw, 1), lambda b, h, i: (0, 0))],
        out_specs=qo,
        out_shape=jax.ShapeDtypeStruct((batch, seq_len, width), BF16),
        scratch_shapes=[pltpu.VMEM((nbuf, kc, tq), F32), pltpu.VMEM((nbuf, kc, tq), BF16),
                        pltpu.VMEM((2, hw + BF16_SUBLANES_V7X, tq), F32)],
        compiler_params=_cparams(("parallel", "parallel", "parallel")),
        name="dattn",
    )(sp(dq), sp(dk), dvt, lam_vecs, subln_g.reshape(hw, 1))


def _kvproj_kernel(m_ref, wk_ref, wv_ref, k_ref, v_ref):
    mb = m_ref[...].astype(BF16)
    k_ref[...] = _dot(mb, wk_ref[...]).astype(BF16)
    v_ref[...] = _dot(mb, wv_ref[...]).astype(BF16)


def _kvproj(mem, wk_bf, wv_bf):
    batch, m, d = mem.shape
    blk = pl.BlockSpec((None, m, d), lambda b: (b, 0, 0))
    const = pl.BlockSpec((d, d), lambda b: (0, 0))
    return pl.pallas_call(
        _kvproj_kernel,
        grid=(batch,),
        in_specs=[blk, const, const],
        out_specs=[blk, blk],
        out_shape=[jax.ShapeDtypeStruct((batch, m, d), BF16)] * 2,
        compiler_params=_cparams(("parallel",)),
        name="kvproj",
    )(mem, wk_bf, wv_bf)


def _mixattn_kernel(hg_ref, da_ref, h0_ref, wm_ref, g1_ref, b1_ref, k_ref, v_ref, wq_ref, wo_ref, g2_ref, b2_ref,
                    wr_ref, h2_ref, aff_ref, mix_buf, h1_buf, *, parts):
    half = hg_ref.shape[1]
    pr = hg_ref.shape[0] // parts
    rows = lambda r: slice(r * pr, (r + 1) * pr)

    def project(r):
        mix_buf[r % 2] = _dot(hg_ref[rows(r), :], wm_ref[:half, :]) + _dot(da_ref[rows(r), :], wm_ref[half:, :])

    def normalize(r):
        h1_buf[rows(r), :] = _layer_norm(DEEPNORM_ALPHA * h0_ref[rows(r), :] + mix_buf[r % 2],
                                         g1_ref[...], b1_ref[...])

    project(0)
    for r in range(parts):
        if r + 1 < parts:
            project(r + 1)
        normalize(r)

    h1 = h1_buf[...]
    d = h1.shape[1]
    hd = d // XA_HEADS
    q = (_dot(h1.astype(BF16), wq_ref[...]) * (hd ** -0.5)).astype(BF16)
    outs = []
    for i in range(XA_HEADS):
        sl = slice(i * hd, (i + 1) * hd)
        s = _dot_nt(q[:, sl], k_ref[:, sl])
        e = jnp.exp(s - jnp.max(s, axis=-1, keepdims=True))
        inv_l = 1.0 / jnp.sum(e, axis=-1, keepdims=True)
        outs.append((_dot(e.astype(BF16), v_ref[:, sl]) * inv_l).astype(BF16))
    xa = _dot(jnp.concatenate(outs, axis=-1), wo_ref[...])
    h2 = _layer_norm(DEEPNORM_ALPHA * h1 + xa, g2_ref[...], b2_ref[...])
    h2_ref[...] = h2
    logits = _dot_nt(wr_ref[...], h2.astype(BF16))
    e = jnp.exp(logits - jnp.max(logits, axis=0, keepdims=True))
    aff_ref[...] = e / jnp.sum(e, axis=0, keepdims=True)


def _mixattn(hg, da, h0, wm_bf, g1, b1, kx, vx, wq_bf, wo_bf, g2, b2, wr_t_bf, batch, seq_len, tm):
    n, d = h0.shape
    half = hg.shape[1]
    m = kx.shape[1]
    ne = wr_t_bf.shape[0]
    tiles = seq_len // tm
    parts = 4 if tm % 64 == 0 else 1
    rows = lambda w: pl.BlockSpec((tm, w), lambda bi, i: (bi * tiles + i, 0))
    kv = pl.BlockSpec((None, m, d), lambda bi, i: (bi, 0, 0))
    const = lambda shape: pl.BlockSpec(shape, lambda bi, i: (0, 0))
    return pl.pallas_call(
        functools.partial(_mixattn_kernel, parts=parts),
        grid=(batch, tiles),
        in_specs=[rows(half), rows(half), rows(d), const((d, d)), const((1, d)), const((1, d)), kv, kv,
                  const((d, d)), const((d, d)), const((1, d)), const((1, d)), const((ne, d))],
        out_specs=[rows(d), pl.BlockSpec((None, ne, tm), lambda bi, i: (bi, 0, i))],
        out_shape=[jax.ShapeDtypeStruct((n, d), F32), jax.ShapeDtypeStruct((batch, ne, seq_len), F32)],
        scratch_shapes=[pltpu.VMEM((2, tm // parts, d), F32), pltpu.VMEM((tm, d), F32)],
        compiler_params=_cparams(("parallel", "parallel")),
        name="mixattn",
    )(hg, da, h0, wm_bf, g1, b1, kx, vx, wq_bf, wo_bf, g2, b2, wr_t_bf)


def _route_kernel(aff_ref, slot_ref, *, cap):
    aff = aff_ref[...]
    ne, seq_len = aff.shape
    nblk = seq_len // LANES_V7X

    def count(mask):
        return jnp.sum(jnp.where(mask, 1.0, 0.0), axis=-1, keepdims=True)

    bits = jnp.zeros((ne, 1), jnp.int32)
    for bit in range(30, -1, -1):
        cand = bits | (1 << bit)
        bits = jnp.where(count(aff >= pltpu.bitcast(cand, F32)) >= cap, cand, bits)
    thr = pltpu.bitcast(bits, F32)

    ri = lax.broadcasted_iota(jnp.int32, (LANES_V7X, LANES_V7X), 0)
    ci = lax.broadcasted_iota(jnp.int32, (LANES_V7X, LANES_V7X), 1)
    tri = jnp.where(ri <= ci, 1.0, 0.0).astype(BF16)

    def prefix_count(mask):
        m = jnp.where(mask, 1.0, 0.0)
        stack = jnp.concatenate([m[:, j * LANES_V7X:(j + 1) * LANES_V7X] for j in range(nblk)], axis=0)
        within = _dot(stack.astype(BF16), tri)
        pieces, carry = [], jnp.zeros((ne, 1), F32)
        for j in range(nblk):
            blk = within[j * ne:(j + 1) * ne]
            pieces.append(blk + carry)
            carry = carry + blk[:, LANES_V7X - 1:LANES_V7X]
        return jnp.concatenate(pieces, axis=1), m

    gt = aff > thr
    eq = aff == thr
    need = cap - count(gt)
    eq_incl, eq_f = prefix_count(eq)
    sel = gt | (eq & ((eq_incl - eq_f) < need))
    sel_incl, sel_f = prefix_count(sel)
    slot_ref[...] = jnp.where(sel, sel_incl - sel_f, -1.0).astype(jnp.int32)


def _route(aff_t, cap):
    batch, ne, seq_len = aff_t.shape
    blk = pl.BlockSpec((batch * ne, seq_len), lambda i: (0, 0))
    slots = pl.pallas_call(
        functools.partial(_route_kernel, cap=cap),
        grid=(1,),
        in_specs=[blk],
        out_specs=blk,
        out_shape=jax.ShapeDtypeStruct((batch * ne, seq_len), jnp.int32),
        compiler_params=_cparams(("arbitrary",)),
        name="route",
    )(aff_t.reshape(batch * ne, seq_len))
    return slots.reshape(batch, ne, seq_len)


def _gather_kernel(slot_ref, aff_ref, h_ref, xs_ref, gate_ref, *, cap):
    hb = h_ref[...].astype(BF16)
    ne, seq_len = slot_ref.shape
    rid = lax.broadcasted_iota(jnp.int32, (cap, seq_len), 0)
    for e in range(ne):
        pick = slot_ref[e:e + 1, :] == rid
        xs_ref[e] = _dot(jnp.where(pick, 1.0, 0.0).astype(BF16), hb).astype(BF16)
        gate_ref[e] = jnp.sum(jnp.where(pick, aff_ref[e:e + 1, :], 0.0), axis=-1, keepdims=True)


def _gather(slots, aff_t, h2, cap):
    batch, ne, seq_len = slots.shape
    d = h2.shape[1]
    blk = pl.BlockSpec((None, ne, seq_len), lambda b: (b, 0, 0))
    return pl.pallas_call(
        functools.partial(_gather_kernel, cap=cap),
        grid=(batch,),
        in_specs=[blk, blk, pl.BlockSpec((seq_len, d), lambda b: (b, 0))],
        out_specs=[pl.BlockSpec((ne, None, cap, d), lambda b: (0, b, 0, 0)),
                   pl.BlockSpec((ne, None, cap, 1), lambda b: (0, b, 0, 0))],
        out_shape=[jax.ShapeDtypeStruct((ne, batch, cap, d), BF16), jax.ShapeDtypeStruct((ne, batch, cap, 1), F32)],
        compiler_params=_cparams(("parallel",)),
        name="gather",
    )(slots, aff_t, h2)


def _ffn_kernel(xs_ref, gate_ref, wg_hbm, wu_hbm, wd_hbm, y_ref, acc_ref, g_buf, u_buf, a_buf,
                wg_buf, wu_buf, wd_buf, sems, *, n_chunks, fc):
    e = pl.program_id(0)
    parts = g_buf.shape[0]
    pr = xs_ref.shape[0] // parts
    rows = lambda r: slice(r * pr, (r + 1) * pr)

    def weight_copies(expert, c, slot):
        cols = pl.ds(c * fc, fc)
        return (pltpu.make_async_copy(wg_hbm.at[expert, :, cols], wg_buf.at[slot], sems.at[slot, 0]),
                pltpu.make_async_copy(wu_hbm.at[expert, :, cols], wu_buf.at[slot], sems.at[slot, 1]),
                pltpu.make_async_copy(wd_hbm.at[expert, cols, :], wd_buf.at[slot], sems.at[slot, 2]))

    def start(expert, c, slot):
        for copy in weight_copies(expert, c, slot):
            copy.start()

    def chunk(c, first, final):
        slot = c % 2
        for copy in weight_copies(e, c, slot):
            copy.wait()
        if not final:
            start(e, c + 1, 1 - slot)
        else:
            pl.when(e + 1 < pl.num_programs(0))(lambda: start(e + 1, 0, 1 - slot))
        wg, wu, wd = wg_buf[slot].astype(BF16), wu_buf[slot].astype(BF16), wd_buf[slot].astype(BF16)

        def gate_up(r):
            xs = xs_ref[rows(r), :]
            g_buf[r] = _dot(xs, wg)
            u_buf[r] = _dot(xs, wu)

        def down(r):
            a_buf[r] = (_silu(g_buf[r]) * u_buf[r]).astype(BF16)
            part = _dot(a_buf[r], wd)
            total = part if first else acc_ref[rows(r), :] + part
            if final:
                y_ref[rows(r), :] = (total * gate_ref[rows(r), :]).astype(BF16)
            else:
                acc_ref[rows(r), :] = total

        gate_up(0)
        for r in range(parts):
            if r + 1 < parts:
                gate_up(r + 1)
            down(r)

    pl.when(e == 0)(lambda: start(0, 0, 0))
    for c in range(n_chunks):
        chunk(c, c == 0, c == n_chunks - 1)


def _ffn(xs, gates, w_gate, w_up, w_down, fc):
    ne, rows, d = xs.shape
    dff = w_gate.shape[2]
    n_chunks = dff // fc
    assert n_chunks % 2 == 0
    parts = 2 if rows % 32 == 0 else 1
    hbm = pl.BlockSpec(memory_space=pl.ANY)
    return pl.pallas_call(
        functools.partial(_ffn_kernel, n_chunks=n_chunks, fc=fc),
        grid=(ne,),
        in_specs=[pl.BlockSpec((None, rows, d), lambda e: (e, 0, 0)),
                  pl.BlockSpec((None, rows, 1), lambda e: (e, 0, 0)), hbm, hbm, hbm],
        out_specs=pl.BlockSpec((None, rows, d), lambda e: (e, 0, 0)),
        out_shape=jax.ShapeDtypeStruct((ne, rows, d), BF16),
        scratch_shapes=[pltpu.VMEM((rows, d), F32), pltpu.VMEM((parts, rows // parts, fc), F32),
                        pltpu.VMEM((parts, rows // parts, fc), F32), pltpu.VMEM((parts, rows // parts, fc), BF16),
                        pltpu.VMEM((2, d, fc), F32), pltpu.VMEM((2, d, fc), F32), pltpu.VMEM((2, fc, d), F32),
                        pltpu.SemaphoreType.DMA((2, 3))],
        compiler_params=_cparams(("arbitrary",)),
        name="ffn",
    )(xs, gates, w_gate, w_up, w_down)


def _combine_kernel(slot_ref, y_ref, h_ref, g_ref, b_ref, o_ref, moe_buf, *, cap, parts):
    ne, tl = slot_ref.shape
    pr = tl // parts
    rows = lambda r: slice(r * pr, (r + 1) * pr)
    slot_t = slot_ref[...].astype(F32).T
    lane = lax.broadcasted_iota(jnp.int32, (pr, cap), 1).astype(F32)
    y = y_ref[...].reshape(ne * cap, y_ref.shape[-1])

    def scatter(r):
        place = jnp.concatenate(
            [jnp.where(slot_t[rows(r), e:e + 1] == lane, 1.0, 0.0).astype(BF16) for e in range(ne)], axis=1)
        moe_buf[r % 2] = _dot(place, y)

    def normalize(r):
        o_ref[rows(r), :] = _layer_norm(DEEPNORM_ALPHA * h_ref[rows(r), :] + moe_buf[r % 2], g_ref[...], b_ref[...])

    scatter(0)
    for r in range(parts):
        if r + 1 < parts:
            scatter(r + 1)
        normalize(r)


def _combine(slots, y, h2, g, b, cap, tl):
    batch, ne, seq_len = slots.shape
    d = h2.shape[1]
    tiles = seq_len // tl
    parts = 4 if tl % 64 == 0 else 1
    row = pl.BlockSpec((tl, d), lambda bi, i: (bi * tiles + i, 0))
    const = pl.BlockSpec((1, d), lambda bi, i: (0, 0))
    return pl.pallas_call(
        functools.partial(_combine_kernel, cap=cap, parts=parts),
        grid=(batch, tiles),
        in_specs=[pl.BlockSpec((None, ne, tl), lambda bi, i: (bi, 0, i)),
                  pl.BlockSpec((ne, None, cap, d), lambda bi, i: (0, bi, 0, 0)),
                  row, const, const],
        out_specs=row,
        out_shape=jax.ShapeDtypeStruct((batch * seq_len, d), F32),
        scratch_shapes=[pltpu.VMEM((2, tl // parts, d), F32)],
        compiler_params=_cparams(("parallel", "parallel")),
        name="combine",
    )(slots, y, h2, g, b)


def kernel(x, mem, emb_ln_g, emb_ln_b, w_in, hg_lb_logits, hg_norm_g, da_lambda_q1, da_lambda_k1, da_lambda_q2,
           da_lambda_k2, da_subln_g, w_mix_out, ln1_g, ln1_b, xa_wq, xa_wk, xa_wv, xa_wo, ln2_g, ln2_b, w_router,
           w_gate, w_up, w_down, ln3_g, ln3_b):
    batch, seq_len, d = x.shape
    assert w_in.shape[0] == DEPTH and seq_len % HG_CHUNK == 0 and seq_len % LANES_V7X == 0
    n = batch * seq_len
    cap = EC_FACTOR * seq_len // N_EXPERTS
    vec = lambda a: a.reshape(1, -1)
    bf = lambda a: a.astype(BF16)

    t = _tiles(seq_len)

    h0, q, v, gt, ff, fb, dq, dk, dv = _inproj(
        x.reshape(n, d), vec(emb_ln_g), vec(emb_ln_b), bf(w_in[0]), seq_len, t.inproj_rows)
    hg = _hgrn(q, v, gt, ff, fb, hg_lb_logits, hg_norm_g[0], batch, seq_len)
    lam_vecs = jnp.stack([da_lambda_q1[0], da_lambda_k1[0], da_lambda_q2[0], da_lambda_k2[0]])
    da = _dattn(dq, dk, dv, lam_vecs, da_subln_g[0], batch, seq_len, t.dattn_queries, t.dattn_keys)
    kx, vx = _kvproj(mem, bf(xa_wk[0]), bf(xa_wv[0]))
    h2, aff_t = _mixattn(hg.reshape(n, -1), da.reshape(n, -1), h0, bf(w_mix_out[0]), ln1_g, ln1_b, kx, vx,
                         bf(xa_wq[0]), bf(xa_wo[0]), ln2_g, ln2_b, bf(w_router[0].T), batch, seq_len, t.mixattn_rows)

    slots = _route(aff_t, cap)
    xs, gates = _gather(slots, aff_t, h2, cap)
    y = _ffn(xs.reshape(N_EXPERTS, batch * cap, d), gates.reshape(N_EXPERTS, batch * cap, 1),
             w_gate[0], w_up[0], w_down[0], t.ffn_cols)
    out = _combine(slots, y.reshape(N_EXPERTS, batch, cap, d), h2, ln3_g, ln3_b, cap, t.combine_rows)
    return out.reshape(batch, seq_len, d)
```

```python
import functools
import math
from typing import NamedTuple

import jax
import jax.numpy as jnp
from jax import lax
from jax.experimental import pallas as pl
from jax.experimental.pallas import tpu as pltpu

F32 = jnp.float32
BF16 = jnp.bfloat16

HG_HEADS = 4
HG_DIM = 128
HG_CHUNK = 64
DA_HEADS = 4
DA_QK_DIM = 64
DA_V_DIM = 128
ROPE_DIM = DA_QK_DIM // 4
ROPE_THETA = 500000.0
XA_HEADS = 4
N_EXPERTS = 16
EC_FACTOR = 2
LN_EPS = 1e-5
RMS_EPS = 1e-6
DEPTH = 1
DEEPNORM_ALPHA = (2.0 * DEPTH) ** 0.25
LAMBDA_INIT = 0.8 - 0.6 * math.exp(-0.3 * 0)
LOG2_E = math.log2(math.e)

LANES_V7X = 128
BF16_SUBLANES_V7X = 16
VMEM_LIMIT_V7X = 56 * 1024 * 1024


class _Tiles(NamedTuple):
    inproj_rows: int
    dattn_queries: int
    dattn_keys: int
    mixattn_rows: int
    ffn_cols: int
    combine_rows: int


def _tiles(seq_len):
    fit = lambda rows: min(rows, seq_len)
    return _Tiles(inproj_rows=fit(1024), dattn_queries=fit(2048), dattn_keys=fit(512), mixattn_rows=fit(1024),
                  ffn_cols=512, combine_rows=fit(1024))


def _cparams(sem):
    return pltpu.CompilerParams(dimension_semantics=sem, vmem_limit_bytes=VMEM_LIMIT_V7X)


def _layer_norm(x, g, b):
    mu = jnp.mean(x, axis=-1, keepdims=True)
    xc = x - mu
    var = jnp.mean(xc * xc, axis=-1, keepdims=True)
    return xc * lax.rsqrt(var + LN_EPS) * g + b


def _silu(x):
    return x * jax.nn.sigmoid(x)


def _dot(a, b):
    return jnp.dot(a, b, preferred_element_type=F32)


def _dot_nt(a, b):
    return lax.dot_general(a, b, (((1,), (1,)), ((), ())), preferred_element_type=F32)


def _dot_tn(a, b):
    return lax.dot_general(a, b, (((0,), (0,)), ((), ())), preferred_element_type=F32)


def _inproj_kernel(x_ref, g_ref, b_ref, w_ref, c_ref, sa_ref, sb_ref, lbl_ref,
                   h0_ref, q_ref, v_ref, gt_ref, ff_ref, fb_ref, dq_ref, dk_ref, dv_ref, hb_buf):
    width = q_ref.shape[1]
    parts = hb_buf.shape[0]
    pr = x_ref.shape[0] // parts
    rows = lambda r: slice(r * pr, (r + 1) * pr)
    widen = lambda ref, r: jnp.concatenate([ref[rows(r), :]] * (width // ref.shape[1]), axis=1)
    lbl = lbl_ref[...]

    def forget_gate(u, d):
        l0, l1 = lbl[2 * d:2 * d + 1], lbl[2 * d + 1:2 * d + 2]
        m = jnp.maximum(l0, l1)
        e0, e1 = jnp.exp(l0 - m), jnp.exp(l1 - m)
        lb = e0 / (e0 + e1)
        return lb + (1.0 - lb) * jax.nn.sigmoid(u)

    def normalize(r):
        h = _layer_norm(x_ref[rows(r), :], g_ref[...], b_ref[...])
        h0_ref[rows(r), :] = h
        hb_buf[r] = h.astype(BF16)

    def project(r, columns):
        proj = lambda c: _dot(hb_buf[r], w_ref[:, c * width:(c + 1) * width])
        rope_c, rope_sa, rope_sb = widen(c_ref, r), widen(sa_ref, r), widen(sb_ref, r)

        def rope(t):
            return (t * rope_c + pltpu.roll(t, width - ROPE_DIM // 2, 1) * rope_sa
                    + pltpu.roll(t, ROPE_DIM // 2, 1) * rope_sb)

        for c in columns:
            u = proj(c)
            if c == 0:
                q_ref[rows(r), :] = _silu(u).astype(BF16)
            elif c == 1:
                v_ref[rows(r), :] = u.astype(BF16)
            elif c == 2:
                gt_ref[rows(r), :] = _silu(u).astype(BF16)
            elif c == 3:
                ff_ref[rows(r), :] = forget_gate(u, 0)
            elif c == 4:
                fb_ref[rows(r), :] = forget_gate(u, 1)
            elif c == 5:
                dq_ref[rows(r), :] = (rope(u) * (DA_QK_DIM ** -0.5 * LOG2_E)).astype(BF16)
            elif c == 6:
                dk_ref[rows(r), :] = rope(u).astype(BF16)
            else:
                dv_ref[:, rows(r)] = u.T.astype(BF16)

    normalize(0)
    for r in range(parts):
        project(r, range(0, 2))
        if r + 1 < parts:
            normalize(r + 1)
        project(r, range(2, 8))


def _rope_lane_tables(seq_len, width):
    half = ROPE_DIM // 2
    inv = 1.0 / (ROPE_THETA ** (jnp.arange(0, ROPE_DIM, 2, dtype=F32) / ROPE_DIM))
    ang = jnp.arange(seq_len, dtype=F32)[:, None] * inv[None, :]
    cos, sin = jnp.cos(ang), jnp.sin(ang)
    pad1 = jnp.ones((seq_len, DA_QK_DIM - ROPE_DIM), F32)
    pad0 = jnp.zeros((seq_len, DA_QK_DIM - ROPE_DIM), F32)
    z = jnp.zeros((seq_len, half), F32)
    reps = width // DA_QK_DIM
    c = jnp.tile(jnp.concatenate([cos, cos, pad1], -1), (1, reps))
    sa = jnp.tile(jnp.concatenate([-sin, z, pad0], -1), (1, reps))
    sb = jnp.tile(jnp.concatenate([z, sin, pad0], -1), (1, reps))
    return c, sa, sb


def _inproj(x2, g, b, w_bf, lb_logits, seq_len, tm):
    n, d = x2.shape
    width = w_bf.shape[1] // 8
    c, sa, sb = _rope_lane_tables(seq_len, LANES_V7X)
    tpb = seq_len // tm
    parts = 4 if tm % 512 == 0 else 1
    row = lambda p, bi: (bi * tpb + p, 0)
    const = lambda p, bi: (0, 0)
    tab = lambda p, bi: (p, 0)
    wide = lambda dt: jax.ShapeDtypeStruct((n, width), dt)
    return pl.pallas_call(
        _inproj_kernel,
        grid=(tpb, n // seq_len),
        in_specs=[pl.BlockSpec((tm, d), row), pl.BlockSpec((1, d), const), pl.BlockSpec((1, d), const),
                  pl.BlockSpec(w_bf.shape, const),
                  pl.BlockSpec((tm, LANES_V7X), tab), pl.BlockSpec((tm, LANES_V7X), tab),
                  pl.BlockSpec((tm, LANES_V7X), tab), pl.BlockSpec((4, width), const)],
        out_specs=([pl.BlockSpec((tm, d), row)] + [pl.BlockSpec((tm, width), row)] * 7
                   + [pl.BlockSpec((width, tm), lambda p, bi: (0, bi * tpb + p))]),
        out_shape=[jax.ShapeDtypeStruct((n, d), F32), wide(BF16), wide(BF16), wide(BF16), wide(F32), wide(F32),
                   wide(BF16), wide(BF16), jax.ShapeDtypeStruct((width, n), BF16)],
        scratch_shapes=[pltpu.VMEM((parts, tm // parts, d), BF16)],
        compiler_params=_cparams(("parallel", "parallel")),
        name="inproj",
    )(x2, g, b, w_bf, c, sa, sb, lb_logits.reshape(4, width))


def _hgrn_kernel(q_ref, v_ref, gt_ref, ff_ref, fb_ref, ng_ref, o_ref, st_ref, oacc_ref):
    seq_len, width = q_ref.shape
    heads = width // HG_DIM
    ch = HG_CHUNK
    blk = min(2 * ch, seq_len)
    cpb = blk // ch
    nblk = seq_len // blk
    sub_rows = 8
    tiles_per_chunk = ch // sub_rows
    st_ref[...] = jnp.zeros_like(st_ref)
    sub = lax.broadcasted_iota(jnp.int32, (blk // sub_rows, sub_rows, width), 1)
    ri = lax.broadcasted_iota(jnp.int32, (blk, blk), 0)
    ci = lax.broadcasted_iota(jnp.int32, (blk, blk), 1)
    same_chunk = (ri // ch) == (ci // ch)
    masks = (same_chunk & (ri >= ci), same_chunk & (ri <= ci))
    row_chunk = (lax.broadcasted_iota(jnp.int32, (blk, HG_DIM), 0) // ch).astype(BF16)
    in_chunk = [row_chunk == j for j in range(cpb)]
    sl = lambda hh: slice(hh * HG_DIM, (hh + 1) * HG_DIM)

    def cumprod(x, backward):
        x3 = x.reshape(blk // sub_rows, sub_rows, width)
        s = 1
        while s < sub_rows:
            if backward:
                x3 = x3 * jnp.where(sub < sub_rows - s, pltpu.roll(x3, sub_rows - s, 1), 1.0)
            else:
                x3 = x3 * jnp.where(sub >= s, pltpu.roll(x3, s, 1), 1.0)
            s *= 2
        edge = 0 if backward else sub_rows - 1
        out = [None] * (blk // sub_rows)
        for c in range(cpb):
            carry = None
            order = range(tiles_per_chunk - 1, -1, -1) if backward else range(tiles_per_chunk)
            for t in order:
                idx = c * tiles_per_chunk + t
                out[idx] = x3[idx] if carry is None else x3[idx] * carry
                carry = out[idx][edge:edge + 1]
        return jnp.concatenate(out, axis=0)

    class Item:
        pass

    def prep(d, i):
        it = Item()
        it.d = d
        bi = i if d == 0 else nblk - 1 - i
        it.rows = pl.ds(pl.multiple_of(bi * blk, blk), blk)
        f = (ff_ref if d == 0 else fb_ref)[it.rows, :]
        g = cumprod(f, d == 1)
        edge = ch - 1 if d == 0 else 0
        it.decays = [g[j * ch + edge:j * ch + edge + 1] for j in range(cpb)]
        kin = (1.0 - f) / g
        it.k_out = (kin * jnp.concatenate([jnp.broadcast_to(dj, (ch, width)) for dj in it.decays], axis=0)
                    ).astype(BF16)
        it.q_in = (q_ref[it.rows, :].astype(F32) * g).astype(BF16)
        it.k_in = kin.astype(BF16)
        it.vv = v_ref[it.rows, :]
        return it

    def scores(it):
        it.scores = [_dot_nt(it.q_in[:, sl(hh)], it.k_in[:, sl(hh)]) for hh in range(heads)]
        it.incs = []
        for hh in range(heads):
            k_diag = jnp.concatenate([jnp.where(m, it.k_out[:, sl(hh)], jnp.zeros((), BF16)) for m in in_chunk],
                                     axis=1)
            it.incs.append(_dot_tn(it.vv[:, sl(hh)], k_diag))

    def intra(it):
        it.intra = [_dot(jnp.where(masks[it.d], it.scores[hh], 0.0).astype(BF16), it.vv[:, sl(hh)])
                    for hh in range(heads)]

    def scan(it):
        d = it.d
        st = [st_ref[d * heads + hh] for hh in range(heads)]
        it.inter = [[None] * cpb for _ in range(heads)]
        for step in range(cpb):
            j = step if d == 0 else cpb - 1 - step
            for hh in range(heads):
                it.inter[hh][j] = _dot_nt(it.q_in[j * ch:(j + 1) * ch, sl(hh)], st[hh].astype(BF16))
            for hh in range(heads):
                st[hh] = st[hh] * it.decays[j][:, sl(hh)] + it.incs[hh][:, j * HG_DIM:(j + 1) * HG_DIM]
        for hh in range(heads):
            st_ref[d * heads + hh] = st[hh]

    def write(it, completes):
        ng = ng_ref[...]
        for hh in range(heads):
            o = it.intra[hh] + jnp.concatenate(it.inter[hh], axis=0)
            if not completes:
                oacc_ref[it.rows, sl(hh)] = o
                continue
            o = o + oacc_ref[it.rows, sl(hh)]
            ms = jnp.mean(o * o, axis=-1, keepdims=True)
            gate = gt_ref[it.rows, sl(hh)].astype(F32)
            o_ref[it.rows, sl(hh)] = (o * lax.rsqrt(ms + RMS_EPS) * ng[:, sl(hh)] * gate).astype(BF16)

    def body(i, fwd_completes, bwd_completes):
        fwd = prep(0, i)
        scores(fwd)
        bwd = prep(1, i)
        intra(fwd)
        scores(bwd)
        scan(fwd)
        intra(bwd)
        write(fwd, fwd_completes)
        scan(bwd)
        write(bwd, bwd_completes)

    half = nblk // 2

    def loop(lo, hi, completes):
        trips = hi - lo
        if trips > 0:
            lax.fori_loop(lo, hi, lambda i, c: (body(i, completes, completes), c)[1], 0,
                          unroll=8 if trips % 8 == 0 else 1)

    loop(0, half, False)
    if nblk % 2:
        body(half, False, True)
    loop(nblk - half, nblk, True)


def _hgrn(q, v, gt, ff, fb, norm_g, batch, seq_len):
    width = q.shape[1]
    hw = width
    sp = lambda a: a.reshape(batch, seq_len, width)
    blk = pl.BlockSpec((None, seq_len, hw), lambda b, h: (b, 0, h))
    return pl.pallas_call(
        _hgrn_kernel,
        grid=(batch, width // hw),
        in_specs=[blk, blk, blk, blk, blk, pl.BlockSpec((1, hw), lambda b, h: (0, h))],
        out_specs=blk,
        out_shape=jax.ShapeDtypeStruct((batch, seq_len, width), BF16),
        scratch_shapes=[pltpu.VMEM((2 * (hw // HG_DIM), HG_DIM, HG_DIM), F32), pltpu.VMEM((seq_len, hw), F32)],
        compiler_params=_cparams(("parallel", "parallel")),
        name="hgrn",
    )(sp(q), sp(v), sp(gt), sp(ff), sp(fb), norm_g.reshape(1, width))


def _dattn_kernel(q_ref, k_ref, vt_ref, lam_ref, sg_ref, o_ref, s_buf, e_buf, acc_buf, *, kc):
    lv = lam_ref[...]
    lam = (jnp.exp(jnp.sum(lv[0:1] * lv[1:2], axis=-1, keepdims=True))
           - jnp.exp(jnp.sum(lv[2:3] * lv[3:4], axis=-1, keepdims=True)) + LAMBDA_INIT)
    q = q_ref[...]
    lane = lax.broadcasted_iota(jnp.int32, q.shape, 1)
    zero = jnp.zeros_like(q)
    qs = (jnp.where(lane < DA_QK_DIM, q, zero), jnp.where(lane >= DA_QK_DIM, q, zero))
    items = [(j, c) for j in range(k_ref.shape[0] // kc) for c in (0, 1)]
    nbuf = s_buf.shape[0]
    dv = vt_ref.shape[0]
    ones = jnp.ones((acc_buf.shape[1] - dv, kc), BF16)
    m, alpha = [None, None], {}

    def scores(i):
        j, c = items[i]
        s_buf[i % nbuf] = _dot_nt(k_ref[j * kc:(j + 1) * kc, :], qs[c])

    def exponentials(i):
        j, c = items[i]
        s = s_buf[i % nbuf]
        mj = jnp.max(s, axis=0, keepdims=True)
        if j == 0:
            m[c], alpha[i] = mj, None
        else:
            mn = jnp.maximum(m[c], mj)
            alpha[i] = jnp.exp2(m[c] - mn)
            m[c] = mn
        e_buf[i % nbuf] = jnp.exp2(s - m[c]).astype(BF16)

    def values(i):
        j, c = items[i]
        pv = _dot(jnp.concatenate([vt_ref[:, j * kc:(j + 1) * kc], ones], axis=0), e_buf[i % nbuf])
        acc_buf[c] = pv if alpha[i] is None else alpha[i] * acc_buf[c] + pv

    scores(0)
    scores(1)
    exponentials(0)
    for i in range(len(items)):
        if i + 2 < len(items):
            scores(i + 2)
        if i + 1 < len(items):
            exponentials(i + 1)
        values(i)
    a0, a1 = acc_buf[0], acc_buf[1]
    ot = a0[:dv] * (1.0 / a0[dv:dv + 1]) - a1[:dv] * (lam / a1[dv:dv + 1])
    ms = jnp.mean(ot * ot, axis=0, keepdims=True)
    y = ot * lax.rsqrt(ms + RMS_EPS) * (sg_ref[...] * (1.0 - LAMBDA_INIT))
    o_ref[...] = y.T.astype(BF16)


def _dattn(dq, dk, dvt, lam_vecs, subln_g, batch, seq_len, tq, kc):
    width = dq.shape[1]
    hw = DA_V_DIM
    sp = lambda a: a.reshape(batch, seq_len, width)
    kk = pl.BlockSpec((None, seq_len, hw), lambda b, h, i: (b, 0, h))
    qo = pl.BlockSpec((None, tq, hw), lambda b, h, i: (b, i, h))
    nbuf = 4
    return pl.pallas_call(
        functools.partial(_dattn_kernel, kc=kc),
        grid=(batch, width // hw, seq_len // tq),
        in_specs=[qo, kk, pl.BlockSpec((hw, seq_len), lambda b, h, i: (h, b)),
                  pl.BlockSpec(lam_vecs.shape, lambda b, h, i: (0, 0)),
                  pl.BlockSpec((hw, 1), lambda b, h, i: (0, 0))],
        out_specs=qo,
        out_shape=jax.ShapeDtypeStruct((batch, seq_len, width), BF16),
        scratch_shapes=[pltpu.VMEM((nbuf, kc, tq), F32), pltpu.VMEM((nbuf, kc, tq), BF16),
                        pltpu.VMEM((2, hw + BF16_SUBLANES_V7X, tq), F32)],
        compiler_params=_cparams(("parallel", "parallel", "parallel")),
        name="dattn",
    )(sp(dq), sp(dk), dvt, lam_vecs, subln_g.reshape(hw, 1))


def _kvproj_kernel(m_ref, wk_ref, wv_ref, k_ref, v_ref):
    mb = m_ref[...].astype(BF16)
    k_ref[...] = _dot(mb, wk_ref[...]).astype(BF16)
    v_ref[...] = _dot(mb, wv_ref[...]).astype(BF16)


def _kvproj(mem, wk_bf, wv_bf):
    batch, m, d = mem.shape
    blk = pl.BlockSpec((None, m, d), lambda b: (b, 0, 0))
    const = pl.BlockSpec((d, d), lambda b: (0, 0))
    return pl.pallas_call(
        _kvproj_kernel,
        grid=(batch,),
        in_specs=[blk, const, const],
        out_specs=[blk, blk],
        out_shape=[jax.ShapeDtypeStruct((batch, m, d), BF16)] * 2,
        compiler_params=_cparams(("parallel",)),
        name="kvproj",
    )(mem, wk_bf, wv_bf)


def _mixattn_kernel(hg_ref, da_ref, h0_ref, wm_ref, g1_ref, b1_ref, k_ref, v_ref, wq_ref, wo_ref, g2_ref, b2_ref,
                    wr_ref, h2_ref, aff_ref, mix_buf, h1_buf, *, parts):
    half = hg_ref.shape[1]
    pr = hg_ref.shape[0] // parts
    rows = lambda r: slice(r * pr, (r + 1) * pr)

    def project(r):
        mix_buf[r % 2] = _dot(hg_ref[rows(r), :], wm_ref[:half, :]) + _dot(da_ref[rows(r), :], wm_ref[half:, :])

    def normalize(r):
        h1_buf[rows(r), :] = _layer_norm(DEEPNORM_ALPHA * h0_ref[rows(r), :] + mix_buf[r % 2],
                                         g1_ref[...], b1_ref[...])

    project(0)
    for r in range(parts):
        if r + 1 < parts:
            project(r + 1)
        normalize(r)

    h1 = h1_buf[...]
    d = h1.shape[1]
    hd = d // XA_HEADS
    q = (_dot(h1.astype(BF16), wq_ref[...]) * (hd ** -0.5)).astype(BF16)
    outs = []
    for i in range(XA_HEADS):
        sl = slice(i * hd, (i + 1) * hd)
        s = _dot_nt(q[:, sl], k_ref[:, sl])
        e = jnp.exp(s - jnp.max(s, axis=-1, keepdims=True))
        inv_l = 1.0 / jnp.sum(e, axis=-1, keepdims=True)
        outs.append((_dot(e.astype(BF16), v_ref[:, sl]) * inv_l).astype(BF16))
    xa = _dot(jnp.concatenate(outs, axis=-1), wo_ref[...])
    h2 = _layer_norm(DEEPNORM_ALPHA * h1 + xa, g2_ref[...], b2_ref[...])
    h2_ref[...] = h2
    logits = _dot_nt(wr_ref[...], h2.astype(BF16))
    e = jnp.exp(logits - jnp.max(logits, axis=0, keepdims=True))
    aff_ref[...] = e / jnp.sum(e, axis=0, keepdims=True)


def _mixattn(hg, da, h0, wm_bf, g1, b1, kx, vx, wq_bf, wo_bf, g2, b2, wr_t_bf, batch, seq_len, tm):
    n, d = h0.shape
    half = hg.shape[1]
    m = kx.shape[1]
    ne = wr_t_bf.shape[0]
    tiles = seq_len // tm
    parts = 4 if tm % 64 == 0 else 1
    rows = lambda w: pl.BlockSpec((tm, w), lambda bi, i: (bi * tiles + i, 0))
    kv = pl.BlockSpec((None, m, d), lambda bi, i: (bi, 0, 0))
    const = lambda shape: pl.BlockSpec(shape, lambda bi, i: (0, 0))
    return pl.pallas_call(
        functools.partial(_mixattn_kernel, parts=parts),
        grid=(batch, tiles),
        in_specs=[rows(half), rows(half), rows(d), const((d, d)), const((1, d)), const((1, d)), kv, kv,
                  const((d, d)), const((d, d)), const((1, d)), const((1, d)), const((ne, d))],
        out_specs=[rows(d), pl.BlockSpec((None, ne, tm), lambda bi, i: (bi, 0, i))],
        out_shape=[jax.ShapeDtypeStruct((n, d), F32), jax.ShapeDtypeStruct((batch, ne, seq_len), F32)],
        scratch_shapes=[pltpu.VMEM((2, tm // parts, d), F32), pltpu.VMEM((tm, d), F32)],
        compiler_params=_cparams(("parallel", "parallel")),
        name="mixattn",
    )(hg, da, h0, wm_bf, g1, b1, kx, vx, wq_bf, wo_bf, g2, b2, wr_t_bf)


def _route_kernel(aff_ref, slot_ref, *, cap):
    aff = aff_ref[...]
    ne, seq_len = aff.shape
    nblk = seq_len // LANES_V7X

    def count(mask):
        return jnp.sum(jnp.where(mask, 1.0, 0.0), axis=-1, keepdims=True)

    bits = jnp.zeros((ne, 1), jnp.int32)
    for bit in range(30, -1, -1):
        cand = bits | (1 << bit)
        bits = jnp.where(count(aff >= pltpu.bitcast(cand, F32)) >= cap, cand, bits)
    thr = pltpu.bitcast(bits, F32)

    ri = lax.broadcasted_iota(jnp.int32, (LANES_V7X, LANES_V7X), 0)
    ci = lax.broadcasted_iota(jnp.int32, (LANES_V7X, LANES_V7X), 1)
    tri = jnp.where(ri <= ci, 1.0, 0.0).astype(BF16)

    def prefix_count(mask):
        m = jnp.where(mask, 1.0, 0.0)
        stack = jnp.concatenate([m[:, j * LANES_V7X:(j + 1) * LANES_V7X] for j in range(nblk)], axis=0)
        within = _dot(stack.astype(BF16), tri)
        pieces, carry = [], jnp.zeros((ne, 1), F32)
        for j in range(nblk):
            blk = within[j * ne:(j + 1) * ne]
            pieces.append(blk + carry)
            carry = carry + blk[:, LANES_V7X - 1:LANES_V7X]
        return jnp.concatenate(pieces, axis=1), m

    gt = aff > thr
    eq = aff == thr
    need = cap - count(gt)
    eq_incl, eq_f = prefix_count(eq)
    sel = gt | (eq & ((eq_incl - eq_f) < need))
    sel_incl, sel_f = prefix_count(sel)
    slot_ref[...] = jnp.where(sel, sel_incl - sel_f, -1.0).astype(jnp.int32)


def _route(aff_t, cap):
    batch, ne, seq_len = aff_t.shape
    blk = pl.BlockSpec((batch * ne, seq_len), lambda i: (0, 0))
    slots = pl.pallas_call(
        functools.partial(_route_kernel, cap=cap),
        grid=(1,),
        in_specs=[blk],
        out_specs=blk,
        out_shape=jax.ShapeDtypeStruct((batch * ne, seq_len), jnp.int32),
        compiler_params=_cparams(("arbitrary",)),
        name="route",
    )(aff_t.reshape(batch * ne, seq_len))
    return slots.reshape(batch, ne, seq_len)


def _gather_kernel(slot_ref, aff_ref, h_ref, xs_ref, gate_ref, *, cap):
    hb = h_ref[...].astype(BF16)
    ne, seq_len = slot_ref.shape
    rid = lax.broadcasted_iota(jnp.int32, (cap, seq_len), 0)
    for e in range(ne):
        pick = slot_ref[e:e + 1, :] == rid
        xs_ref[e] = _dot(jnp.where(pick, 1.0, 0.0).astype(BF16), hb).astype(BF16)
        gate_ref[e] = jnp.sum(jnp.where(pick, aff_ref[e:e + 1, :], 0.0), axis=-1, keepdims=True)


def _gather(slots, aff_t, h2, cap):
    batch, ne, seq_len = slots.shape
    d = h2.shape[1]
    blk = pl.BlockSpec((None, ne, seq_len), lambda b: (b, 0, 0))
    return pl.pallas_call(
        functools.partial(_gather_kernel, cap=cap),
        grid=(batch,),
        in_specs=[blk, blk, pl.BlockSpec((seq_len, d), lambda b: (b, 0))],
        out_specs=[pl.BlockSpec((ne, None, cap, d), lambda b: (0, b, 0, 0)),
                   pl.BlockSpec((ne, None, cap, 1), lambda b: (0, b, 0, 0))],
        out_shape=[jax.ShapeDtypeStruct((ne, batch, cap, d), BF16), jax.ShapeDtypeStruct((ne, batch, cap, 1), F32)],
        compiler_params=_cparams(("parallel",)),
        name="gather",
    )(slots, aff_t, h2)


def _ffn_kernel(xs_ref, gate_ref, wg_ref, wu_ref, wd_ref, y_ref, acc_ref, g_buf, u_buf, a_buf, *, n_chunks):
    c = pl.program_id(1)
    parts = g_buf.shape[0]
    pr = xs_ref.shape[0] // parts
    rows = lambda r: slice(r * pr, (r + 1) * pr)

    def chunk(first, final):
        wg, wu, wd = wg_ref[...].astype(BF16), wu_ref[...].astype(BF16), wd_ref[...].astype(BF16)

        def gate_up(r):
            xs = xs_ref[rows(r), :]
            g_buf[r] = _dot(xs, wg)
            u_buf[r] = _dot(xs, wu)

        def down(r):
            a_buf[r] = (_silu(g_buf[r]) * u_buf[r]).astype(BF16)
            part = _dot(a_buf[r], wd)
            total = part if first else acc_ref[rows(r), :] + part
            if final:
                y_ref[rows(r), :] = (total * gate_ref[rows(r), :]).astype(BF16)
            else:
                acc_ref[rows(r), :] = total

        gate_up(0)
        for r in range(parts):
            if r + 1 < parts:
                gate_up(r + 1)
            down(r)

    if n_chunks == 1:
        chunk(True, True)
    else:
        pl.when(c == 0)(lambda: chunk(True, False))
        if n_chunks > 2:
            pl.when((c > 0) & (c < n_chunks - 1))(lambda: chunk(False, False))
        pl.when(c == n_chunks - 1)(lambda: chunk(False, True))


def _ffn(xs, gates, w_gate, w_up, w_down, fc):
    ne, rows, d = xs.shape
    dff = w_gate.shape[2]
    parts = 2 if rows % 32 == 0 else 1
    return pl.pallas_call(
        functools.partial(_ffn_kernel, n_chunks=dff // fc),
        grid=(ne, dff // fc),
        in_specs=[pl.BlockSpec((None, rows, d), lambda e, c: (e, 0, 0)),
                  pl.BlockSpec((None, rows, 1), lambda e, c: (e, 0, 0)),
                  pl.BlockSpec((None, d, fc), lambda e, c: (e, 0, c)),
                  pl.BlockSpec((None, d, fc), lambda e, c: (e, 0, c)),
                  pl.BlockSpec((None, fc, d), lambda e, c: (e, c, 0))],
        out_specs=pl.BlockSpec((None, rows, d), lambda e, c: (e, 0, 0)),
        out_shape=jax.ShapeDtypeStruct((ne, rows, d), BF16),
        scratch_shapes=[pltpu.VMEM((rows, d), F32), pltpu.VMEM((parts, rows // parts, fc), F32),
                        pltpu.VMEM((parts, rows // parts, fc), F32), pltpu.VMEM((parts, rows // parts, fc), BF16)],
        compiler_params=_cparams(("parallel", "arbitrary")),
        name="ffn",
    )(xs, gates, w_gate, w_up, w_down)


def _combine_kernel(slot_ref, y_ref, h_ref, g_ref, b_ref, o_ref, moe_buf, *, cap, parts):
    ne, tl = slot_ref.shape
    pr = tl // parts
    rows = lambda r: slice(r * pr, (r + 1) * pr)
    slot_t = slot_ref[...].astype(F32).T
    lane = lax.broadcasted_iota(jnp.int32, (pr, cap), 1).astype(F32)
    y = y_ref[...].reshape(ne * cap, y_ref.shape[-1])

    def scatter(r):
        place = jnp.concatenate(
            [jnp.where(slot_t[rows(r), e:e + 1] == lane, 1.0, 0.0).astype(BF16) for e in range(ne)], axis=1)
        moe_buf[r % 2] = _dot(place, y)

    def normalize(r):
        o_ref[rows(r), :] = _layer_norm(DEEPNORM_ALPHA * h_ref[rows(r), :] + moe_buf[r % 2], g_ref[...], b_ref[...])

    scatter(0)
    for r in range(parts):
        if r + 1 < parts:
            scatter(r + 1)
        normalize(r)


def _combine(slots, y, h2, g, b, cap, tl):
    batch, ne, seq_len = slots.shape
    d = h2.shape[1]
    tiles = seq_len // tl
    parts = 4 if tl % 64 == 0 else 1
    row = pl.BlockSpec((tl, d), lambda bi, i: (bi * tiles + i, 0))
    const = pl.BlockSpec((1, d), lambda bi, i: (0, 0))
    return pl.pallas_call(
        functools.partial(_combine_kernel, cap=cap, parts=parts),
        grid=(batch, tiles),
        in_specs=[pl.BlockSpec((None, ne, tl), lambda bi, i: (bi, 0, i)),
                  pl.BlockSpec((ne, None, cap, d), lambda bi, i: (0, bi, 0, 0)),
                  row, const, const],
        out_specs=row,
        out_shape=jax.ShapeDtypeStruct((batch * seq_len, d), F32),
        scratch_shapes=[pltpu.VMEM((2, tl // parts, d), F32)],
        compiler_params=_cparams(("parallel", "parallel")),
        name="combine",
    )(slots, y, h2, g, b)


def kernel(x, mem, emb_ln_g, emb_ln_b, w_in, hg_lb_logits, hg_norm_g, da_lambda_q1, da_lambda_k1, da_lambda_q2,
           da_lambda_k2, da_subln_g, w_mix_out, ln1_g, ln1_b, xa_wq, xa_wk, xa_wv, xa_wo, ln2_g, ln2_b, w_router,
           w_gate, w_up, w_down, ln3_g, ln3_b):
    batch, seq_len, d = x.shape
    assert w_in.shape[0] == DEPTH and seq_len % HG_CHUNK == 0 and seq_len % LANES_V7X == 0
    n = batch * seq_len
    cap = EC_FACTOR * seq_len // N_EXPERTS
    vec = lambda a: a.reshape(1, -1)
    bf = lambda a: a.astype(BF16)

    t = _tiles(seq_len)

    h0, q, v, gt, ff, fb, dq, dk, dv = _inproj(
        x.reshape(n, d), vec(emb_ln_g), vec(emb_ln_b), bf(w_in[0]), hg_lb_logits, seq_len, t.inproj_rows)
    hg = _hgrn(q, v, gt, ff, fb, hg_norm_g[0], batch, seq_len)
    lam_vecs = jnp.stack([da_lambda_q1[0], da_lambda_k1[0], da_lambda_q2[0], da_lambda_k2[0]])
    da = _dattn(dq, dk, dv, lam_vecs, da_subln_g[0], batch, seq_len, t.dattn_queries, t.dattn_keys)
    kx, vx = _kvproj(mem, bf(xa_wk[0]), bf(xa_wv[0]))
    h2, aff_t = _mixattn(hg.reshape(n, -1), da.reshape(n, -1), h0, bf(w_mix_out[0]), ln1_g, ln1_b, kx, vx,
                         bf(xa_wq[0]), bf(xa_wo[0]), ln2_g, ln2_b, bf(w_router[0].T), batch, seq_len, t.mixattn_rows)

    slots = _route(aff_t, cap)
    xs, gates = _gather(slots, aff_t, h2, cap)
    y = _ffn(xs.reshape(N_EXPERTS, batch * cap, d), gates.reshape(N_EXPERTS, batch * cap, 1),
             w_gate[0], w_up[0], w_down[0], t.ffn_cols)
    out = _combine(slots, y.reshape(N_EXPERTS, batch, cap, d), h2, ln3_g, ln3_b, cap, t.combine_rows)
    return out.reshape(batch, seq_len, d)
```

```python
import functools
import math
from typing import NamedTuple

import jax
import jax.numpy as jnp
from jax import lax
from jax.experimental import pallas as pl
from jax.experimental.pallas import tpu as pltpu

F32 = jnp.float32
BF16 = jnp.bfloat16

HG_HEADS = 4
HG_DIM = 128
HG_CHUNK = 64
DA_HEADS = 4
DA_QK_DIM = 64
DA_V_DIM = 128
ROPE_DIM = DA_QK_DIM // 4
ROPE_THETA = 500000.0
XA_HEADS = 4
N_EXPERTS = 16
EC_FACTOR = 2
LN_EPS = 1e-5
RMS_EPS = 1e-6
DEPTH = 1
DEEPNORM_ALPHA = (2.0 * DEPTH) ** 0.25
LAMBDA_INIT = 0.8 - 0.6 * math.exp(-0.3 * 0)
LOG2_E = math.log2(math.e)

LANES_V7X = 128
BF16_SUBLANES_V7X = 16
VMEM_LIMIT_V7X = 56 * 1024 * 1024


class _Tiles(NamedTuple):
    inproj_rows: int
    dattn_queries: int
    dattn_keys: int
    kv_cols: int
    mixattn_rows: int
    ffn_cols: int
    combine_rows: int


def _tiles(seq_len):
    fit = lambda rows: min(rows, seq_len)
    return _Tiles(inproj_rows=fit(1024), dattn_queries=fit(2048), dattn_keys=fit(512), kv_cols=256, mixattn_rows=fit(1024),
                  ffn_cols=512, combine_rows=fit(1024))


def _cparams(sem):
    return pltpu.CompilerParams(dimension_semantics=sem, vmem_limit_bytes=VMEM_LIMIT_V7X)


def _layer_norm(x, g, b):
    mu = jnp.mean(x, axis=-1, keepdims=True)
    xc = x - mu
    var = jnp.mean(xc * xc, axis=-1, keepdims=True)
    return xc * lax.rsqrt(var + LN_EPS) * g + b


def _silu(x):
    return x * jax.nn.sigmoid(x)


def _dot(a, b):
    return jnp.dot(a, b, preferred_element_type=F32)


def _dot_nt(a, b):
    return lax.dot_general(a, b, (((1,), (1,)), ((), ())), preferred_element_type=F32)


def _dot_tn(a, b):
    return lax.dot_general(a, b, (((0,), (0,)), ((), ())), preferred_element_type=F32)


def _inproj_kernel(x_ref, g_ref, b_ref, w_ref, c_ref, sa_ref, sb_ref, lbl_ref,
                   h0_ref, q_ref, v_ref, gt_ref, ff_ref, fb_ref, dq_ref, dk_ref, dv_ref, hb_buf):
    width = q_ref.shape[1]
    parts = hb_buf.shape[0]
    pr = x_ref.shape[0] // parts
    rows = lambda r: slice(r * pr, (r + 1) * pr)
    widen = lambda ref, r: jnp.concatenate([ref[rows(r), :]] * (width // ref.shape[1]), axis=1)
    lbl = lbl_ref[...]

    def forget_gate(u, d):
        l0, l1 = lbl[2 * d:2 * d + 1], lbl[2 * d + 1:2 * d + 2]
        m = jnp.maximum(l0, l1)
        e0, e1 = jnp.exp(l0 - m), jnp.exp(l1 - m)
        lb = e0 / (e0 + e1)
        return lb + (1.0 - lb) * jax.nn.sigmoid(u)

    def normalize(r):
        h = _layer_norm(x_ref[rows(r), :], g_ref[...], b_ref[...])
        h0_ref[rows(r), :] = h
        hb_buf[r] = h.astype(BF16)

    def project(r, columns):
        proj = lambda c: _dot(hb_buf[r], w_ref[:, c * width:(c + 1) * width])
        rope_c, rope_sa, rope_sb = widen(c_ref, r), widen(sa_ref, r), widen(sb_ref, r)

        def rope(t):
            return (t * rope_c + pltpu.roll(t, width - ROPE_DIM // 2, 1) * rope_sa
                    + pltpu.roll(t, ROPE_DIM // 2, 1) * rope_sb)

        for c in columns:
            u = proj(c)
            if c == 0:
                q_ref[rows(r), :] = _silu(u).astype(BF16)
            elif c == 1:
                v_ref[rows(r), :] = u.astype(BF16)
            elif c == 2:
                gt_ref[rows(r), :] = _silu(u).astype(BF16)
            elif c == 3:
                ff_ref[rows(r), :] = forget_gate(u, 0)
            elif c == 4:
                fb_ref[rows(r), :] = forget_gate(u, 1)
            elif c == 5:
                dq_ref[rows(r), :] = (rope(u) * (DA_QK_DIM ** -0.5 * LOG2_E)).astype(BF16)
            elif c == 6:
                dk_ref[rows(r), :] = rope(u).astype(BF16)
            else:
                dv_ref[:, rows(r)] = u.T.astype(BF16)

    normalize(0)
    for r in range(parts):
        project(r, range(0, 2))
        if r + 1 < parts:
            normalize(r + 1)
        project(r, range(2, 8))


def _rope_lane_tables(seq_len, width):
    half = ROPE_DIM // 2
    inv = 1.0 / (ROPE_THETA ** (jnp.arange(0, ROPE_DIM, 2, dtype=F32) / ROPE_DIM))
    ang = jnp.arange(seq_len, dtype=F32)[:, None] * inv[None, :]
    cos, sin = jnp.cos(ang), jnp.sin(ang)
    pad1 = jnp.ones((seq_len, DA_QK_DIM - ROPE_DIM), F32)
    pad0 = jnp.zeros((seq_len, DA_QK_DIM - ROPE_DIM), F32)
    z = jnp.zeros((seq_len, half), F32)
    reps = width // DA_QK_DIM
    c = jnp.tile(jnp.concatenate([cos, cos, pad1], -1), (1, reps))
    sa = jnp.tile(jnp.concatenate([-sin, z, pad0], -1), (1, reps))
    sb = jnp.tile(jnp.concatenate([z, sin, pad0], -1), (1, reps))
    return c, sa, sb


def _inproj(x2, g, b, w_bf, lb_logits, seq_len, tm):
    n, d = x2.shape
    width = w_bf.shape[1] // 8
    c, sa, sb = _rope_lane_tables(seq_len, LANES_V7X)
    tpb = seq_len // tm
    parts = 4 if tm % 512 == 0 else 1
    row = lambda p, bi: (bi * tpb + p, 0)
    const = lambda p, bi: (0, 0)
    tab = lambda p, bi: (p, 0)
    wide = lambda dt: jax.ShapeDtypeStruct((n, width), dt)
    return pl.pallas_call(
        _inproj_kernel,
        grid=(tpb, n // seq_len),
        in_specs=[pl.BlockSpec((tm, d), row), pl.BlockSpec((1, d), const), pl.BlockSpec((1, d), const),
                  pl.BlockSpec(w_bf.shape, const),
                  pl.BlockSpec((tm, LANES_V7X), tab), pl.BlockSpec((tm, LANES_V7X), tab),
                  pl.BlockSpec((tm, LANES_V7X), tab), pl.BlockSpec((4, width), const)],
        out_specs=([pl.BlockSpec((tm, d), row)] + [pl.BlockSpec((tm, width), row)] * 7
                   + [pl.BlockSpec((width, tm), lambda p, bi: (0, bi * tpb + p))]),
        out_shape=[jax.ShapeDtypeStruct((n, d), F32), wide(BF16), wide(BF16), wide(BF16), wide(F32), wide(F32),
                   wide(BF16), wide(BF16), jax.ShapeDtypeStruct((width, n), BF16)],
        scratch_shapes=[pltpu.VMEM((parts, tm // parts, d), BF16)],
        compiler_params=_cparams(("parallel", "parallel")),
        name="inproj",
    )(x2, g, b, w_bf, c, sa, sb, lb_logits.reshape(4, width))


def _hgrn_kernel(q_ref, v_ref, gt_ref, ff_ref, fb_ref, ng_ref, o_ref, st_ref, oacc_ref):
    seq_len, width = q_ref.shape
    heads = width // HG_DIM
    ch = HG_CHUNK
    blk = min(2 * ch, seq_len)
    cpb = blk // ch
    nblk = seq_len // blk
    sub_rows = 8
    tiles_per_chunk = ch // sub_rows
    st_ref[...] = jnp.zeros_like(st_ref)
    sub = lax.broadcasted_iota(jnp.int32, (blk // sub_rows, sub_rows, width), 1)
    ri = lax.broadcasted_iota(jnp.int32, (blk, blk), 0)
    ci = lax.broadcasted_iota(jnp.int32, (blk, blk), 1)
    same_chunk = (ri // ch) == (ci // ch)
    masks = (same_chunk & (ri >= ci), same_chunk & (ri <= ci))
    row_chunk = (lax.broadcasted_iota(jnp.int32, (blk, HG_DIM), 0) // ch).astype(BF16)
    in_chunk = [row_chunk == j for j in range(cpb)]
    sl = lambda hh: slice(hh * HG_DIM, (hh + 1) * HG_DIM)

    def cumprod(x, backward):
        x3 = x.reshape(blk // sub_rows, sub_rows, width)
        s = 1
        while s < sub_rows:
            if backward:
                x3 = x3 * jnp.where(sub < sub_rows - s, pltpu.roll(x3, sub_rows - s, 1), 1.0)
            else:
                x3 = x3 * jnp.where(sub >= s, pltpu.roll(x3, s, 1), 1.0)
            s *= 2
        edge = 0 if backward else sub_rows - 1
        out = [None] * (blk // sub_rows)
        for c in range(cpb):
            carry = None
            order = range(tiles_per_chunk - 1, -1, -1) if backward else range(tiles_per_chunk)
            for t in order:
                idx = c * tiles_per_chunk + t
                out[idx] = x3[idx] if carry is None else x3[idx] * carry
                carry = out[idx][edge:edge + 1]
        return jnp.concatenate(out, axis=0)

    class Item:
        pass

    def prep(d, i):
        it = Item()
        it.d = d
        bi = i if d == 0 else nblk - 1 - i
        it.rows = pl.ds(pl.multiple_of(bi * blk, blk), blk)
        f = (ff_ref if d == 0 else fb_ref)[it.rows, :]
        g = cumprod(f, d == 1)
        edge = ch - 1 if d == 0 else 0
        it.decays = [g[j * ch + edge:j * ch + edge + 1] for j in range(cpb)]
        kin = (1.0 - f) / g
        it.k_out = (kin * jnp.concatenate([jnp.broadcast_to(dj, (ch, width)) for dj in it.decays], axis=0)
                    ).astype(BF16)
        it.q_in = (q_ref[it.rows, :].astype(F32) * g).astype(BF16)
        it.k_in = kin.astype(BF16)
        it.vv = v_ref[it.rows, :]
        return it

    def scores(it):
        it.scores = [_dot_nt(it.q_in[:, sl(hh)], it.k_in[:, sl(hh)]) for hh in range(heads)]
        it.incs = []
        for hh in range(heads):
            k_diag = jnp.concatenate([jnp.where(m, it.k_out[:, sl(hh)], jnp.zeros((), BF16)) for m in in_chunk],
                                     axis=1)
            it.incs.append(_dot_tn(it.vv[:, sl(hh)], k_diag))

    def intra(it):
        it.intra = [_dot(jnp.where(masks[it.d], it.scores[hh], 0.0).astype(BF16), it.vv[:, sl(hh)])
                    for hh in range(heads)]

    def scan(it):
        d = it.d
        st = [st_ref[d * heads + hh] for hh in range(heads)]
        it.inter = [[None] * cpb for _ in range(heads)]
        for step in range(cpb):
            j = step if d == 0 else cpb - 1 - step
            for hh in range(heads):
                it.inter[hh][j] = _dot_nt(it.q_in[j * ch:(j + 1) * ch, sl(hh)], st[hh].astype(BF16))
            for hh in range(heads):
                st[hh] = st[hh] * it.decays[j][:, sl(hh)] + it.incs[hh][:, j * HG_DIM:(j + 1) * HG_DIM]
        for hh in range(heads):
            st_ref[d * heads + hh] = st[hh]

    def write(it, completes):
        ng = ng_ref[...]
        for hh in range(heads):
            o = it.intra[hh] + jnp.concatenate(it.inter[hh], axis=0)
            if not completes:
                oacc_ref[it.rows, sl(hh)] = o
                continue
            o = o + oacc_ref[it.rows, sl(hh)]
            ms = jnp.mean(o * o, axis=-1, keepdims=True)
            gate = gt_ref[it.rows, sl(hh)].astype(F32)
            o_ref[it.rows, sl(hh)] = (o * lax.rsqrt(ms + RMS_EPS) * ng[:, sl(hh)] * gate).astype(BF16)

    def body(i, fwd_completes, bwd_completes):
        fwd = prep(0, i)
        scores(fwd)
        bwd = prep(1, i)
        intra(fwd)
        scores(bwd)
        scan(fwd)
        intra(bwd)
        write(fwd, fwd_completes)
        scan(bwd)
        write(bwd, bwd_completes)

    half = nblk // 2

    def loop(lo, hi, completes):
        trips = hi - lo
        if trips > 0:
            lax.fori_loop(lo, hi, lambda i, c: (body(i, completes, completes), c)[1], 0,
                          unroll=8 if trips % 8 == 0 else 1)

    loop(0, half, False)
    if nblk % 2:
        body(half, False, True)
    loop(nblk - half, nblk, True)


def _hgrn(q, v, gt, ff, fb, norm_g, batch, seq_len):
    width = q.shape[1]
    hw = width
    sp = lambda a: a.reshape(batch, seq_len, width)
    blk = pl.BlockSpec((None, seq_len, hw), lambda b, h: (b, 0, h))
    return pl.pallas_call(
        _hgrn_kernel,
        grid=(batch, width // hw),
        in_specs=[blk, blk, blk, blk, blk, pl.BlockSpec((1, hw), lambda b, h: (0, h))],
        out_specs=blk,
        out_shape=jax.ShapeDtypeStruct((batch, seq_len, width), BF16),
        scratch_shapes=[pltpu.VMEM((2 * (hw // HG_DIM), HG_DIM, HG_DIM), F32), pltpu.VMEM((seq_len, hw), F32)],
        compiler_params=_cparams(("parallel", "parallel")),
        name="hgrn",
    )(sp(q), sp(v), sp(gt), sp(ff), sp(fb), norm_g.reshape(1, width))


def _dattn_kernel(q_ref, k_ref, vt_ref, lam_ref, sg_ref, o_ref, s_buf, e_buf, acc_buf, *, kc):
    lv = lam_ref[...]
    lam = (jnp.exp(jnp.sum(lv[0:1] * lv[1:2], axis=-1, keepdims=True))
           - jnp.exp(jnp.sum(lv[2:3] * lv[3:4], axis=-1, keepdims=True)) + LAMBDA_INIT)
    q = q_ref[...]
    lane = lax.broadcasted_iota(jnp.int32, q.shape, 1)
    zero = jnp.zeros_like(q)
    qs = (jnp.where(lane < DA_QK_DIM, q, zero), jnp.where(lane >= DA_QK_DIM, q, zero))
    items = [(j, c) for j in range(k_ref.shape[0] // kc) for c in (0, 1)]
    nbuf = s_buf.shape[0]
    dv = vt_ref.shape[0]
    ones = jnp.ones((acc_buf.shape[1] - dv, kc), BF16)
    m, alpha = [None, None], {}

    def scores(i):
        j, c = items[i]
        s_buf[i % nbuf] = _dot_nt(k_ref[j * kc:(j + 1) * kc, :], qs[c])

    def exponentials(i):
        j, c = items[i]
        s = s_buf[i % nbuf]
        mj = jnp.max(s, axis=0, keepdims=True)
        if j == 0:
            m[c], alpha[i] = mj, None
        else:
            mn = jnp.maximum(m[c], mj)
            alpha[i] = jnp.exp2(m[c] - mn)
            m[c] = mn
        e_buf[i % nbuf] = jnp.exp2(s - m[c]).astype(BF16)

    def values(i):
        j, c = items[i]
        pv = _dot(jnp.concatenate([vt_ref[:, j * kc:(j + 1) * kc], ones], axis=0), e_buf[i % nbuf])
        acc_buf[c] = pv if alpha[i] is None else alpha[i] * acc_buf[c] + pv

    scores(0)
    scores(1)
    exponentials(0)
    for i in range(len(items)):
        if i + 2 < len(items):
            scores(i + 2)
        if i + 1 < len(items):
            exponentials(i + 1)
        values(i)
    a0, a1 = acc_buf[0], acc_buf[1]
    ot = a0[:dv] * (1.0 / a0[dv:dv + 1]) - a1[:dv] * (lam / a1[dv:dv + 1])
    ms = jnp.mean(ot * ot, axis=0, keepdims=True)
    y = ot * lax.rsqrt(ms + RMS_EPS) * (sg_ref[...] * (1.0 - LAMBDA_INIT))
    o_ref[...] = y.T.astype(BF16)


def _dattn(dq, dk, dvt, lam_vecs, subln_g, batch, seq_len, tq, kc):
    width = dq.shape[1]
    hw = DA_V_DIM
    sp = lambda a: a.reshape(batch, seq_len, width)
    kk = pl.BlockSpec((None, seq_len, hw), lambda b, h, i: (b, 0, h))
    qo = pl.BlockSpec((None, tq, hw), lambda b, h, i: (b, i, h))
    nbuf = 4
    return pl.pallas_call(
        functools.partial(_dattn_kernel, kc=kc),
        grid=(batch, width // hw, seq_len // tq),
        in_specs=[qo, kk, pl.BlockSpec((hw, seq_len), lambda b, h, i: (h, b)),
                  pl.BlockSpec(lam_vecs.shape, lambda b, h, i: (0, 0)),
                  pl.BlockSpec((hw, 1), lambda b, h, i: (0, 0))],
        out_specs=qo,
        out_shape=jax.ShapeDtypeStruct((batch, seq_len, width), BF16),
        scratch_shapes=[pltpu.VMEM((nbuf, kc, tq), F32), pltpu.VMEM((nbuf, kc, tq), BF16),
                        pltpu.VMEM((2, hw + BF16_SUBLANES_V7X, tq), F32)],
        compiler_params=_cparams(("parallel", "parallel", "parallel")),
        name="dattn",
    )(sp(dq), sp(dk), dvt, lam_vecs, subln_g.reshape(hw, 1))


def _kvproj_kernel(m_ref, wk_ref, wv_ref, k_ref, v_ref, mb_buf):
    @pl.when(pl.program_id(0) == 0)
    def _():
        mb_buf[...] = m_ref[...].astype(BF16)

    mb = mb_buf[...]
    k_ref[...] = _dot(mb, wk_ref[...].astype(BF16)).astype(BF16)
    v_ref[...] = _dot(mb, wv_ref[...].astype(BF16)).astype(BF16)


def _kvproj(mem, wk, wv, tn):
    batch, m, d = mem.shape
    rows = batch * m
    resident = pl.BlockSpec((rows, d), lambda j: (0, 0))
    w_cols = pl.BlockSpec((d, tn), lambda j: (0, j))
    o_cols = pl.BlockSpec((rows, tn), lambda j: (0, j))
    kx, vx = pl.pallas_call(
        _kvproj_kernel,
        grid=(d // tn,),
        in_specs=[resident, w_cols, w_cols],
        out_specs=[o_cols, o_cols],
        out_shape=[jax.ShapeDtypeStruct((rows, d), BF16)] * 2,
        scratch_shapes=[pltpu.VMEM((rows, d), BF16)],
        compiler_params=_cparams(("arbitrary",)),
        name="kvproj",
    )(mem.reshape(rows, d), wk, wv)
    return kx.reshape(batch, m, d), vx.reshape(batch, m, d)


def _mixattn_kernel(hg_ref, da_ref, h0_ref, wm_ref, g1_ref, b1_ref, k_ref, v_ref, wq_ref, wo_ref, g2_ref, b2_ref,
                    wr_ref, h2_ref, aff_ref, mix_buf, h1_buf, *, parts):
    half = hg_ref.shape[1]
    pr = hg_ref.shape[0] // parts
    rows = lambda r: slice(r * pr, (r + 1) * pr)

    def project(r):
        mix_buf[r % 2] = _dot(hg_ref[rows(r), :], wm_ref[:half, :]) + _dot(da_ref[rows(r), :], wm_ref[half:, :])

    def normalize(r):
        h1_buf[rows(r), :] = _layer_norm(DEEPNORM_ALPHA * h0_ref[rows(r), :] + mix_buf[r % 2],
                                         g1_ref[...], b1_ref[...])

    project(0)
    for r in range(parts):
        if r + 1 < parts:
            project(r + 1)
        normalize(r)

    h1 = h1_buf[...]
    d = h1.shape[1]
    hd = d // XA_HEADS
    q = (_dot(h1.astype(BF16), wq_ref[...]) * (hd ** -0.5)).astype(BF16)
    outs = []
    for i in range(XA_HEADS):
        sl = slice(i * hd, (i + 1) * hd)
        s = _dot_nt(q[:, sl], k_ref[:, sl])
        e = jnp.exp(s - jnp.max(s, axis=-1, keepdims=True))
        inv_l = 1.0 / jnp.sum(e, axis=-1, keepdims=True)
        outs.append((_dot(e.astype(BF16), v_ref[:, sl]) * inv_l).astype(BF16))
    xa = _dot(jnp.concatenate(outs, axis=-1), wo_ref[...])
    h2 = _layer_norm(DEEPNORM_ALPHA * h1 + xa, g2_ref[...], b2_ref[...])
    h2_ref[...] = h2
    logits = _dot_nt(wr_ref[...], h2.astype(BF16))
    e = jnp.exp(logits - jnp.max(logits, axis=0, keepdims=True))
    aff_ref[...] = e / jnp.sum(e, axis=0, keepdims=True)


def _mixattn(hg, da, h0, wm_bf, g1, b1, kx, vx, wq_bf, wo_bf, g2, b2, wr_t_bf, batch, seq_len, tm):
    n, d = h0.shape
    half = hg.shape[1]
    m = kx.shape[1]
    ne = wr_t_bf.shape[0]
    tiles = seq_len // tm
    parts = 4 if tm % 64 == 0 else 1
    rows = lambda w: pl.BlockSpec((tm, w), lambda bi, i: (bi * tiles + i, 0))
    kv = pl.BlockSpec((None, m, d), lambda bi, i: (bi, 0, 0))
    const = lambda shape: pl.BlockSpec(shape, lambda bi, i: (0, 0))
    return pl.pallas_call(
        functools.partial(_mixattn_kernel, parts=parts),
        grid=(batch, tiles),
        in_specs=[rows(half), rows(half), rows(d), const((d, d)), const((1, d)), const((1, d)), kv, kv,
                  const((d, d)), const((d, d)), const((1, d)), const((1, d)), const((ne, d))],
        out_specs=[rows(d), pl.BlockSpec((None, ne, tm), lambda bi, i: (bi, 0, i))],
        out_shape=[jax.ShapeDtypeStruct((n, d), F32), jax.ShapeDtypeStruct((batch, ne, seq_len), F32)],
        scratch_shapes=[pltpu.VMEM((2, tm // parts, d), F32), pltpu.VMEM((tm, d), F32)],
        compiler_params=_cparams(("parallel", "parallel")),
        name="mixattn",
    )(hg, da, h0, wm_bf, g1, b1, kx, vx, wq_bf, wo_bf, g2, b2, wr_t_bf)


def _route_kernel(aff_ref, slot_ref, *, cap):
    aff = aff_ref[...]
    ne, seq_len = aff.shape
    nblk = seq_len // LANES_V7X

    def count(mask):
        return jnp.sum(jnp.where(mask, 1.0, 0.0), axis=-1, keepdims=True)

    bits = jnp.zeros((ne, 1), jnp.int32)
    for bit in range(30, -1, -1):
        cand = bits | (1 << bit)
        bits = jnp.where(count(aff >= pltpu.bitcast(cand, F32)) >= cap, cand, bits)
    thr = pltpu.bitcast(bits, F32)

    ri = lax.broadcasted_iota(jnp.int32, (LANES_V7X, LANES_V7X), 0)
    ci = lax.broadcasted_iota(jnp.int32, (LANES_V7X, LANES_V7X), 1)
    tri = jnp.where(ri <= ci, 1.0, 0.0).astype(BF16)

    def prefix_count(mask):
        m = jnp.where(mask, 1.0, 0.0)
        stack = jnp.concatenate([m[:, j * LANES_V7X:(j + 1) * LANES_V7X] for j in range(nblk)], axis=0)
        within = _dot(stack.astype(BF16), tri)
        pieces, carry = [], jnp.zeros((ne, 1), F32)
        for j in range(nblk):
            blk = within[j * ne:(j + 1) * ne]
            pieces.append(blk + carry)
            carry = carry + blk[:, LANES_V7X - 1:LANES_V7X]
        return jnp.concatenate(pieces, axis=1), m

    gt = aff > thr
    eq = aff == thr
    need = cap - count(gt)
    eq_incl, eq_f = prefix_count(eq)
    sel = gt | (eq & ((eq_incl - eq_f) < need))
    sel_incl, sel_f = prefix_count(sel)
    slot_ref[...] = jnp.where(sel, sel_incl - sel_f, -1.0).astype(jnp.int32)


def _route(aff_t, cap):
    batch, ne, seq_len = aff_t.shape
    blk = pl.BlockSpec((batch * ne, seq_len), lambda i: (0, 0))
    slots = pl.pallas_call(
        functools.partial(_route_kernel, cap=cap),
        grid=(1,),
        in_specs=[blk],
        out_specs=blk,
        out_shape=jax.ShapeDtypeStruct((batch * ne, seq_len), jnp.int32),
        compiler_params=_cparams(("arbitrary",)),
        name="route",
    )(aff_t.reshape(batch * ne, seq_len))
    return slots.reshape(batch, ne, seq_len)


def _gather_kernel(slot_ref, aff_ref, h_ref, xs_ref, gate_ref, *, cap):
    hb = h_ref[...].astype(BF16)
    ne, seq_len = slot_ref.shape
    rid = lax.broadcasted_iota(jnp.int32, (cap, seq_len), 0)
    for e in range(ne):
        pick = slot_ref[e:e + 1, :] == rid
        xs_ref[e] = _dot(jnp.where(pick, 1.0, 0.0).astype(BF16), hb).astype(BF16)
        gate_ref[e] = jnp.sum(jnp.where(pick, aff_ref[e:e + 1, :], 0.0), axis=-1, keepdims=True)


def _gather(slots, aff_t, h2, cap):
    batch, ne, seq_len = slots.shape
    d = h2.shape[1]
    blk = pl.BlockSpec((None, ne, seq_len), lambda b: (b, 0, 0))
    return pl.pallas_call(
        functools.partial(_gather_kernel, cap=cap),
        grid=(batch,),
        in_specs=[blk, blk, pl.BlockSpec((seq_len, d), lambda b: (b, 0))],
        out_specs=[pl.BlockSpec((ne, None, cap, d), lambda b: (0, b, 0, 0)),
                   pl.BlockSpec((ne, None, cap, 1), lambda b: (0, b, 0, 0))],
        out_shape=[jax.ShapeDtypeStruct((ne, batch, cap, d), BF16), jax.ShapeDtypeStruct((ne, batch, cap, 1), F32)],
        compiler_params=_cparams(("parallel",)),
        name="gather",
    )(slots, aff_t, h2)


def _ffn_kernel(xs_ref, gate_ref, wg_ref, wu_ref, wd_ref, y_ref, acc_ref, g_buf, u_buf, a_buf, *, n_chunks):
    c = pl.program_id(1)
    parts = g_buf.shape[0]
    pr = xs_ref.shape[0] // parts
    rows = lambda r: slice(r * pr, (r + 1) * pr)

    def chunk(first, final):
        wg, wu, wd = wg_ref[...].astype(BF16), wu_ref[...].astype(BF16), wd_ref[...].astype(BF16)

        def gate_up(r):
            xs = xs_ref[rows(r), :]
            g_buf[r] = _dot(xs, wg)
            u_buf[r] = _dot(xs, wu)

        def down(r):
            a_buf[r] = (_silu(g_buf[r]) * u_buf[r]).astype(BF16)
            part = _dot(a_buf[r], wd)
            total = part if first else acc_ref[rows(r), :] + part
            if final:
                y_ref[rows(r), :] = (total * gate_ref[rows(r), :]).astype(BF16)
            else:
                acc_ref[rows(r), :] = total

        gate_up(0)
        for r in range(parts):
            if r + 1 < parts:
                gate_up(r + 1)
            down(r)

    if n_chunks == 1:
        chunk(True, True)
    else:
        pl.when(c == 0)(lambda: chunk(True, False))
        if n_chunks > 2:
            pl.when((c > 0) & (c < n_chunks - 1))(lambda: chunk(False, False))
        pl.when(c == n_chunks - 1)(lambda: chunk(False, True))


def _ffn(xs, gates, w_gate, w_up, w_down, fc):
    ne, rows, d = xs.shape
    dff = w_gate.shape[2]
    parts = 2 if rows % 32 == 0 else 1
    return pl.pallas_call(
        functools.partial(_ffn_kernel, n_chunks=dff // fc),
        grid=(ne, dff // fc),
        in_specs=[pl.BlockSpec((None, rows, d), lambda e, c: (e, 0, 0)),
                  pl.BlockSpec((None, rows, 1), lambda e, c: (e, 0, 0)),
                  pl.BlockSpec((None, d, fc), lambda e, c: (e, 0, c)),
                  pl.BlockSpec((None, d, fc), lambda e, c: (e, 0, c)),
                  pl.BlockSpec((None, fc, d), lambda e, c: (e, c, 0))],
        out_specs=pl.BlockSpec((None, rows, d), lambda e, c: (e, 0, 0)),
        out_shape=jax.ShapeDtypeStruct((ne, rows, d), BF16),
        scratch_shapes=[pltpu.VMEM((rows, d), F32), pltpu.VMEM((parts, rows // parts, fc), F32),
                        pltpu.VMEM((parts, rows // parts, fc), F32), pltpu.VMEM((parts, rows // parts, fc), BF16)],
        compiler_params=_cparams(("parallel", "arbitrary")),
        name="ffn",
    )(xs, gates, w_gate, w_up, w_down)


def _combine_kernel(slot_ref, y_ref, h_ref, g_ref, b_ref, o_ref, moe_buf, *, cap, parts):
    ne, tl = slot_ref.shape
    pr = tl // parts
    rows = lambda r: slice(r * pr, (r + 1) * pr)
    slot_t = slot_ref[...].astype(F32).T
    lane = lax.broadcasted_iota(jnp.int32, (pr, cap), 1).astype(F32)
    y = y_ref[...].reshape(ne * cap, y_ref.shape[-1])

    def scatter(r):
        place = jnp.concatenate(
            [jnp.where(slot_t[rows(r), e:e + 1] == lane, 1.0, 0.0).astype(BF16) for e in range(ne)], axis=1)
        moe_buf[r % 2] = _dot(place, y)

    def normalize(r):
        o_ref[rows(r), :] = _layer_norm(DEEPNORM_ALPHA * h_ref[rows(r), :] + moe_buf[r % 2], g_ref[...], b_ref[...])

    scatter(0)
    for r in range(parts):
        if r + 1 < parts:
            scatter(r + 1)
        normalize(r)


def _combine(slots, y, h2, g, b, cap, tl):
    batch, ne, seq_len = slots.shape
    d = h2.shape[1]
    tiles = seq_len // tl
    parts = 4 if tl % 64 == 0 else 1
    row = pl.BlockSpec((tl, d), lambda bi, i: (bi * tiles + i, 0))
    const = pl.BlockSpec((1, d), lambda bi, i: (0, 0))
    return pl.pallas_call(
        functools.partial(_combine_kernel, cap=cap, parts=parts),
        grid=(batch, tiles),
        in_specs=[pl.BlockSpec((None, ne, tl), lambda bi, i: (bi, 0, i)),
                  pl.BlockSpec((ne, None, cap, d), lambda bi, i: (0, bi, 0, 0)),
                  row, const, const],
        out_specs=row,
        out_shape=jax.ShapeDtypeStruct((batch * seq_len, d), F32),
        scratch_shapes=[pltpu.VMEM((2, tl // parts, d), F32)],
        compiler_params=_cparams(("parallel", "parallel")),
        name="combine",
    )(slots, y, h2, g, b)


def kernel(x, mem, emb_ln_g, emb_ln_b, w_in, hg_lb_logits, hg_norm_g, da_lambda_q1, da_lambda_k1, da_lambda_q2,
           da_lambda_k2, da_subln_g, w_mix_out, ln1_g, ln1_b, xa_wq, xa_wk, xa_wv, xa_wo, ln2_g, ln2_b, w_router,
           w_gate, w_up, w_down, ln3_g, ln3_b):
    batch, seq_len, d = x.shape
    assert w_in.shape[0] == DEPTH and seq_len % HG_CHUNK == 0 and seq_len % LANES_V7X == 0
    n = batch * seq_len
    cap = EC_FACTOR * seq_len // N_EXPERTS
    vec = lambda a: a.reshape(1, -1)
    bf = lambda a: a.astype(BF16)

    t = _tiles(seq_len)

    h0, q, v, gt, ff, fb, dq, dk, dv = _inproj(
        x.reshape(n, d), vec(emb_ln_g), vec(emb_ln_b), bf(w_in[0]), hg_lb_logits, seq_len, t.inproj_rows)
    hg = _hgrn(q, v, gt, ff, fb, hg_norm_g[0], batch, seq_len)
    lam_vecs = jnp.stack([da_lambda_q1[0], da_lambda_k1[0], da_lambda_q2[0], da_lambda_k2[0]])
    da = _dattn(dq, dk, dv, lam_vecs, da_subln_g[0], batch, seq_len, t.dattn_queries, t.dattn_keys)
    kx, vx = _kvproj(mem, xa_wk[0], xa_wv[0], t.kv_cols)
    h2, aff_t = _mixattn(hg.reshape(n, -1), da.reshape(n, -1), h0, bf(w_mix_out[0]), ln1_g, ln1_b, kx, vx,
                         bf(xa_wq[0]), bf(xa_wo[0]), ln2_g, ln2_b, bf(w_router[0].T), batch, seq_len, t.mixattn_rows)

    slots = _route(aff_t, cap)
    xs, gates = _gather(slots, aff_t, h2, cap)
    y = _ffn(xs.reshape(N_EXPERTS, batch * cap, d), gates.reshape(N_EXPERTS, batch * cap, 1),
             w_gate[0], w_up[0], w_down[0], t.ffn_cols)
    out = _combine(slots, y.reshape(N_EXPERTS, batch, cap, d), h2, ln3_g, ln3_b, cap, t.combine_rows)
    return out.reshape(batch, seq_len, d)
```

```python
import functools
import math
from typing import NamedTuple

import jax
import jax.numpy as jnp
from jax import lax
from jax.experimental import pallas as pl
from jax.experimental.pallas import tpu as pltpu

F32 = jnp.float32
BF16 = jnp.bfloat16

HG_HEADS = 4
HG_DIM = 128
HG_CHUNK = 64
DA_HEADS = 4
DA_QK_DIM = 64
DA_V_DIM = 128
ROPE_DIM = DA_QK_DIM // 4
ROPE_THETA = 500000.0
XA_HEADS = 4
N_EXPERTS = 16
EC_FACTOR = 2
LN_EPS = 1e-5
RMS_EPS = 1e-6
DEPTH = 1
DEEPNORM_ALPHA = (2.0 * DEPTH) ** 0.25
LAMBDA_INIT = 0.8 - 0.6 * math.exp(-0.3 * 0)
LOG2_E = math.log2(math.e)

LANES_V7X = 128
BF16_SUBLANES_V7X = 16
VMEM_LIMIT_V7X = 56 * 1024 * 1024


class _Tiles(NamedTuple):
    inproj_rows: int
    dattn_queries: int
    dattn_keys: int
    mixattn_rows: int
    ffn_cols: int
    combine_rows: int


def _tiles(seq_len):
    fit = lambda rows: min(rows, seq_len)
    return _Tiles(inproj_rows=fit(1024), dattn_queries=fit(2048), dattn_keys=fit(512), mixattn_rows=fit(1024),
                  ffn_cols=512, combine_rows=fit(1024))


def _cparams(sem):
    return pltpu.CompilerParams(dimension_semantics=sem, vmem_limit_bytes=VMEM_LIMIT_V7X)


def _layer_norm(x, g, b):
    mu = jnp.mean(x, axis=-1, keepdims=True)
    xc = x - mu
    var = jnp.mean(xc * xc, axis=-1, keepdims=True)
    return xc * lax.rsqrt(var + LN_EPS) * g + b


def _silu(x):
    return x * jax.nn.sigmoid(x)


def _dot(a, b):
    return jnp.dot(a, b, preferred_element_type=F32)


def _dot_nt(a, b):
    return lax.dot_general(a, b, (((1,), (1,)), ((), ())), preferred_element_type=F32)


def _dot_tn(a, b):
    return lax.dot_general(a, b, (((0,), (0,)), ((), ())), preferred_element_type=F32)


def _inproj_kernel(x_ref, g_ref, b_ref, w_ref, c_ref, sa_ref, sb_ref, lbl_ref,
                   h0_ref, q_ref, v_ref, gt_ref, ff_ref, fb_ref, dq_ref, dk_ref, dv_ref, hb_buf):
    width = q_ref.shape[1]
    parts = hb_buf.shape[0]
    pr = x_ref.shape[0] // parts
    rows = lambda r: slice(r * pr, (r + 1) * pr)
    widen = lambda ref, r: jnp.concatenate([ref[rows(r), :]] * (width // ref.shape[1]), axis=1)
    lbl = lbl_ref[...]

    def forget_gate(u, d):
        l0, l1 = lbl[2 * d:2 * d + 1], lbl[2 * d + 1:2 * d + 2]
        m = jnp.maximum(l0, l1)
        e0, e1 = jnp.exp(l0 - m), jnp.exp(l1 - m)
        lb = e0 / (e0 + e1)
        return lb + (1.0 - lb) * jax.nn.sigmoid(u)

    def normalize(r):
        h = _layer_norm(x_ref[rows(r), :], g_ref[...], b_ref[...])
        h0_ref[rows(r), :] = h
        hb_buf[r] = h.astype(BF16)

    def project(r, columns):
        proj = lambda c: _dot(hb_buf[r], w_ref[:, c * width:(c + 1) * width])
        rope_c, rope_sa, rope_sb = widen(c_ref, r), widen(sa_ref, r), widen(sb_ref, r)

        def rope(t):
            return (t * rope_c + pltpu.roll(t, width - ROPE_DIM // 2, 1) * rope_sa
                    + pltpu.roll(t, ROPE_DIM // 2, 1) * rope_sb)

        for c in columns:
            u = proj(c)
            if c == 0:
                q_ref[rows(r), :] = _silu(u).astype(BF16)
            elif c == 1:
                v_ref[rows(r), :] = u.astype(BF16)
            elif c == 2:
                gt_ref[rows(r), :] = _silu(u).astype(BF16)
            elif c == 3:
                ff_ref[rows(r), :] = forget_gate(u, 0)
            elif c == 4:
                fb_ref[rows(r), :] = forget_gate(u, 1)
            elif c == 5:
                dq_ref[rows(r), :] = (rope(u) * (DA_QK_DIM ** -0.5 * LOG2_E)).astype(BF16)
            elif c == 6:
                dk_ref[rows(r), :] = rope(u).astype(BF16)
            else:
                dv_ref[:, rows(r)] = u.T.astype(BF16)

    normalize(0)
    for r in range(parts):
        project(r, range(0, 2))
        if r + 1 < parts:
            normalize(r + 1)
        project(r, range(2, 8))


def _rope_lane_tables(seq_len, width):
    half = ROPE_DIM // 2
    inv = 1.0 / (ROPE_THETA ** (jnp.arange(0, ROPE_DIM, 2, dtype=F32) / ROPE_DIM))
    ang = jnp.arange(seq_len, dtype=F32)[:, None] * inv[None, :]
    cos, sin = jnp.cos(ang), jnp.sin(ang)
    pad1 = jnp.ones((seq_len, DA_QK_DIM - ROPE_DIM), F32)
    pad0 = jnp.zeros((seq_len, DA_QK_DIM - ROPE_DIM), F32)
    z = jnp.zeros((seq_len, half), F32)
    reps = width // DA_QK_DIM
    c = jnp.tile(jnp.concatenate([cos, cos, pad1], -1), (1, reps))
    sa = jnp.tile(jnp.concatenate([-sin, z, pad0], -1), (1, reps))
    sb = jnp.tile(jnp.concatenate([z, sin, pad0], -1), (1, reps))
    return c, sa, sb


def _inproj(x2, g, b, w_bf, lb_logits, seq_len, tm):
    n, d = x2.shape
    width = w_bf.shape[1] // 8
    c, sa, sb = _rope_lane_tables(seq_len, LANES_V7X)
    tpb = seq_len // tm
    parts = 4 if tm % 512 == 0 else 1
    row = lambda p, bi: (bi * tpb + p, 0)
    const = lambda p, bi: (0, 0)
    tab = lambda p, bi: (p, 0)
    wide = lambda dt: jax.ShapeDtypeStruct((n, width), dt)
    return pl.pallas_call(
        _inproj_kernel,
        grid=(tpb, n // seq_len),
        in_specs=[pl.BlockSpec((tm, d), row), pl.BlockSpec((1, d), const), pl.BlockSpec((1, d), const),
                  pl.BlockSpec(w_bf.shape, const),
                  pl.BlockSpec((tm, LANES_V7X), tab), pl.BlockSpec((tm, LANES_V7X), tab),
                  pl.BlockSpec((tm, LANES_V7X), tab), pl.BlockSpec((4, width), const)],
        out_specs=([pl.BlockSpec((tm, d), row)] + [pl.BlockSpec((tm, width), row)] * 7
                   + [pl.BlockSpec((width, tm), lambda p, bi: (0, bi * tpb + p))]),
        out_shape=[jax.ShapeDtypeStruct((n, d), F32), wide(BF16), wide(BF16), wide(BF16), wide(F32), wide(F32),
                   wide(BF16), wide(BF16), jax.ShapeDtypeStruct((width, n), BF16)],
        scratch_shapes=[pltpu.VMEM((parts, tm // parts, d), BF16)],
        compiler_params=_cparams(("parallel", "parallel")),
        name="inproj",
    )(x2, g, b, w_bf, c, sa, sb, lb_logits.reshape(4, width))


def _hgrn_kernel(q_ref, v_ref, gt_ref, ff_ref, fb_ref, ng_ref, o_ref, st_ref, oacc_ref):
    seq_len, width = q_ref.shape
    heads = width // HG_DIM
    ch = HG_CHUNK
    blk = min(2 * ch, seq_len)
    cpb = blk // ch
    nblk = seq_len // blk
    sub_rows = 8
    tiles_per_chunk = ch // sub_rows
    st_ref[...] = jnp.zeros_like(st_ref)
    sub = lax.broadcasted_iota(jnp.int32, (blk // sub_rows, sub_rows, width), 1)
    ri = lax.broadcasted_iota(jnp.int32, (blk, blk), 0)
    ci = lax.broadcasted_iota(jnp.int32, (blk, blk), 1)
    same_chunk = (ri // ch) == (ci // ch)
    masks = (same_chunk & (ri >= ci), same_chunk & (ri <= ci))
    row_chunk = (lax.broadcasted_iota(jnp.int32, (blk, HG_DIM), 0) // ch).astype(BF16)
    in_chunk = [row_chunk == j for j in range(cpb)]
    sl = lambda hh: slice(hh * HG_DIM, (hh + 1) * HG_DIM)

    def cumprod(x, backward):
        x3 = x.reshape(blk // sub_rows, sub_rows, width)
        s = 1
        while s < sub_rows:
            if backward:
                x3 = x3 * jnp.where(sub < sub_rows - s, pltpu.roll(x3, sub_rows - s, 1), 1.0)
            else:
                x3 = x3 * jnp.where(sub >= s, pltpu.roll(x3, s, 1), 1.0)
            s *= 2
        edge = 0 if backward else sub_rows - 1
        out = [None] * (blk // sub_rows)
        for c in range(cpb):
            carry = None
            order = range(tiles_per_chunk - 1, -1, -1) if backward else range(tiles_per_chunk)
            for t in order:
                idx = c * tiles_per_chunk + t
                out[idx] = x3[idx] if carry is None else x3[idx] * carry
                carry = out[idx][edge:edge + 1]
        return jnp.concatenate(out, axis=0)

    class Item:
        pass

    def prep(d, i):
        it = Item()
        it.d = d
        bi = i if d == 0 else nblk - 1 - i
        it.rows = pl.ds(pl.multiple_of(bi * blk, blk), blk)
        f = (ff_ref if d == 0 else fb_ref)[it.rows, :]
        g = cumprod(f, d == 1)
        edge = ch - 1 if d == 0 else 0
        it.decays = [g[j * ch + edge:j * ch + edge + 1] for j in range(cpb)]
        kin = (1.0 - f) / g
        it.k_out = (kin * jnp.concatenate([jnp.broadcast_to(dj, (ch, width)) for dj in it.decays], axis=0)
                    ).astype(BF16)
        it.q_in = (q_ref[it.rows, :].astype(F32) * g).astype(BF16)
        it.k_in = kin.astype(BF16)
        it.vv = v_ref[it.rows, :]
        return it

    def scores(it):
        it.scores = [_dot_nt(it.q_in[:, sl(hh)], it.k_in[:, sl(hh)]) for hh in range(heads)]
        it.incs = []
        for hh in range(heads):
            k_diag = jnp.concatenate([jnp.where(m, it.k_out[:, sl(hh)], jnp.zeros((), BF16)) for m in in_chunk],
                                     axis=1)
            it.incs.append(_dot_tn(it.vv[:, sl(hh)], k_diag))

    def intra(it):
        it.intra = [_dot(jnp.where(masks[it.d], it.scores[hh], 0.0).astype(BF16), it.vv[:, sl(hh)])
                    for hh in range(heads)]

    def scan(it):
        d = it.d
        st = [st_ref[d * heads + hh] for hh in range(heads)]
        it.inter = [[None] * cpb for _ in range(heads)]
        for step in range(cpb):
            j = step if d == 0 else cpb - 1 - step
            for hh in range(heads):
                it.inter[hh][j] = _dot_nt(it.q_in[j * ch:(j + 1) * ch, sl(hh)], st[hh].astype(BF16))
            for hh in range(heads):
                st[hh] = st[hh] * it.decays[j][:, sl(hh)] + it.incs[hh][:, j * HG_DIM:(j + 1) * HG_DIM]
        for hh in range(heads):
            st_ref[d * heads + hh] = st[hh]

    def write(it, completes):
        ng = ng_ref[...]
        for hh in range(heads):
            o = it.intra[hh] + jnp.concatenate(it.inter[hh], axis=0)
            if not completes:
                oacc_ref[it.rows, sl(hh)] = o
                continue
            o = o + oacc_ref[it.rows, sl(hh)]
            ms = jnp.mean(o * o, axis=-1, keepdims=True)
            gate = gt_ref[it.rows, sl(hh)].astype(F32)
            o_ref[it.rows, sl(hh)] = (o * lax.rsqrt(ms + RMS_EPS) * ng[:, sl(hh)] * gate).astype(BF16)

    def body(i, fwd_completes, bwd_completes):
        fwd = prep(0, i)
        scores(fwd)
        bwd = prep(1, i)
        intra(fwd)
        scores(bwd)
        scan(fwd)
        intra(bwd)
        write(fwd, fwd_completes)
        scan(bwd)
        write(bwd, bwd_completes)

    half = nblk // 2

    def loop(lo, hi, completes):
        trips = hi - lo
        if trips > 0:
            lax.fori_loop(lo, hi, lambda i, c: (body(i, completes, completes), c)[1], 0,
                          unroll=8 if trips % 8 == 0 else 1)

    loop(0, half, False)
    if nblk % 2:
        body(half, False, True)
    loop(nblk - half, nblk, True)


def _hgrn(q, v, gt, ff, fb, norm_g, batch, seq_len):
    width = q.shape[1]
    hw = width
    sp = lambda a: a.reshape(batch, seq_len, width)
    blk = pl.BlockSpec((None, seq_len, hw), lambda b, h: (b, 0, h))
    return pl.pallas_call(
        _hgrn_kernel,
        grid=(batch, width // hw),
        in_specs=[blk, blk, blk, blk, blk, pl.BlockSpec((1, hw), lambda b, h: (0, h))],
        out_specs=blk,
        out_shape=jax.ShapeDtypeStruct((batch, seq_len, width), BF16),
        scratch_shapes=[pltpu.VMEM((2 * (hw // HG_DIM), HG_DIM, HG_DIM), F32), pltpu.VMEM((seq_len, hw), F32)],
        compiler_params=_cparams(("parallel", "parallel")),
        name="hgrn",
    )(sp(q), sp(v), sp(gt), sp(ff), sp(fb), norm_g.reshape(1, width))


def _dattn_kernel(q_ref, k_ref, vt_ref, lam_ref, sg_ref, o_ref, s_buf, e_buf, acc_buf, *, kc):
    lv = lam_ref[...]
    lam = (jnp.exp(jnp.sum(lv[0:1] * lv[1:2], axis=-1, keepdims=True))
           - jnp.exp(jnp.sum(lv[2:3] * lv[3:4], axis=-1, keepdims=True)) + LAMBDA_INIT)
    q = q_ref[...]
    lane = lax.broadcasted_iota(jnp.int32, q.shape, 1)
    zero = jnp.zeros_like(q)
    qs = (jnp.where(lane < DA_QK_DIM, q, zero), jnp.where(lane >= DA_QK_DIM, q, zero))
    items = [(j, c) for j in range(k_ref.shape[0] // kc) for c in (0, 1)]
    nbuf = s_buf.shape[0]
    dv = vt_ref.shape[0]
    vt_ones = jnp.concatenate([vt_ref[...], jnp.ones((acc_buf.shape[1] - dv, vt_ref.shape[1]), BF16)], axis=0)
    m, alpha = [None, None], {}

    def scores(i):
        j, c = items[i]
        s_buf[i % nbuf] = _dot_nt(k_ref[j * kc:(j + 1) * kc, :], qs[c])

    def exponentials(i):
        j, c = items[i]
        s = s_buf[i % nbuf]
        mj = jnp.max(s, axis=0, keepdims=True)
        if j == 0:
            m[c], alpha[i] = mj, None
        else:
            mn = jnp.maximum(m[c], mj)
            alpha[i] = jnp.exp2(m[c] - mn)
            m[c] = mn
        e_buf[i % nbuf] = jnp.exp2(s - m[c]).astype(BF16)

    def values(i):
        j, c = items[i]
        pv = _dot(vt_ones[:, j * kc:(j + 1) * kc], e_buf[i % nbuf])
        acc_buf[c] = pv if alpha[i] is None else alpha[i] * acc_buf[c] + pv

    scores(0)
    scores(1)
    exponentials(0)
    for i in range(len(items)):
        if i + 2 < len(items):
            scores(i + 2)
        if i + 1 < len(items):
            exponentials(i + 1)
        values(i)
    a0, a1 = acc_buf[0], acc_buf[1]
    ot = a0[:dv] * (1.0 / a0[dv:dv + 1]) - a1[:dv] * (lam / a1[dv:dv + 1])
    ms = jnp.mean(ot * ot, axis=0, keepdims=True)
    y = ot * lax.rsqrt(ms + RMS_EPS) * (sg_ref[...] * (1.0 - LAMBDA_INIT))
    o_ref[...] = y.T.astype(BF16)


def _dattn(dq, dk, dvt, lam_vecs, subln_g, batch, seq_len, tq, kc):
    width = dq.shape[1]
    hw = DA_V_DIM
    sp = lambda a: a.reshape(batch, seq_len, width)
    kk = pl.BlockSpec((None, seq_len, hw), lambda b, h, i: (b, 0, h))
    qo = pl.BlockSpec((None, tq, hw), lambda b, h, i: (b, i, h))
    nbuf = 4
    return pl.pallas_call(
        functools.partial(_dattn_kernel, kc=kc),
        grid=(batch, width // hw, seq_len // tq),
        in_specs=[qo, kk, pl.BlockSpec((hw, seq_len), lambda b, h, i: (h, b)),
                  pl.BlockSpec(lam_vecs.shape, lambda b, h, i: (0, 0)),
                  pl.BlockSpec((hw, 1), lambda b, h, i: (0, 0))],
        out_specs=qo,
        out_shape=jax.ShapeDtypeStruct((batch, seq_len, width), BF16),
        scratch_shapes=[pltpu.VMEM((nbuf, kc, tq), F32), pltpu.VMEM((nbuf, kc, tq), BF16),
                        pltpu.VMEM((2, hw + BF16_SUBLANES_V7X, tq), F32)],
        compiler_params=_cparams(("parallel", "parallel", "parallel")),
        name="dattn",
    )(sp(dq), sp(dk), dvt, lam_vecs, subln_g.reshape(hw, 1))


def _kvproj_kernel(m_ref, wk_ref, wv_ref, k_ref, v_ref):
    mb = m_ref[...].astype(BF16)
    k_ref[...] = _dot(mb, wk_ref[...]).astype(BF16)
    v_ref[...] = _dot(mb, wv_ref[...]).astype(BF16)


def _kvproj(mem, wk_bf, wv_bf):
    batch, m, d = mem.shape
    blk = pl.BlockSpec((None, m, d), lambda b: (b, 0, 0))
    const = pl.BlockSpec((d, d), lambda b: (0, 0))
    return pl.pallas_call(
        _kvproj_kernel,
        grid=(batch,),
        in_specs=[blk, const, const],
        out_specs=[blk, blk],
        out_shape=[jax.ShapeDtypeStruct((batch, m, d), BF16)] * 2,
        compiler_params=_cparams(("parallel",)),
        name="kvproj",
    )(mem, wk_bf, wv_bf)


def _mixattn_kernel(hg_ref, da_ref, h0_ref, wm_ref, g1_ref, b1_ref, k_ref, v_ref, wq_ref, wo_ref, g2_ref, b2_ref,
                    wr_ref, h2_ref, aff_ref, mix_buf, h1_buf, *, parts):
    half = hg_ref.shape[1]
    pr = hg_ref.shape[0] // parts
    rows = lambda r: slice(r * pr, (r + 1) * pr)

    def project(r):
        mix_buf[r % 2] = _dot(hg_ref[rows(r), :], wm_ref[:half, :]) + _dot(da_ref[rows(r), :], wm_ref[half:, :])

    def normalize(r):
        h1_buf[rows(r), :] = _layer_norm(DEEPNORM_ALPHA * h0_ref[rows(r), :] + mix_buf[r % 2],
                                         g1_ref[...], b1_ref[...])

    project(0)
    for r in range(parts):
        if r + 1 < parts:
            project(r + 1)
        normalize(r)

    h1 = h1_buf[...]
    d = h1.shape[1]
    hd = d // XA_HEADS
    q = (_dot(h1.astype(BF16), wq_ref[...]) * (hd ** -0.5)).astype(BF16)
    outs = []
    for i in range(XA_HEADS):
        sl = slice(i * hd, (i + 1) * hd)
        s = _dot_nt(q[:, sl], k_ref[:, sl])
        e = jnp.exp(s - jnp.max(s, axis=-1, keepdims=True))
        inv_l = 1.0 / jnp.sum(e, axis=-1, keepdims=True)
        outs.append((_dot(e.astype(BF16), v_ref[:, sl]) * inv_l).astype(BF16))
    xa = _dot(jnp.concatenate(outs, axis=-1), wo_ref[...])
    h2 = _layer_norm(DEEPNORM_ALPHA * h1 + xa, g2_ref[...], b2_ref[...])
    h2_ref[...] = h2
    logits = _dot_nt(wr_ref[...], h2.astype(BF16))
    e = jnp.exp(logits - jnp.max(logits, axis=0, keepdims=True))
    aff_ref[...] = e / jnp.sum(e, axis=0, keepdims=True)


def _mixattn(hg, da, h0, wm_bf, g1, b1, kx, vx, wq_bf, wo_bf, g2, b2, wr_t_bf, batch, seq_len, tm):
    n, d = h0.shape
    half = hg.shape[1]
    m = kx.shape[1]
    ne = wr_t_bf.shape[0]
    tiles = seq_len // tm
    parts = 4 if tm % 64 == 0 else 1
    rows = lambda w: pl.BlockSpec((tm, w), lambda bi, i: (bi * tiles + i, 0))
    kv = pl.BlockSpec((None, m, d), lambda bi, i: (bi, 0, 0))
    const = lambda shape: pl.BlockSpec(shape, lambda bi, i: (0, 0))
    return pl.pallas_call(
        functools.partial(_mixattn_kernel, parts=parts),
        grid=(batch, tiles),
        in_specs=[rows(half), rows(half), rows(d), const((d, d)), const((1, d)), const((1, d)), kv, kv,
                  const((d, d)), const((d, d)), const((1, d)), const((1, d)), const((ne, d))],
        out_specs=[rows(d), pl.BlockSpec((None, ne, tm), lambda bi, i: (bi, 0, i))],
        out_shape=[jax.ShapeDtypeStruct((n, d), F32), jax.ShapeDtypeStruct((batch, ne, seq_len), F32)],
        scratch_shapes=[pltpu.VMEM((2, tm // parts, d), F32), pltpu.VMEM((tm, d), F32)],
        compiler_params=_cparams(("parallel", "parallel")),
        name="mixattn",
    )(hg, da, h0, wm_bf, g1, b1, kx, vx, wq_bf, wo_bf, g2, b2, wr_t_bf)


def _route_kernel(aff_ref, slot_ref, *, cap):
    aff = aff_ref[...]
    ne, seq_len = aff.shape
    nblk = seq_len // LANES_V7X

    def count(mask):
        return jnp.sum(jnp.where(mask, 1.0, 0.0), axis=-1, keepdims=True)

    bits = jnp.zeros((ne, 1), jnp.int32)
    for bit in range(30, -1, -1):
        cand = bits | (1 << bit)
        bits = jnp.where(count(aff >= pltpu.bitcast(cand, F32)) >= cap, cand, bits)
    thr = pltpu.bitcast(bits, F32)

    ri = lax.broadcasted_iota(jnp.int32, (LANES_V7X, LANES_V7X), 0)
    ci = lax.broadcasted_iota(jnp.int32, (LANES_V7X, LANES_V7X), 1)
    tri = jnp.where(ri <= ci, 1.0, 0.0).astype(BF16)

    def prefix_count(mask):
        m = jnp.where(mask, 1.0, 0.0)
        stack = jnp.concatenate([m[:, j * LANES_V7X:(j + 1) * LANES_V7X] for j in range(nblk)], axis=0)
        within = _dot(stack.astype(BF16), tri)
        pieces, carry = [], jnp.zeros((ne, 1), F32)
        for j in range(nblk):
            blk = within[j * ne:(j + 1) * ne]
            pieces.append(blk + carry)
            carry = carry + blk[:, LANES_V7X - 1:LANES_V7X]
        return jnp.concatenate(pieces, axis=1), m

    gt = aff > thr
    eq = aff == thr
    need = cap - count(gt)
    eq_incl, eq_f = prefix_count(eq)
    sel = gt | (eq & ((eq_incl - eq_f) < need))
    sel_incl, sel_f = prefix_count(sel)
    slot_ref[...] = jnp.where(sel, sel_incl - sel_f, -1.0).astype(jnp.int32)


def _route(aff_t, cap):
    batch, ne, seq_len = aff_t.shape
    blk = pl.BlockSpec((batch * ne, seq_len), lambda i: (0, 0))
    slots = pl.pallas_call(
        functools.partial(_route_kernel, cap=cap),
        grid=(1,),
        in_specs=[blk],
        out_specs=blk,
        out_shape=jax.ShapeDtypeStruct((batch * ne, seq_len), jnp.int32),
        compiler_params=_cparams(("arbitrary",)),
        name="route",
    )(aff_t.reshape(batch * ne, seq_len))
    return slots.reshape(batch, ne, seq_len)


def _gather_kernel(slot_ref, aff_ref, h_ref, xs_ref, gate_ref, *, cap):
    hb = h_ref[...].astype(BF16)
    ne, seq_len = slot_ref.shape
    rid = lax.broadcasted_iota(jnp.int32, (cap, seq_len), 0)
    for e in range(ne):
        pick = slot_ref[e:e + 1, :] == rid
        xs_ref[e] = _dot(jnp.where(pick, 1.0, 0.0).astype(BF16), hb).astype(BF16)
        gate_ref[e] = jnp.sum(jnp.where(pick, aff_ref[e:e + 1, :], 0.0), axis=-1, keepdims=True)


def _gather(slots, aff_t, h2, cap):
    batch, ne, seq_len = slots.shape
    d = h2.shape[1]
    blk = pl.BlockSpec((None, ne, seq_len), lambda b: (b, 0, 0))
    return pl.pallas_call(
        functools.partial(_gather_kernel, cap=cap),
        grid=(batch,),
        in_specs=[blk, blk, pl.BlockSpec((seq_len, d), lambda b: (b, 0))],
        out_specs=[pl.BlockSpec((ne, None, cap, d), lambda b: (0, b, 0, 0)),
                   pl.BlockSpec((ne, None, cap, 1), lambda b: (0, b, 0, 0))],
        out_shape=[jax.ShapeDtypeStruct((ne, batch, cap, d), BF16), jax.ShapeDtypeStruct((ne, batch, cap, 1), F32)],
        compiler_params=_cparams(("parallel",)),
        name="gather",
    )(slots, aff_t, h2)


def _ffn_kernel(xs_ref, gate_ref, wg_ref, wu_ref, wd_ref, y_ref, acc_ref, g_buf, u_buf, a_buf, *, n_chunks):
    c = pl.program_id(1)
    parts = g_buf.shape[0]
    pr = xs_ref.shape[0] // parts
    rows = lambda r: slice(r * pr, (r + 1) * pr)

    def chunk(first, final):
        wg, wu, wd = wg_ref[...].astype(BF16), wu_ref[...].astype(BF16), wd_ref[...].astype(BF16)

        def gate_up(r):
            xs = xs_ref[rows(r), :]
            g_buf[r] = _dot(xs, wg)
            u_buf[r] = _dot(xs, wu)

        def down(r):
            a_buf[r] = (_silu(g_buf[r]) * u_buf[r]).astype(BF16)
            part = _dot(a_buf[r], wd)
            total = part if first else acc_ref[rows(r), :] + part
            if final:
                y_ref[rows(r), :] = (total * gate_ref[rows(r), :]).astype(BF16)
            else:
                acc_ref[rows(r), :] = total

        gate_up(0)
        for r in range(parts):
            if r + 1 < parts:
                gate_up(r + 1)
            down(r)

    if n_chunks == 1:
        chunk(True, True)
    else:
        pl.when(c == 0)(lambda: chunk(True, False))
        if n_chunks > 2:
            pl.when((c > 0) & (c < n_chunks - 1))(lambda: chunk(False, False))
        pl.when(c == n_chunks - 1)(lambda: chunk(False, True))


def _ffn(xs, gates, w_gate, w_up, w_down, fc):
    ne, rows, d = xs.shape
    dff = w_gate.shape[2]
    parts = 2 if rows % 32 == 0 else 1
    return pl.pallas_call(
        functools.partial(_ffn_kernel, n_chunks=dff // fc),
        grid=(ne, dff // fc),
        in_specs=[pl.BlockSpec((None, rows, d), lambda e, c: (e, 0, 0)),
                  pl.BlockSpec((None, rows, 1), lambda e, c: (e, 0, 0)),
                  pl.BlockSpec((None, d, fc), lambda e, c: (e, 0, c)),
                  pl.BlockSpec((None, d, fc), lambda e, c: (e, 0, c)),
                  pl.BlockSpec((None, fc, d), lambda e, c: (e, c, 0))],
        out_specs=pl.BlockSpec((None, rows, d), lambda e, c: (e, 0, 0)),
        out_shape=jax.ShapeDtypeStruct((ne, rows, d), BF16),
        scratch_shapes=[pltpu.VMEM((rows, d), F32), pltpu.VMEM((parts, rows // parts, fc), F32),
                        pltpu.VMEM((parts, rows // parts, fc), F32), pltpu.VMEM((parts, rows // parts, fc), BF16)],
        compiler_params=_cparams(("parallel", "arbitrary")),
        name="ffn",
    )(xs, gates, w_gate, w_up, w_down)


def _combine_kernel(slot_ref, y_ref, h_ref, g_ref, b_ref, o_ref, moe_buf, *, cap, parts):
    ne, tl = slot_ref.shape
    pr = tl // parts
    rows = lambda r: slice(r * pr, (r + 1) * pr)
    slot_t = slot_ref[...].astype(F32).T
    lane = lax.broadcasted_iota(jnp.int32, (pr, cap), 1).astype(F32)
    y = y_ref[...].reshape(ne * cap, y_ref.shape[-1])

    def scatter(r):
        place = jnp.concatenate(
            [jnp.where(slot_t[rows(r), e:e + 1] == lane, 1.0, 0.0).astype(BF16) for e in range(ne)], axis=1)
        moe_buf[r % 2] = _dot(place, y)

    def normalize(r):
        o_ref[rows(r), :] = _layer_norm(DEEPNORM_ALPHA * h_ref[rows(r), :] + moe_buf[r % 2], g_ref[...], b_ref[...])

    scatter(0)
    for r in range(parts):
        if r + 1 < parts:
            scatter(r + 1)
        normalize(r)


def _combine(slots, y, h2, g, b, cap, tl):
    batch, ne, seq_len = slots.shape
    d = h2.shape[1]
    tiles = seq_len // tl
    parts = 4 if tl % 64 == 0 else 1
    row = pl.BlockSpec((tl, d), lambda bi, i: (bi * tiles + i, 0))
    const = pl.BlockSpec((1, d), lambda bi, i: (0, 0))
    return pl.pallas_call(
        functools.partial(_combine_kernel, cap=cap, parts=parts),
        grid=(batch, tiles),
        in_specs=[pl.BlockSpec((None, ne, tl), lambda bi, i: (bi, 0, i)),
                  pl.BlockSpec((ne, None, cap, d), lambda bi, i: (0, bi, 0, 0)),
                  row, const, const],
        out_specs=row,
        out_shape=jax.ShapeDtypeStruct((batch * seq_len, d), F32),
        scratch_shapes=[pltpu.VMEM((2, tl // parts, d), F32)],
        compiler_params=_cparams(("parallel", "parallel")),
        name="combine",
    )(slots, y, h2, g, b)


def kernel(x, mem, emb_ln_g, emb_ln_b, w_in, hg_lb_logits, hg_norm_g, da_lambda_q1, da_lambda_k1, da_lambda_q2,
           da_lambda_k2, da_subln_g, w_mix_out, ln1_g, ln1_b, xa_wq, xa_wk, xa_wv, xa_wo, ln2_g, ln2_b, w_router,
           w_gate, w_up, w_down, ln3_g, ln3_b):
    batch, seq_len, d = x.shape
    assert w_in.shape[0] == DEPTH and seq_len % HG_CHUNK == 0 and seq_len % LANES_V7X == 0
    n = batch * seq_len
    cap = EC_FACTOR * seq_len // N_EXPERTS
    vec = lambda a: a.reshape(1, -1)
    bf = lambda a: a.astype(BF16)

    t = _tiles(seq_len)

    h0, q, v, gt, ff, fb, dq, dk, dv = _inproj(
        x.reshape(n, d), vec(emb_ln_g), vec(emb_ln_b), bf(w_in[0]), hg_lb_logits, seq_len, t.inproj_rows)
    hg = _hgrn(q, v, gt, ff, fb, hg_norm_g[0], batch, seq_len)
    lam_vecs = jnp.stack([da_lambda_q1[0], da_lambda_k1[0], da_lambda_q2[0], da_lambda_k2[0]])
    da = _dattn(dq, dk, dv, lam_vecs, da_subln_g[0], batch, seq_len, t.dattn_queries, t.dattn_keys)
    kx, vx = _kvproj(mem, bf(xa_wk[0]), bf(xa_wv[0]))
    h2, aff_t = _mixattn(hg.reshape(n, -1), da.reshape(n, -1), h0, bf(w_mix_out[0]), ln1_g, ln1_b, kx, vx,
                         bf(xa_wq[0]), bf(xa_wo[0]), ln2_g, ln2_b, bf(w_router[0].T), batch, seq_len, t.mixattn_rows)

    slots = _route(aff_t, cap)
    xs, gates = _gather(slots, aff_t, h2, cap)
    y = _ffn(xs.reshape(N_EXPERTS, batch * cap, d), gates.reshape(N_EXPERTS, batch * cap, 1),
             w_gate[0], w_up[0], w_down[0], t.ffn_cols)
    out = _combine(slots, y.reshape(N_EXPERTS, batch, cap, d), h2, ln3_g, ln3_b, cap, t.combine_rows)
    return out.reshape(batch, seq_len, d)
```
